```python
import math
import jax
import jax.numpy as jnp
from jax import lax
import numpy as np

D_MODEL = 1024
BATCH = 16
SEQ = 2048
DEPTH = 1

RET_HEADS = 4
RET_DK = 128
RET_DV = D_MODEL // RET_HEADS
GDN_HEADS = 4
GDN_DK = 128
GDN_DV = D_MODEL // GDN_HEADS
CHUNK = 64
CONV_K = 4
N_GROUPS = 4
EXPERTS_PER_GROUP = 4
TOP_K = 2
D_FF_EXPERT = 512
ROPE_BASE = 10000.0
NORM_EPS = 1e-6
L2_EPS = 1e-6
N_MOD = 6

RET_QK = RET_HEADS * RET_DK
RET_V = RET_HEADS * RET_DV
GDN_QK = GDN_HEADS * GDN_DK
GDN_V = GDN_HEADS * GDN_DV
IN_WIDTHS = (RET_QK, RET_QK, RET_V, RET_V, GDN_QK, GDN_QK, GDN_V, GDN_V, GDN_HEADS, GDN_HEADS, D_MODEL, D_MODEL)
D_IN = sum(IN_WIDTHS)
F32 = jnp.float32

kernel_name = 'hybrid_retention_gdn_hmoe_block'


def _rms_norm(x, w=None):
    xf = x.astype(F32)
    y = xf * lax.rsqrt(jnp.mean(xf * xf, axis=-1, keepdims=True) + NORM_EPS)
    if w is not None:
        y = y * w.astype(F32)
    return y.astype(x.dtype)


def _l2_normalize(x):
    xf = x.astype(F32)
    return xf * lax.rsqrt(jnp.sum(xf * xf, axis=-1, keepdims=True) + L2_EPS)


def _split_columns(t, widths):
    out, start = [], 0
    for w in widths:
        out.append(t[..., start:start + w])
        start += w
    return out


def _rope(x):
    d = x.shape[-1]
    half = d // 2
    inv_freq = 1.0 / (ROPE_BASE ** (jnp.arange(half, dtype=F32) / half))
    ang = jnp.arange(x.shape[1], dtype=F32)[:, None] * inv_freq[None, :]
    cos = jnp.cos(ang)[None, :, None, :]
    sin = jnp.sin(ang)[None, :, None, :]
    xf = x.astype(F32)
    x1, x2 = xf[..., :half], xf[..., half:]
    return jnp.concatenate([x1 * cos - x2 * sin, x1 * sin + x2 * cos], axis=-1)


def _to_chunks(t):
    b, s, h, d = t.shape
    return t.reshape(b, s // CHUNK, CHUNK, h, d).transpose(0, 3, 1, 2, 4)


def _from_chunks(t):
    b, h, n, c, d = t.shape
    return t.transpose(0, 2, 3, 1, 4).reshape(b, n * c, h, d)


def _causal_conv(x, w):
    s = x.shape[1]
    xp = jnp.pad(x, ((0, 0), (CONV_K - 1, 0), (0, 0)))
    out = xp[:, 0:s] * w[0]
    for i in range(1, CONV_K):
        out = out + xp[:, i:i + s] * w[i]
    return out


def _retention_chunkwise(q, k, v):
    n_heads, dk = q.shape[2], q.shape[3]
    q = _to_chunks(q.astype(F32))
    k = _to_chunks(k.astype(F32)) * (dk ** -0.5)
    v = _to_chunks(v.astype(F32))
    b, h, _, _, dv = v.shape
    log_gamma = jnp.log(1.0 - 2.0 ** (-5.0 - jnp.arange(n_heads, dtype=F32)))
    pos = jnp.arange(CHUNK, dtype=F32)
    rel = pos[:, None] - pos[None, :]
    causal = rel >= 0
    intra_decay = jnp.where(causal, jnp.exp(jnp.where(causal, rel, 0.0) * log_gamma[:, None, None]), 0.0)
    q_decay = jnp.exp((pos + 1.0) * log_gamma[:, None])
    k_decay = jnp.exp((CHUNK - 1.0 - pos) * log_gamma[:, None])
    chunk_decay = jnp.exp(CHUNK * log_gamma)[None, :, None, None]
    scores = jnp.einsum('bhnid,bhnjd->bhnij', q, k) * intra_decay[None, :, None]
    o_intra = jnp.einsum('bhnij,bhnjv->bhniv', scores, v)
    q_in = q * q_decay[None, :, None, :, None]
    k_in = k * k_decay[None, :, None, :, None]

    def step(state, xs):
        q_n, k_n, v_n = xs
        o_n = jnp.einsum('bhcd,bhdv->bhcv', q_n, state)
        state = state * chunk_decay + jnp.einsum('bhcd,bhcv->bhdv', k_n, v_n)
        return state, o_n

    state0 = jnp.zeros((b, h, dk, dv), F32)
    _, o_inter = lax.scan(step, state0, (jnp.moveaxis(q_in, 2, 0), jnp.moveaxis(k_in, 2, 0), jnp.moveaxis(v, 2, 0)))
    return _from_chunks(o_intra + jnp.moveaxis(o_inter, 0, 2))


def _gated_delta_rule_chunked(q, k, v, log_decay, beta):
    dk = q.shape[-1]
    q = _to_chunks(q.astype(F32)) * (dk ** -0.5)
    k = _to_chunks(k.astype(F32))
    v = _to_chunks(v.astype(F32))
    b, h, n, c, dv = v.shape
    g = log_decay.astype(F32).reshape(b, n, c, h).transpose(0, 3, 1, 2)
    bt = beta.astype(F32).reshape(b, n, c, h).transpose(0, 3, 1, 2)
    g_cum = jnp.cumsum(g, axis=-1)
    idx = jnp.arange(CHUNK)
    causal = idx[:, None] >= idx[None, :]
    strict = idx[:, None] > idx[None, :]
    decay = jnp.exp(jnp.where(causal, g_cum[..., :, None] - g_cum[..., None, :], -jnp.inf))
    k_beta = k * bt[..., None]
    v_beta = v * bt[..., None]
    a_mat = jnp.where(strict, jnp.einsum('bhnid,bhnjd->bhnij', k_beta, k) * decay, 0.0) + jnp.eye(CHUNK, dtype=F32)
    rhs = jnp.concatenate([v_beta, k_beta * jnp.exp(g_cum)[..., None]], axis=-1)
    sol = lax.linalg.triangular_solve(a_mat, rhs, left_side=True, lower=True, unit_diagonal=True)
    u, w = sol[..., :dv], sol[..., dv:]
    qk = jnp.einsum('bhnid,bhnjd->bhnij', q, k) * decay
    q_g = q * jnp.exp(g_cum)[..., None]
    g_last = g_cum[..., -1]
    k_g = k * jnp.exp(g_last[..., None] - g_cum)[..., None]

    def step(state, xs):
        u_n, w_n, qk_n, q_n, k_n, gl_n = xs
        v_new = u_n - jnp.einsum('bhcd,bhdv->bhcv', w_n, state)
        o_n = jnp.einsum('bhcd,bhdv->bhcv', q_n, state) + jnp.einsum('bhij,bhjv->bhiv', qk_n, v_new)
        state = state * jnp.exp(gl_n)[..., None, None] + jnp.einsum('bhcd,bhcv->bhdv', k_n, v_new)
        return state, o_n

    xs = tuple(jnp.moveaxis(t, 2, 0) for t in (u, w, qk, q_g, k_g, g_last))
    state0 = jnp.zeros((b, h, dk, dv), F32)
    _, o = lax.scan(step, state0, xs)
    return _from_chunks(jnp.moveaxis(o, 0, 2))


def _hybrid_mixer(xm, w_in, conv_w, a_log, dt_bias, gdn_norm_w, w_out):
    b, s, _ = xm.shape
    proj = jnp.einsum('bsd,de->bse', xm, w_in)
    rq, rk, rv, rg, gq, gk, gv, gz, ga, gb, gate_a, gate_b = _split_columns(proj, IN_WIDTHS)
    q_r = _rope(rq.reshape(b, s, RET_HEADS, RET_DK))
    k_r = _rope(rk.reshape(b, s, RET_HEADS, RET_DK))
    o_r = _retention_chunkwise(q_r, k_r, rv.reshape(b, s, RET_HEADS, RET_DV))
    o_r = _rms_norm(o_r).reshape(b, s, RET_V).astype(xm.dtype)
    y_ret = jax.nn.silu(rg) * o_r
    qkv = jax.nn.silu(_causal_conv(jnp.concatenate([gq, gk, gv], axis=-1), conv_w))
    q_g, k_g, v_g = _split_columns(qkv, (GDN_QK, GDN_QK, GDN_V))
    q_g = _l2_normalize(q_g.reshape(b, s, GDN_HEADS, GDN_DK))
    k_g = _l2_normalize(k_g.reshape(b, s, GDN_HEADS, GDN_DK))
    beta = jax.nn.sigmoid(gb.astype(F32))
    log_decay = -jnp.exp(a_log.astype(F32)) * jax.nn.softplus(ga.astype(F32) + dt_bias.astype(F32))
    o_g = _gated_delta_rule_chunked(q_g, k_g, v_g.reshape(b, s, GDN_HEADS, GDN_DV), log_decay, beta)
    o_g = _rms_norm(o_g, gdn_norm_w) * jax.nn.silu(gz.reshape(b, s, GDN_HEADS, GDN_DV).astype(F32))
    y_gdn = o_g.reshape(b, s, GDN_V).astype(xm.dtype)
    merged = jax.nn.sigmoid(gate_a) * y_ret + jax.nn.sigmoid(gate_b) * y_gdn
    return jnp.einsum('bsd,de->bse', merged, w_out)


def _hierarchical_moe(xm, w_group, b_group, w_router, b_router, w_gate, w_up, w_down):
    b, s, d = xm.shape
    xf = xm.reshape(-1, d)
    t = xf.shape[0]
    group_prob = jax.nn.softmax((xf @ w_group).astype(F32) + b_group.astype(F32), axis=-1)
    g_top, g_idx = lax.top_k(group_prob, 1)
    exp_logits = (xf @ w_router).astype(F32).reshape(t, N_GROUPS, EXPERTS_PER_GROUP) + b_router.astype(F32)
    exp_logits = jnp.take_along_axis(exp_logits, g_idx[:, :, None], axis=1)[:, 0]
    exp_prob = jax.nn.softmax(exp_logits, axis=-1)
    e_top, e_idx = lax.top_k(exp_prob, TOP_K)
    e_w = e_top / jnp.sum(e_top, axis=-1, keepdims=True)
    within = jnp.sum(jax.nn.one_hot(e_idx, EXPERTS_PER_GROUP, dtype=F32) * e_w[..., None], axis=1)
    comb = jax.nn.one_hot(g_idx[:, 0], N_GROUPS, dtype=F32)[:, :, None] * (g_top[:, :, None] * within[:, None, :])
    comb = comb.astype(xm.dtype)
    out = jnp.zeros_like(xf)
    for gi in range(N_GROUPS):
        hg = jnp.einsum('td,edf->tef', xf, w_gate[gi])
        hu = jnp.einsum('td,edf->tef', xf, w_up[gi])
        act = jax.nn.silu(hg) * hu * comb[:, gi, :, None]
        out = out + jnp.einsum('tef,efd->td', act, w_down[gi])
    return out.reshape(b, s, d)


def setup_inputs(seed: int = 0) -> dict:
    key = jax.random.key(seed)
    ks = jax.random.split(key, 20)
    L, G, E, F = DEPTH, N_GROUPS, EXPERTS_PER_GROUP, D_FF_EXPERT

    def nrm(k, shape, scale):
        return jax.random.normal(k, shape, F32) * scale

    dt = jnp.exp(jax.random.uniform(ks[8], (L, GDN_HEADS), F32, math.log(1e-3), math.log(1e-1)))
    return {
        'x': nrm(ks[0], (BATCH, SEQ, D_MODEL), 1.0),
        'c': nrm(ks[1], (BATCH, D_MODEL), 1.0),
        'mod_w': nrm(ks[2], (L, D_MODEL, N_MOD * D_MODEL), 0.5 * D_MODEL ** -0.5),
        'mod_b': nrm(ks[3], (L, N_MOD * D_MODEL), 0.02),
        'norm_mix_w': 1.0 + nrm(ks[4], (L, D_MODEL), 0.02),
        'w_in': nrm(ks[5], (L, D_MODEL, D_IN), D_MODEL ** -0.5),
        'gdn_conv_w': nrm(ks[6], (L, CONV_K, 2 * GDN_QK + GDN_V), CONV_K ** -0.5),
        'gdn_a_log': jnp.log(jax.random.uniform(ks[7], (L, GDN_HEADS), F32, 1.0, 16.0)),
        'gdn_dt_bias': dt + jnp.log(-jnp.expm1(-dt)),
        'gdn_norm_w': 1.0 + nrm(ks[9], (L, GDN_DV), 0.02),
        'w_out': nrm(ks[10], (L, D_MODEL, D_MODEL), D_MODEL ** -0.5),
        'norm_ffn_w': 1.0 + nrm(ks[11], (L, D_MODEL), 0.02),
        'w_group': nrm(ks[12], (L, D_MODEL, G), D_MODEL ** -0.5),
        'b_group': nrm(ks[13], (L, G), 0.01),
        'w_router': nrm(ks[14], (L, D_MODEL, G * E), D_MODEL ** -0.5),
        'b_router': nrm(ks[15], (L, G, E), 0.01),
        'w_gate': nrm(ks[16], (L, G, E, D_MODEL, F), D_MODEL ** -0.5),
        'w_up': nrm(ks[17], (L, G, E, D_MODEL, F), D_MODEL ** -0.5),
        'w_down': nrm(ks[18], (L, G, E, F, D_MODEL), F ** -0.5),
        'norm_out_w': 1.0 + nrm(ks[19], (D_MODEL,), 0.02),
    }


def reference(x, c, mod_w, mod_b, norm_mix_w, w_in, gdn_conv_w, gdn_a_log, gdn_dt_bias, gdn_norm_w, w_out,
              norm_ffn_w, w_group, b_group, w_router, b_router, w_gate, w_up, w_down, norm_out_w):
    h = x
    for l in range(DEPTH):
        mod = jax.nn.silu(c) @ mod_w[l] + mod_b[l]
        shift_m, scale_m, gate_m, shift_f, scale_f, gate_f = jnp.split(mod[:, None, :], N_MOD, axis=-1)
        xm = _rms_norm(h, norm_mix_w[l]) * (1.0 + scale_m) + shift_m
        h = h + gate_m * _hybrid_mixer(xm, w_in[l], gdn_conv_w[l], gdn_a_log[l], gdn_dt_bias[l], gdn_norm_w[l], w_out[l])
        xf = _rms_norm(h, norm_ffn_w[l]) * (1.0 + scale_f) + shift_f
        h = h + gate_f * _hierarchical_moe(xf, w_group[l], b_group[l], w_router[l], b_router[l], w_gate[l], w_up[l], w_down[l])
    return _rms_norm(h, norm_out_w)
```

```python
import functools

import jax
import jax.numpy as jnp
from jax import lax
from jax.experimental import pallas as pl
from jax.experimental.pallas import tpu as pltpu

F32 = jnp.float32
BF16 = jnp.bfloat16
HIGHEST = lax.Precision.HIGHEST

RET_HEADS = 4
RET_DK = 128
RET_DV = 256
GDN_HEADS = 4
GDN_DK = 128
GDN_DV = 256
GDN_CHUNK = 64
CONV_K = 4
N_GROUPS = 4
EXPERTS_PER_GROUP = 4
N_EXPERTS = N_GROUPS * EXPERTS_PER_GROUP
D_FF_EXPERT = 512
ROPE_BASE = 10000.0
NORM_EPS = 1e-6
L2_EPS = 1e-6
N_MOD = 6
LANES = 128
SEQ_TILE = 256
VMEM_LIMIT = 48 * 1024 * 1024

COL_RQ, COL_RK, COL_RV, COL_RG = 0, 512, 1024, 2048
COL_GQ, COL_GK, COL_GV, COL_GZ = 3072, 3584, 4096, 5120
COL_GATE_A, COL_GATE_B = 6144, 7168
N_MAIN = 8192


def _silu(x):
    return x * jax.nn.sigmoid(x)


def _dot(a, b, **kw):
    return jnp.dot(a, b, preferred_element_type=F32, **kw)


def _dot_nt(a, b, **kw):
    return lax.dot_general(a, b, (((1,), (1,)), ((), ())), preferred_element_type=F32, **kw)


def _dot_tn(a, b, **kw):
    return lax.dot_general(a, b, (((0,), (0,)), ((), ())), preferred_element_type=F32, **kw)


def _params(*sem):
    return pltpu.CompilerParams(dimension_semantics=sem, vmem_limit_bytes=VMEM_LIMIT)


def _mod_kernel(c_ref, w_ref, b_ref, o_ref):
    a = _silu(c_ref[...])
    o_ref[...] = _dot(a, w_ref[...], precision=HIGHEST) + b_ref[...]


def _mod(c, mod_w, mod_b):
    b, d = c.shape
    n = mod_w.shape[1]
    tn = d
    return pl.pallas_call(
        _mod_kernel,
        grid=(n // tn,),
        in_specs=[
            pl.BlockSpec((b, d), lambda j: (0, 0)),
            pl.BlockSpec((d, tn), lambda j: (0, j)),
            pl.BlockSpec((1, tn), lambda j: (0, j)),
        ],
        out_specs=pl.BlockSpec((b, tn), lambda j: (0, j)),
        out_shape=jax.ShapeDtypeStruct((b, n), F32),
        compiler_params=_params("arbitrary"),
        name="mod",
    )(c, mod_w, mod_b.reshape(1, n))


def _inproj_kernel(x_ref, nw_ref, shift_ref, scale_ref, w_ref, ws_ref, o_ref, og_ref, xb_ref):
    @pl.when(pl.program_id(1) == 0)
    def _():
        x = x_ref[...]
        y = x * lax.rsqrt(jnp.mean(x * x, axis=-1, keepdims=True) + NORM_EPS) * nw_ref[...]
        xm = y * (1.0 + scale_ref[...]) + shift_ref[...]
        xb_ref[...] = xm.astype(BF16)
        og_ref[...] = _dot(xm, ws_ref[...], precision=HIGHEST)

    o_ref[...] = _dot(xb_ref[...], w_ref[...]).astype(BF16)


def _inproj(x2, mod4, norm_w, w_main, w_small, seq, tm=1024, tn=1024):
    t, d = x2.shape
    n = w_main.shape[1]
    per_b = seq // tm
    return pl.pallas_call(
        _inproj_kernel,
        grid=(t // tm, n // tn),
        in_specs=[
            pl.BlockSpec((tm, d), lambda i, j: (i, 0)),
            pl.BlockSpec((1, d), lambda i, j: (0, 0)),
            pl.BlockSpec((None, None, 1, d), lambda i, j: (i // per_b, 0, 0, 0)),
            pl.BlockSpec((None, None, 1, d), lambda i, j: (i // per_b, 1, 0, 0)),
            pl.BlockSpec((d, tn), lambda i, j: (0, j)),
            pl.BlockSpec((d, LANES), lambda i, j: (0, 0)),
        ],
        out_specs=[
            pl.BlockSpec((tm, tn), lambda i, j: (i, j)),
            pl.BlockSpec((tm, LANES), lambda i, j: (i, 0)),
        ],
        out_shape=[
            jax.ShapeDtypeStruct((t, n), BF16),
            jax.ShapeDtypeStruct((t, LANES), F32),
        ],
        scratch_shapes=[pltpu.VMEM((tm, d), BF16)],
        compiler_params=_params("arbitrary", "arbitrary"),
        name="inproj",
    )(x2, norm_w.reshape(1, d), mod4, mod4, w_main, w_small)


def _ret_kernel(lg_ref, q_ref, k_ref, v_ref, rg_ref, ga_ref, cos_ref, sin_ref, o_ref,
                state_ref, intra_ref, qd_ref, kd_ref):
    c = SEQ_TILE
    first = jnp.logical_and(pl.program_id(0) == 0, pl.program_id(1) == 0)

    @pl.when(first)
    def _():
        row = lax.broadcasted_iota(jnp.int32, (c, c), 0)
        col = lax.broadcasted_iota(jnp.int32, (c, c), 1)
        rel = (row - col).astype(F32)
        causal = row >= col
        pos = lax.broadcasted_iota(jnp.int32, (c, RET_DK), 0).astype(F32)
        for h in range(RET_HEADS):
            lg = lg_ref[h:h + 1, :]
            intra_ref[h] = jnp.where(causal, jnp.exp(jnp.where(causal, rel, 0.0) * lg), 0.0)
            qd_ref[h] = jnp.exp((pos + 1.0) * lg[:, :RET_DK])
            kd_ref[h] = jnp.exp((c - 1.0 - pos) * lg[:, :RET_DK])

    @pl.when(pl.program_id(1) == 0)
    def _():
        state_ref[...] = jnp.zeros_like(state_ref)

    cos = cos_ref[...]
    sin = sin_ref[...]
    for h in range(RET_HEADS):
        qs = slice(h * RET_DK, (h + 1) * RET_DK)
        vs = slice(h * RET_DV, (h + 1) * RET_DV)
        qr = q_ref[:, qs].astype(F32)
        kr = k_ref[:, qs].astype(F32)
        q = qr * cos + pltpu.roll(qr, RET_DK // 2, 1) * sin
        k = (kr * cos + pltpu.roll(kr, RET_DK // 2, 1) * sin) * (RET_DK ** -0.5)
        v = v_ref[:, vs]
        state = state_ref[h]
        chunk_decay = jnp.exp(float(c) * lg_ref[h:h + 1, :])
        scores = _dot_nt(q.astype(BF16), k.astype(BF16)) * intra_ref[h]
        o = _dot(scores.astype(BF16), v) + _dot((q * qd_ref[h]).astype(BF16), state.astype(BF16))
        state_ref[h] = state * chunk_decay + _dot_tn((k * kd_ref[h]).astype(BF16), v)
        o = o * lax.rsqrt(jnp.mean(o * o, axis=-1, keepdims=True) + NORM_EPS)
        y = _silu(rg_ref[:, vs].astype(F32)) * o
        o_ref[:, vs] = (jax.nn.sigmoid(ga_ref[:, vs].astype(F32)) * y).astype(BF16)


def _retention(proj3, cos_t, sin_t, log_gamma):
    b, s, _ = proj3.shape
    ts = SEQ_TILE
    qk_w = RET_HEADS * RET_DK
    v_w = RET_HEADS * RET_DV
    return pl.pallas_call(
        _ret_kernel,
        grid=(b, s // ts),
        in_specs=[
            pl.BlockSpec((RET_HEADS, RET_DV), lambda i, j: (0, 0)),
            pl.BlockSpec((None, ts, qk_w), lambda i, j: (i, j, COL_RQ // qk_w)),
            pl.BlockSpec((None, ts, qk_w), lambda i, j: (i, j, COL_RK // qk_w)),
            pl.BlockSpec((None, ts, v_w), lambda i, j: (i, j, COL_RV // v_w)),
            pl.BlockSpec((None, ts, v_w), lambda i, j: (i, j, COL_RG // v_w)),
            pl.BlockSpec((None, ts, v_w), lambda i, j: (i, j, COL_GATE_A // v_w)),
            pl.BlockSpec((ts, RET_DK), lambda i, j: (j, 0)),
            pl.BlockSpec((ts, RET_DK), lambda i, j: (j, 0)),
        ],
        out_specs=pl.BlockSpec((None, ts, v_w), lambda i, j: (i, j, 0)),
        out_shape=jax.ShapeDtypeStruct((b, s, v_w), BF16),
        scratch_shapes=[
            pltpu.VMEM((RET_HEADS, RET_DK, RET_DV), F32),
            pltpu.VMEM((RET_HEADS, ts, ts), F32),
            pltpu.VMEM((RET_HEADS, ts, RET_DK), F32),
            pltpu.VMEM((RET_HEADS, ts, RET_DK), F32),
        ],
        compiler_params=_params("arbitrary", "arbitrary"),
        name="retention",
    )(log_gamma, proj3, proj3, proj3, proj3, proj3, cos_t, sin_t)


def _unit_lower_inverse(a):
    n = a.shape[0]
    row = lax.broadcasted_iota(jnp.int32, (n, n), 0)
    col = lax.broadcasted_iota(jnp.int32, (n, n), 1)
    p = jnp.where(row == col, 1.0, 0.0) - a
    m = a
    steps = (n - 1).bit_length() - 1
    for _ in range(steps):
        m = _dot(m, m, precision=HIGHEST)
        p = p + _dot(p, m, precision=HIGHEST)
    return p


def _gdn_kernel(q_ref, k_ref, v_ref, z_ref, gb_ref, gates_ref, alog_ref, dtb_ref, cwq_ref, cwk_ref, cwv_ref,
                nw_ref, o_ref, state_ref, tail_ref):
    ts = SEQ_TILE
    cc = GDN_CHUNK
    qk_w = GDN_HEADS * GDN_DK

    @pl.when(pl.program_id(1) == 0)
    def _():
        state_ref[...] = jnp.zeros_like(state_ref)
        tail_ref[...] = jnp.zeros_like(tail_ref)

    def conv_silu(cur, tail, cw_ref):
        ext = jnp.concatenate([tail, cur], axis=0)
        out = ext[8 - (CONV_K - 1):8 - (CONV_K - 1) + ts] * cw_ref[0:1, :]
        for i in range(1, CONV_K):
            d = CONV_K - 1 - i
            out = out + ext[8 - d:8 - d + ts] * cw_ref[i:i + 1, :]
        return _silu(out)

    q_raw = q_ref[...].astype(F32)
    k_raw = k_ref[...].astype(F32)
    v_raw = v_ref[...].astype(F32)
    q_all = conv_silu(q_raw, tail_ref[:, 0:qk_w], cwq_ref)
    k_all = conv_silu(k_raw, tail_ref[:, qk_w:2 * qk_w], cwk_ref)
    v_all = conv_silu(v_raw, tail_ref[:, 2 * qk_w:], cwv_ref)
    tail_ref[:, 0:qk_w] = q_raw[ts - 8:]
    tail_ref[:, qk_w:2 * qk_w] = k_raw[ts - 8:]
    tail_ref[:, 2 * qk_w:] = v_raw[ts - 8:]

    gates = gates_ref[...]
    x = gates + dtb_ref[...]
    softplus = jnp.maximum(x, 0.0) + jnp.log1p(jnp.exp(-jnp.abs(x)))
    g_all = -jnp.exp(alog_ref[...]) * softplus
    beta_all = jax.nn.sigmoid(gates)
    row = lax.broadcasted_iota(jnp.int32, (ts, ts), 0)
    col = lax.broadcasted_iota(jnp.int32, (ts, ts), 1)
    same_chunk = (row // cc) == (col // cc)
    tri_lower = jnp.where(jnp.logical_and(same_chunk, row >= col), 1.0, 0.0)
    tri_upper = jnp.where(jnp.logical_and(same_chunk, row <= col), 1.0, 0.0)
    gc_col_all = _dot(tri_lower, g_all, precision=HIGHEST)
    gc_row_all = _dot_tn(g_all, tri_upper, precision=HIGHEST)

    ci = lax.broadcasted_iota(jnp.int32, (cc, cc), 0)
    cj = lax.broadcasted_iota(jnp.int32, (cc, cc), 1)
    causal = ci >= cj
    strict = ci > cj

    for h in range(GDN_HEADS):
        qs = slice(h * GDN_DK, (h + 1) * GDN_DK)
        vs = slice(h * GDN_DV, (h + 1) * GDN_DV)
        qh = q_all[:, qs]
        kh = k_all[:, qs]
        qn = qh * lax.rsqrt(jnp.sum(qh * qh, axis=-1, keepdims=True) + L2_EPS) * (GDN_DK ** -0.5)
        kn = kh * lax.rsqrt(jnp.sum(kh * kh, axis=-1, keepdims=True) + L2_EPS)
        vh = v_all[:, vs]
        state = state_ref[h]
        outs = []
        for n in range(ts // cc):
            rs = slice(n * cc, (n + 1) * cc)
            gc = gc_col_all[rs, h:h + 1]
            gcr = gc_row_all[h:h + 1, rs]
            beta = beta_all[rs, GDN_HEADS + h:GDN_HEADS + h + 1]
            q_c, k_c, v_c = qn[rs], kn[rs], vh[rs]
            decay = jnp.exp(jnp.where(causal, gc - gcr, -jnp.inf))
            k_beta = k_c * beta
            v_beta = v_c * beta
            kk = _dot_nt(k_beta.astype(BF16), k_c.astype(BF16))
            a_mat = jnp.where(strict, kk * decay, 0.0)
            t_inv = _unit_lower_inverse(a_mat)
            egc = jnp.exp(gc)
            u = _dot(t_inv, v_beta, precision=HIGHEST)
            w = _dot(t_inv, k_beta * egc, precision=HIGHEST)
            qk = _dot_nt(q_c.astype(BF16), k_c.astype(BF16)) * decay
            state_b = state.astype(BF16)
            v_new = u - _dot(w.astype(BF16), state_b)
            o_c = _dot((q_c * egc).astype(BF16), state_b) + _dot(qk.astype(BF16), v_new.astype(BF16))
            g_last = gc[cc - 1:cc, :]
            k_g = k_c * jnp.exp(g_last - gc)
            state = state * jnp.exp(g_last) + _dot_tn(k_g.astype(BF16), v_new.astype(BF16))
            outs.append(o_c)
        state_ref[h] = state
        o = jnp.concatenate(outs, axis=0)
        o = o * lax.rsqrt(jnp.mean(o * o, axis=-1, keepdims=True) + NORM_EPS) * nw_ref[...]
        o = o * _silu(z_ref[:, vs].astype(F32))
        o_ref[:, vs] = (jax.nn.sigmoid(gb_ref[:, vs].astype(F32)) * o).astype(BF16)


def _gdn(proj3, gates3, alog_row, dtb_row, conv_w, norm_w):
    b, s, _ = proj3.shape
    ts = SEQ_TILE
    qk_w = GDN_HEADS * GDN_DK
    v_w = GDN_HEADS * GDN_DV
    return pl.pallas_call(
        _gdn_kernel,
        grid=(b, s // ts),
        in_specs=[
            pl.BlockSpec((None, ts, qk_w), lambda i, j: (i, j, COL_GQ // qk_w)),
            pl.BlockSpec((None, ts, qk_w), lambda i, j: (i, j, COL_GK // qk_w)),
            pl.BlockSpec((None, ts, v_w), lambda i, j: (i, j, COL_GV // v_w)),
            pl.BlockSpec((None, ts, v_w), lambda i, j: (i, j, COL_GZ // v_w)),
            pl.BlockSpec((None, ts, v_w), lambda i, j: (i, j, COL_GATE_B // v_w)),
            pl.BlockSpec((None, ts, LANES), lambda i, j: (i, j, 0)),
            pl.BlockSpec((1, LANES), lambda i, j: (0, 0)),
            pl.BlockSpec((1, LANES), lambda i, j: (0, 0)),
            pl.BlockSpec((CONV_K, qk_w), lambda i, j: (0, 0)),
            pl.BlockSpec((CONV_K, qk_w), lambda i, j: (0, 1)),
            pl.BlockSpec((CONV_K, v_w), lambda i, j: (0, 1)),
            pl.BlockSpec((1, GDN_DV), lambda i, j: (0, 0)),
        ],
        out_specs=pl.BlockSpec((None, ts, v_w), lambda i, j: (i, j, 0)),
        out_shape=jax.ShapeDtypeStruct((b, s, v_w), BF16),
        scratch_shapes=[
            pltpu.VMEM((GDN_HEADS, GDN_DK, GDN_DV), F32),
            pltpu.VMEM((8, 2 * qk_w + v_w), F32),
        ],
        compiler_params=_params("arbitrary", "arbitrary"),
        name="gdn",
    )(proj3, proj3, proj3, proj3, proj3, gates3, alog_row, dtb_row, conv_w, conv_w, conv_w,
      norm_w.reshape(1, GDN_DV))


def _outproj_kernel(ya_ref, yb_ref, x_ref, w_ref, gate_ref, nw_ref, shift_ref, scale_ref, wr_ref, br_ref,
                    h_ref, xf_ref, comb_ref):
    merged = (ya_ref[...].astype(F32) + yb_ref[...].astype(F32)).astype(BF16)
    h = x_ref[...] + gate_ref[...] * _dot(merged, w_ref[...])
    h_ref[...] = h
    y = h * lax.rsqrt(jnp.mean(h * h, axis=-1, keepdims=True) + NORM_EPS) * nw_ref[...]
    xf = y * (1.0 + scale_ref[...]) + shift_ref[...]
    xf_ref[...] = xf.astype(BF16)

    logits = _dot(xf, wr_ref[...], precision=HIGHEST) + br_ref[...]
    lane = lax.broadcasted_iota(jnp.int32, logits.shape, 1)
    neg = jnp.float32(-jnp.inf)

    def masked_softmax(mask):
        m = jnp.max(jnp.where(mask, logits, neg), axis=-1, keepdims=True)
        e = jnp.where(mask, jnp.exp(jnp.where(mask, logits, m) - m), 0.0)
        return e / jnp.sum(e, axis=-1, keepdims=True)

    def first_argmax(p, mask):
        top = jnp.max(jnp.where(mask, p, -1.0), axis=-1, keepdims=True)
        idx = jnp.min(jnp.where(jnp.logical_and(mask, p == top), lane, LANES), axis=-1, keepdims=True)
        return top, idx

    gmask = lane < N_GROUPS
    g_top, g_idx = first_argmax(masked_softmax(gmask), gmask)
    lo = N_GROUPS + EXPERTS_PER_GROUP * g_idx
    emask = jnp.logical_and(lane >= lo, lane < lo + EXPERTS_PER_GROUP)
    pe = masked_softmax(emask)
    top1, i1 = first_argmax(pe, emask)
    emask2 = jnp.logical_and(emask, lane != i1)
    top2, i2 = first_argmax(pe, emask2)
    denom = top1 + top2
    comb = jnp.where(lane == i1, g_top * (top1 / denom), jnp.where(lane == i2, g_top * (top2 / denom), 0.0))
    comb_ref[...] = comb


def _outproj(ya2, yb2, x2, w_out_b, mod4, norm_w, w_route, b_route, seq, tm=512):
    t, d = x2.shape
    per_b = seq // tm
    row_spec = pl.BlockSpec((tm, d), lambda i: (i, 0))

    def mod_spec(k):
        return pl.BlockSpec((None, None, 1, d), lambda i: (i // per_b, k, 0, 0))

    return pl.pallas_call(
        _outproj_kernel,
        grid=(t // tm,),
        in_specs=[
            row_spec, row_spec, row_spec,
            pl.BlockSpec((d, d), lambda i: (0, 0)),
            mod_spec(2),
            pl.BlockSpec((1, d), lambda i: (0, 0)),
            mod_spec(3),
            mod_spec(4),
            pl.BlockSpec((d, LANES), lambda i: (0, 0)),
            pl.BlockSpec((1, LANES), lambda i: (0, 0)),
        ],
        out_specs=[row_spec, row_spec, pl.BlockSpec((tm, LANES), lambda i: (i, 0))],
        out_shape=[
            jax.ShapeDtypeStruct((t, d), F32),
            jax.ShapeDtypeStruct((t, d), BF16),
            jax.ShapeDtypeStruct((t, LANES), F32),
        ],
        compiler_params=_params("arbitrary"),
        name="outproj",
    )(ya2, yb2, x2, w_out_b, mod4, norm_w.reshape(1, d), mod4, mod4, w_route, b_route)


def _moe_kernel(xf_ref, comb_ref, wg_ref, wu_ref, wd_ref, h_ref, gate_ref, nw_ref, o_ref, acc_ref):
    e = pl.program_id(1)

    @pl.when(e == 0)
    def _():
        acc_ref[...] = jnp.zeros_like(acc_ref)

    xf = xf_ref[...]
    comb = comb_ref[...]
    lane = lax.broadcasted_iota(jnp.int32, comb.shape, 1)
    wcol = jnp.sum(jnp.where(lane == N_GROUPS + e, comb, 0.0), axis=-1, keepdims=True)
    hg = _dot(xf, wg_ref[...])
    hu = _dot(xf, wu_ref[...])
    act = _silu(hg) * hu * wcol
    acc_ref[...] += _dot(act.astype(BF16), wd_ref[...])

    @pl.when(e == pl.num_programs(1) - 1)
    def _():
        h = h_ref[...] + gate_ref[...] * acc_ref[...]
        o_ref[...] = h * lax.rsqrt(jnp.mean(h * h, axis=-1, keepdims=True) + NORM_EPS) * nw_ref[...]


def _moe(xf2, comb, wg, wu, wd, h2, mod4, norm_out_w, seq, tm=1024):
    t, d = xf2.shape
    f = wg.shape[-1]
    per_b = seq // tm
    row_spec = pl.BlockSpec((tm, d), lambda i, e: (i, 0))
    return pl.pallas_call(
        _moe_kernel,
        grid=(t // tm, N_EXPERTS),
        in_specs=[
            row_spec,
            pl.BlockSpec((tm, LANES), lambda i, e: (i, 0)),
            pl.BlockSpec((None, d, f), lambda i, e: (e, 0, 0)),
            pl.BlockSpec((None, d, f), lambda i, e: (e, 0, 0)),
            pl.BlockSpec((None, f, d), lambda i, e: (e, 0, 0)),
            row_spec,
            pl.BlockSpec((None, None, 1, d), lambda i, e: (i // per_b, 5, 0, 0)),
            pl.BlockSpec((1, d), lambda i, e: (0, 0)),
        ],
        out_specs=row_spec,
        out_shape=jax.ShapeDtypeStruct((t, d), F32),
        scratch_shapes=[pltpu.VMEM((tm, d), F32)],
        compiler_params=_params("arbitrary", "arbitrary"),
        name="moe",
    )(xf2, comb, wg, wu, wd, h2, mod4, norm_out_w.reshape(1, d))


def _pad_lanes(a):
    return jnp.pad(a, ((0, 0), (0, LANES - a.shape[1])))


def _layer(h3, c, mod_w, mod_b, norm_mix_w, w_in, conv_w, a_log, dt_bias, gdn_norm_w, w_out, norm_ffn_w,
           w_group, b_group, w_router, b_router, w_gate, w_up, w_down, norm_out_w):
    b, s, d = h3.shape
    t = b * s
    x2 = h3.reshape(t, d)

    n_gate_cols = 2 * GDN_HEADS
    small_lo = COL_GATE_A
    w_main = jnp.concatenate([w_in[:, :small_lo], w_in[:, small_lo + n_gate_cols:]], axis=1).astype(BF16)
    w_small = _pad_lanes(w_in[:, small_lo:small_lo + n_gate_cols])
    w_route = _pad_lanes(jnp.concatenate([w_group, w_router], axis=1))
    b_route = _pad_lanes(jnp.concatenate([b_group, b_router.reshape(-1)])[None, :])
    alog_row = _pad_lanes(a_log[None, :])
    dtb_row = _pad_lanes(dt_bias[None, :])
    f = w_gate.shape[-1]
    wg = w_gate.reshape(N_EXPERTS, d, f).astype(BF16)
    wu = w_up.reshape(N_EXPERTS, d, f).astype(BF16)
    wd = w_down.reshape(N_EXPERTS, f, d).astype(BF16)

    half = RET_DK // 2
    inv_freq = 1.0 / (ROPE_BASE ** (jnp.arange(half, dtype=F32) / half))
    ang = jnp.arange(s, dtype=F32)[:, None] * inv_freq[None, :]
    cos_t = jnp.concatenate([jnp.cos(ang), jnp.cos(ang)], axis=1)
    sin_t = jnp.concatenate([-jnp.sin(ang), jnp.sin(ang)], axis=1)
    log_gamma = jnp.log(1.0 - 2.0 ** (-5.0 - jnp.arange(RET_HEADS, dtype=F32)))
    log_gamma = jnp.broadcast_to(log_gamma[:, None], (RET_HEADS, RET_DV))

    mod4 = _mod(c, mod_w, mod_b).reshape(b, N_MOD, 1, d)
    proj, gates = _inproj(x2, mod4, norm_mix_w, w_main, w_small, s)
    proj3 = proj.reshape(b, s, N_MAIN)
    ya = _retention(proj3, cos_t, sin_t, log_gamma)
    yb = _gdn(proj3, gates.reshape(b, s, LANES), alog_row, dtb_row, conv_w, gdn_norm_w)
    h2, xf2, comb = _outproj(ya.reshape(t, d), yb.reshape(t, d), x2, w_out.astype(BF16), mod4, norm_ffn_w,
                             w_route, b_route, s)
    return _moe(xf2, comb, wg, wu, wd, h2, mod4, norm_out_w, s).reshape(b, s, d)


def kernel(x, c, mod_w, mod_b, norm_mix_w, w_in, gdn_conv_w, gdn_a_log, gdn_dt_bias, gdn_norm_w, w_out, norm_ffn_w,
           w_group, b_group, w_router, b_router, w_gate, w_up, w_down, norm_out_w):
    assert mod_w.shape[0] == 1, "one residual layer"
    return _layer(x, c, mod_w[0], mod_b[0], norm_mix_w[0], w_in[0], gdn_conv_w[0], gdn_a_log[0], gdn_dt_bias[0],
                  gdn_norm_w[0], w_out[0], norm_ffn_w[0], w_group[0], b_group[0], w_router[0], b_router[0],
                  w_gate[0], w_up[0], w_down[0], norm_out_w)
```

```python
import functools

import jax
import jax.numpy as jnp
from jax import lax
from jax.experimental import pallas as pl
from jax.experimental.pallas import tpu as pltpu

F32 = jnp.float32
BF16 = jnp.bfloat16
HIGHEST = lax.Precision.HIGHEST

RET_HEADS = 4
RET_DK = 128
RET_DV = 256
GDN_HEADS = 4
GDN_DK = 128
GDN_DV = 256
GDN_CHUNK = 64
CONV_K = 4
N_GROUPS = 4
EXPERTS_PER_GROUP = 4
N_EXPERTS = N_GROUPS * EXPERTS_PER_GROUP
D_FF_EXPERT = 512
ROPE_BASE = 10000.0
NORM_EPS = 1e-6
L2_EPS = 1e-6
N_MOD = 6
LANES = 128
SEQ_TILE = 256
VMEM_LIMIT = 48 * 1024 * 1024

COL_RQ, COL_RK, COL_RV, COL_RG = 0, 512, 1024, 2048
COL_GQ, COL_GK, COL_GV, COL_GZ = 3072, 3584, 4096, 5120
COL_GATE_A, COL_GATE_B = 6144, 7168
N_MAIN = 8192


def _silu(x):
    return x * jax.nn.sigmoid(x)


def _dot(a, b, **kw):
    return jnp.dot(a, b, preferred_element_type=F32, **kw)


def _dot_nt(a, b, **kw):
    return lax.dot_general(a, b, (((1,), (1,)), ((), ())), preferred_element_type=F32, **kw)


def _dot_tn(a, b, **kw):
    return lax.dot_general(a, b, (((0,), (0,)), ((), ())), preferred_element_type=F32, **kw)


def _params(*sem):
    return pltpu.CompilerParams(dimension_semantics=sem, vmem_limit_bytes=VMEM_LIMIT)


def _mod_kernel(c_ref, w_ref, b_ref, o_ref):
    a = _silu(c_ref[...])
    o_ref[...] = _dot(a, w_ref[...], precision=HIGHEST) + b_ref[...]


def _mod(c, mod_w, mod_b):
    b, d = c.shape
    n = mod_w.shape[1]
    tn = d
    return pl.pallas_call(
        _mod_kernel,
        grid=(n // tn,),
        in_specs=[
            pl.BlockSpec((b, d), lambda j: (0, 0)),
            pl.BlockSpec((d, tn), lambda j: (0, j)),
            pl.BlockSpec((1, tn), lambda j: (0, j)),
        ],
        out_specs=pl.BlockSpec((b, tn), lambda j: (0, j)),
        out_shape=jax.ShapeDtypeStruct((b, n), F32),
        compiler_params=_params("arbitrary"),
        name="mod",
    )(c, mod_w, mod_b.reshape(1, n))


def _inproj_kernel(x_ref, nw_ref, shift_ref, scale_ref, w_ref, ws_ref, o_ref, og_ref, xb_ref):
    @pl.when(pl.program_id(1) == 0)
    def _():
        x = x_ref[...]
        y = x * lax.rsqrt(jnp.mean(x * x, axis=-1, keepdims=True) + NORM_EPS) * nw_ref[...]
        xm = y * (1.0 + scale_ref[...]) + shift_ref[...]
        xb_ref[...] = xm.astype(BF16)
        og_ref[...] = _dot(xm, ws_ref[...], precision=HIGHEST)

    o_ref[...] = _dot(xb_ref[...], w_ref[...]).astype(BF16)


def _inproj(x2, mod4, norm_w, w_main, w_small, seq, tm=1024, tn=1024):
    t, d = x2.shape
    n = w_main.shape[1]
    per_b = seq // tm
    return pl.pallas_call(
        _inproj_kernel,
        grid=(t // tm, n // tn),
        in_specs=[
            pl.BlockSpec((tm, d), lambda i, j: (i, 0)),
            pl.BlockSpec((1, d), lambda i, j: (0, 0)),
            pl.BlockSpec((None, None, 1, d), lambda i, j: (i // per_b, 0, 0, 0)),
            pl.BlockSpec((None, None, 1, d), lambda i, j: (i // per_b, 1, 0, 0)),
            pl.BlockSpec((d, tn), lambda i, j: (0, j)),
            pl.BlockSpec((d, LANES), lambda i, j: (0, 0)),
        ],
        out_specs=[
            pl.BlockSpec((tm, tn), lambda i, j: (i, j)),
            pl.BlockSpec((tm, LANES), lambda i, j: (i, 0)),
        ],
        out_shape=[
            jax.ShapeDtypeStruct((t, n), BF16),
            jax.ShapeDtypeStruct((t, LANES), F32),
        ],
        scratch_shapes=[pltpu.VMEM((tm, d), BF16)],
        compiler_params=_params("arbitrary", "arbitrary"),
        name="inproj",
    )(x2, norm_w.reshape(1, d), mod4, mod4, w_main, w_small)


def _ret_kernel(lg_ref, q_ref, k_ref, v_ref, rg_ref, ga_ref, cos_ref, sin_ref, o_ref,
                state_ref, intra_ref, qd_ref, kd_ref):
    c = SEQ_TILE
    first = jnp.logical_and(pl.program_id(0) == 0, pl.program_id(1) == 0)

    @pl.when(first)
    def _():
        row = lax.broadcasted_iota(jnp.int32, (c, c), 0)
        col = lax.broadcasted_iota(jnp.int32, (c, c), 1)
        rel = (row - col).astype(F32)
        causal = row >= col
        pos = lax.broadcasted_iota(jnp.int32, (c, RET_DK), 0).astype(F32)
        for h in range(RET_HEADS):
            lg = lg_ref[h:h + 1, :]
            intra_ref[h] = jnp.where(causal, jnp.exp(jnp.where(causal, rel, 0.0) * lg), 0.0)
            qd_ref[h] = jnp.exp((pos + 1.0) * lg[:, :RET_DK])
            kd_ref[h] = jnp.exp((c - 1.0 - pos) * lg[:, :RET_DK])

    @pl.when(pl.program_id(1) == 0)
    def _():
        state_ref[...] = jnp.zeros_like(state_ref)

    cos = cos_ref[...]
    sin = sin_ref[...]
    for h in range(RET_HEADS):
        qs = slice(h * RET_DK, (h + 1) * RET_DK)
        vs = slice(h * RET_DV, (h + 1) * RET_DV)
        qr = q_ref[:, qs].astype(F32)
        kr = k_ref[:, qs].astype(F32)
        q = qr * cos + pltpu.roll(qr, RET_DK // 2, 1) * sin
        k = (kr * cos + pltpu.roll(kr, RET_DK // 2, 1) * sin) * (RET_DK ** -0.5)
        v = v_ref[:, vs]
        state = state_ref[h]
        chunk_decay = jnp.exp(float(c) * lg_ref[h:h + 1, :])
        scores = _dot_nt(q.astype(BF16), k.astype(BF16)) * intra_ref[h]
        o = _dot(scores.astype(BF16), v) + _dot((q * qd_ref[h]).astype(BF16), state.astype(BF16))
        state_ref[h] = state * chunk_decay + _dot_tn((k * kd_ref[h]).astype(BF16), v)
        o = o * lax.rsqrt(jnp.mean(o * o, axis=-1, keepdims=True) + NORM_EPS)
        y = _silu(rg_ref[:, vs].astype(F32)) * o
        o_ref[:, vs] = (jax.nn.sigmoid(ga_ref[:, vs].astype(F32)) * y).astype(BF16)


def _retention(proj3, cos_t, sin_t, log_gamma):
    b, s, _ = proj3.shape
    ts = SEQ_TILE
    qk_w = RET_HEADS * RET_DK
    v_w = RET_HEADS * RET_DV
    return pl.pallas_call(
        _ret_kernel,
        grid=(b, s // ts),
        in_specs=[
            pl.BlockSpec((RET_HEADS, RET_DV), lambda i, j: (0, 0)),
            pl.BlockSpec((None, ts, qk_w), lambda i, j: (i, j, COL_RQ // qk_w)),
            pl.BlockSpec((None, ts, qk_w), lambda i, j: (i, j, COL_RK // qk_w)),
            pl.BlockSpec((None, ts, v_w), lambda i, j: (i, j, COL_RV // v_w)),
            pl.BlockSpec((None, ts, v_w), lambda i, j: (i, j, COL_RG // v_w)),
            pl.BlockSpec((None, ts, v_w), lambda i, j: (i, j, COL_GATE_A // v_w)),
            pl.BlockSpec((ts, RET_DK), lambda i, j: (j, 0)),
            pl.BlockSpec((ts, RET_DK), lambda i, j: (j, 0)),
        ],
        out_specs=pl.BlockSpec((None, ts, v_w), lambda i, j: (i, j, 0)),
        out_shape=jax.ShapeDtypeStruct((b, s, v_w), BF16),
        scratch_shapes=[
            pltpu.VMEM((RET_HEADS, RET_DK, RET_DV), F32),
            pltpu.VMEM((RET_HEADS, ts, ts), F32),
            pltpu.VMEM((RET_HEADS, ts, RET_DK), F32),
            pltpu.VMEM((RET_HEADS, ts, RET_DK), F32),
        ],
        compiler_params=_params("arbitrary", "arbitrary"),
        name="retention",
    )(log_gamma, proj3, proj3, proj3, proj3, proj3, cos_t, sin_t)


def _unit_lower_inverse(a, eye, nilpotency):
    p = eye - a
    m = a
    for _ in range((nilpotency - 1).bit_length() - 1):
        m16 = m.astype(BF16)
        m = _dot(m16, m16)
        p = p + _dot(p.astype(BF16), m.astype(BF16))
    return p


def _gdn_kernel(q_ref, k_ref, v_ref, z_ref, gb_ref, gates_ref, alog_ref, dtb_ref, cwq_ref, cwk_ref, cwv_ref,
                nw_ref, o_ref, state_ref, tail_ref):
    ts = SEQ_TILE
    cc = GDN_CHUNK
    qk_w = GDN_HEADS * GDN_DK

    @pl.when(pl.program_id(1) == 0)
    def _():
        state_ref[...] = jnp.zeros_like(state_ref)
        tail_ref[...] = jnp.zeros_like(tail_ref)

    def conv_silu(cur, tail, cw_ref):
        ext = jnp.concatenate([tail, cur], axis=0)
        out = ext[8 - (CONV_K - 1):8 - (CONV_K - 1) + ts] * cw_ref[0:1, :]
        for i in range(1, CONV_K):
            d = CONV_K - 1 - i
            out = out + ext[8 - d:8 - d + ts] * cw_ref[i:i + 1, :]
        return _silu(out)

    q_raw = q_ref[...].astype(F32)
    k_raw = k_ref[...].astype(F32)
    v_raw = v_ref[...].astype(F32)
    q_all = conv_silu(q_raw, tail_ref[:, 0:qk_w], cwq_ref)
    k_all = conv_silu(k_raw, tail_ref[:, qk_w:2 * qk_w], cwk_ref)
    v_all = conv_silu(v_raw, tail_ref[:, 2 * qk_w:], cwv_ref)
    tail_ref[:, 0:qk_w] = q_raw[ts - 8:]
    tail_ref[:, qk_w:2 * qk_w] = k_raw[ts - 8:]
    tail_ref[:, 2 * qk_w:] = v_raw[ts - 8:]

    gates = gates_ref[...]
    x = gates + dtb_ref[...]
    softplus = jnp.maximum(x, 0.0) + jnp.log1p(jnp.exp(-jnp.abs(x)))
    g_all = -jnp.exp(alog_ref[...]) * softplus
    beta_all = jax.nn.sigmoid(gates)
    row = lax.broadcasted_iota(jnp.int32, (ts, ts), 0)
    col = lax.broadcasted_iota(jnp.int32, (ts, ts), 1)
    same_chunk = (row // cc) == (col // cc)
    causal = jnp.logical_and(same_chunk, row >= col)
    strict = jnp.logical_and(same_chunk, row > col)
    eye = jnp.where(row == col, 1.0, 0.0)
    tri_lower = jnp.where(causal, 1.0, 0.0)
    tri_upper = jnp.where(jnp.logical_and(same_chunk, row <= col), 1.0, 0.0)
    gc_col_all = _dot(tri_lower, g_all, precision=HIGHEST)
    gc_row_all = _dot_tn(g_all, tri_upper, precision=HIGHEST)

    heads = range(GDN_HEADS)
    qn, kn, k16, k_beta, gc, decay, a_mat = [], [], [], [], [], [], []
    for h in heads:
        qs = slice(h * GDN_DK, (h + 1) * GDN_DK)
        qh = q_all[:, qs]
        kh = k_all[:, qs]
        qn.append(qh * lax.rsqrt(jnp.sum(qh * qh, axis=-1, keepdims=True) + L2_EPS) * (GDN_DK ** -0.5))
        kn.append(kh * lax.rsqrt(jnp.sum(kh * kh, axis=-1, keepdims=True) + L2_EPS))
        gc.append(gc_col_all[:, h:h + 1])
        gcr = gc_row_all[h:h + 1, :]
        decay.append(jnp.exp(jnp.where(causal, gc[h] - gcr, -jnp.inf)))
        k_beta.append(kn[h] * beta_all[:, GDN_HEADS + h:GDN_HEADS + h + 1])
        k16.append(kn[h].astype(BF16))
    for h in heads:
        a_mat.append(jnp.where(strict, _dot_nt(k_beta[h].astype(BF16), k16[h]) * decay[h], 0.0))
    p = [eye - a_mat[h] for h in heads]
    m = a_mat
    for _ in range((cc - 1).bit_length() - 1):
        m16 = [m[h].astype(BF16) for h in heads]
        m = [_dot(m16[h], m16[h]) for h in heads]
        p = [p[h] + _dot(p[h].astype(BF16), m[h].astype(BF16)) for h in heads]
    t16 = [p[h].astype(BF16) for h in heads]
    egc = [jnp.exp(gc[h]) for h in heads]
    u_all, w_all, qk_all, qg_all = [], [], [], []
    for h in heads:
        vs = slice(h * GDN_DV, (h + 1) * GDN_DV)
        beta = beta_all[:, GDN_HEADS + h:GDN_HEADS + h + 1]
        u_all.append(_dot(t16[h], (v_all[:, vs] * beta).astype(BF16)))
        w_all.append(_dot(t16[h], (k_beta[h] * egc[h]).astype(BF16)).astype(BF16))
        qk_all.append((_dot_nt(qn[h].astype(BF16), k16[h]) * decay[h]).astype(BF16))
        qg_all.append((qn[h] * egc[h]).astype(BF16))
    state = [state_ref[h] for h in heads]
    outs = [[] for _ in heads]
    for n in range(ts // cc):
        rs = slice(n * cc, (n + 1) * cc)
        for h in heads:
            state_b = state[h].astype(BF16)
            v_new = (u_all[h][rs] - _dot(w_all[h][rs], state_b)).astype(BF16)
            outs[h].append(_dot(qg_all[h][rs], state_b) + _dot(qk_all[h][rs, rs], v_new))
            g_last = gc[h][(n + 1) * cc - 1:(n + 1) * cc, :]
            k_g = kn[h][rs] * jnp.exp(g_last - gc[h][rs])
            state[h] = state[h] * jnp.exp(g_last) + _dot_tn(k_g.astype(BF16), v_new)
    for h in heads:
        vs = slice(h * GDN_DV, (h + 1) * GDN_DV)
        state_ref[h] = state[h]
        o = jnp.concatenate(outs[h], axis=0)
        o = o * lax.rsqrt(jnp.mean(o * o, axis=-1, keepdims=True) + NORM_EPS) * nw_ref[...]
        o = o * _silu(z_ref[:, vs].astype(F32))
        o_ref[:, vs] = (jax.nn.sigmoid(gb_ref[:, vs].astype(F32)) * o).astype(BF16)


def _gdn(proj3, gates3, alog_row, dtb_row, conv_w, norm_w):
    b, s, _ = proj3.shape
    ts = SEQ_TILE
    qk_w = GDN_HEADS * GDN_DK
    v_w = GDN_HEADS * GDN_DV
    return pl.pallas_call(
        _gdn_kernel,
        grid=(b, s // ts),
        in_specs=[
            pl.BlockSpec((None, ts, qk_w), lambda i, j: (i, j, COL_GQ // qk_w)),
            pl.BlockSpec((None, ts, qk_w), lambda i, j: (i, j, COL_GK // qk_w)),
            pl.BlockSpec((None, ts, v_w), lambda i, j: (i, j, COL_GV // v_w)),
            pl.BlockSpec((None, ts, v_w), lambda i, j: (i, j, COL_GZ // v_w)),
            pl.BlockSpec((None, ts, v_w), lambda i, j: (i, j, COL_GATE_B // v_w)),
            pl.BlockSpec((None, ts, LANES), lambda i, j: (i, j, 0)),
            pl.BlockSpec((1, LANES), lambda i, j: (0, 0)),
            pl.BlockSpec((1, LANES), lambda i, j: (0, 0)),
            pl.BlockSpec((CONV_K, qk_w), lambda i, j: (0, 0)),
            pl.BlockSpec((CONV_K, qk_w), lambda i, j: (0, 1)),
            pl.BlockSpec((CONV_K, v_w), lambda i, j: (0, 1)),
            pl.BlockSpec((1, GDN_DV), lambda i, j: (0, 0)),
        ],
        out_specs=pl.BlockSpec((None, ts, v_w), lambda i, j: (i, j, 0)),
        out_shape=jax.ShapeDtypeStruct((b, s, v_w), BF16),
        scratch_shapes=[
            pltpu.VMEM((GDN_HEADS, GDN_DK, GDN_DV), F32),
            pltpu.VMEM((8, 2 * qk_w + v_w), F32),
        ],
        compiler_params=_params("arbitrary", "arbitrary"),
        name="gdn",
    )(proj3, proj3, proj3, proj3, proj3, gates3, alog_row, dtb_row, conv_w, conv_w, conv_w,
      norm_w.reshape(1, GDN_DV))


def _outproj_kernel(ya_ref, yb_ref, x_ref, w_ref, gate_ref, nw_ref, shift_ref, scale_ref, wr_ref, br_ref,
                    h_ref, xf_ref, comb_ref):
    merged = (ya_ref[...].astype(F32) + yb_ref[...].astype(F32)).astype(BF16)
    h = x_ref[...] + gate_ref[...] * _dot(merged, w_ref[...])
    h_ref[...] = h
    y = h * lax.rsqrt(jnp.mean(h * h, axis=-1, keepdims=True) + NORM_EPS) * nw_ref[...]
    xf = y * (1.0 + scale_ref[...]) + shift_ref[...]
    xf_ref[...] = xf.astype(BF16)

    logits = _dot(xf, wr_ref[...], precision=HIGHEST) + br_ref[...]
    lane = lax.broadcasted_iota(jnp.int32, logits.shape, 1)
    neg = jnp.float32(-jnp.inf)

    def masked_softmax(mask):
        m = jnp.max(jnp.where(mask, logits, neg), axis=-1, keepdims=True)
        e = jnp.where(mask, jnp.exp(jnp.where(mask, logits, m) - m), 0.0)
        return e / jnp.sum(e, axis=-1, keepdims=True)

    def first_argmax(p, mask):
        top = jnp.max(jnp.where(mask, p, -1.0), axis=-1, keepdims=True)
        idx = jnp.min(jnp.where(jnp.logical_and(mask, p == top), lane, LANES), axis=-1, keepdims=True)
        return top, idx

    gmask = lane < N_GROUPS
    g_top, g_idx = first_argmax(masked_softmax(gmask), gmask)
    lo = N_GROUPS + EXPERTS_PER_GROUP * g_idx
    emask = jnp.logical_and(lane >= lo, lane < lo + EXPERTS_PER_GROUP)
    pe = masked_softmax(emask)
    top1, i1 = first_argmax(pe, emask)
    emask2 = jnp.logical_and(emask, lane != i1)
    top2, i2 = first_argmax(pe, emask2)
    denom = top1 + top2
    comb = jnp.where(lane == i1, g_top * (top1 / denom), jnp.where(lane == i2, g_top * (top2 / denom), 0.0))
    comb_ref[...] = comb


def _outproj(ya2, yb2, x2, w_out_b, mod4, norm_w, w_route, b_route, seq, tm=512):
    t, d = x2.shape
    per_b = seq // tm
    row_spec = pl.BlockSpec((tm, d), lambda i: (i, 0))

    def mod_spec(k):
        return pl.BlockSpec((None, None, 1, d), lambda i: (i // per_b, k, 0, 0))

    return pl.pallas_call(
        _outproj_kernel,
        grid=(t // tm,),
        in_specs=[
            row_spec, row_spec, row_spec,
            pl.BlockSpec((d, d), lambda i: (0, 0)),
            mod_spec(2),
            pl.BlockSpec((1, d), lambda i: (0, 0)),
            mod_spec(3),
            mod_spec(4),
            pl.BlockSpec((d, LANES), lambda i: (0, 0)),
            pl.BlockSpec((1, LANES), lambda i: (0, 0)),
        ],
        out_specs=[row_spec, row_spec, pl.BlockSpec((tm, LANES), lambda i: (i, 0))],
        out_shape=[
            jax.ShapeDtypeStruct((t, d), F32),
            jax.ShapeDtypeStruct((t, d), BF16),
            jax.ShapeDtypeStruct((t, LANES), F32),
        ],
        compiler_params=_params("arbitrary"),
        name="outproj",
    )(ya2, yb2, x2, w_out_b, mod4, norm_w.reshape(1, d), mod4, mod4, w_route, b_route)


def _moe_kernel(xf_ref, comb_ref, wg_ref, wu_ref, wd_ref, h_ref, gate_ref, nw_ref, o_ref, acc_ref):
    e = pl.program_id(1)

    @pl.when(e == 0)
    def _():
        acc_ref[...] = jnp.zeros_like(acc_ref)

    xf = xf_ref[...]
    comb = comb_ref[...]
    lane = lax.broadcasted_iota(jnp.int32, comb.shape, 1)
    wcol = jnp.sum(jnp.where(lane == N_GROUPS + e, comb, 0.0), axis=-1, keepdims=True)
    hg = _dot(xf, wg_ref[...])
    hu = _dot(xf, wu_ref[...])
    act = _silu(hg) * hu * wcol
    acc_ref[...] += _dot(act.astype(BF16), wd_ref[...])

    @pl.when(e == pl.num_programs(1) - 1)
    def _():
        h = h_ref[...] + gate_ref[...] * acc_ref[...]
        o_ref[...] = h * lax.rsqrt(jnp.mean(h * h, axis=-1, keepdims=True) + NORM_EPS) * nw_ref[...]


def _moe(xf2, comb, wg, wu, wd, h2, mod4, norm_out_w, seq, tm=1024):
    t, d = xf2.shape
    f = wg.shape[-1]
    per_b = seq // tm
    row_spec = pl.BlockSpec((tm, d), lambda i, e: (i, 0))
    return pl.pallas_call(
        _moe_kernel,
        grid=(t // tm, N_EXPERTS),
        in_specs=[
            row_spec,
            pl.BlockSpec((tm, LANES), lambda i, e: (i, 0)),
            pl.BlockSpec((None, d, f), lambda i, e: (e, 0, 0)),
            pl.BlockSpec((None, d, f), lambda i, e: (e, 0, 0)),
            pl.BlockSpec((None, f, d), lambda i, e: (e, 0, 0)),
            row_spec,
            pl.BlockSpec((None, None, 1, d), lambda i, e: (i // per_b, 5, 0, 0)),
            pl.BlockSpec((1, d), lambda i, e: (0, 0)),
        ],
        out_specs=row_spec,
        out_shape=jax.ShapeDtypeStruct((t, d), F32),
        scratch_shapes=[pltpu.VMEM((tm, d), F32)],
        compiler_params=_params("arbitrary", "arbitrary"),
        name="moe",
    )(xf2, comb, wg, wu, wd, h2, mod4, norm_out_w.reshape(1, d))


def _pad_lanes(a):
    return jnp.pad(a, ((0, 0), (0, LANES - a.shape[1])))


def _layer(h3, c, mod_w, mod_b, norm_mix_w, w_in, conv_w, a_log, dt_bias, gdn_norm_w, w_out, norm_ffn_w,
           w_group, b_group, w_router, b_router, w_gate, w_up, w_down, norm_out_w):
    b, s, d = h3.shape
    t = b * s
    x2 = h3.reshape(t, d)

    n_gate_cols = 2 * GDN_HEADS
    small_lo = COL_GATE_A
    w_main = jnp.concatenate([w_in[:, :small_lo], w_in[:, small_lo + n_gate_cols:]], axis=1).astype(BF16)
    w_small = _pad_lanes(w_in[:, small_lo:small_lo + n_gate_cols])
    w_route = _pad_lanes(jnp.concatenate([w_group, w_router], axis=1))
    b_route = _pad_lanes(jnp.concatenate([b_group, b_router.reshape(-1)])[None, :])
    alog_row = _pad_lanes(a_log[None, :])
    dtb_row = _pad_lanes(dt_bias[None, :])
    f = w_gate.shape[-1]
    wg = w_gate.reshape(N_EXPERTS, d, f).astype(BF16)
    wu = w_up.reshape(N_EXPERTS, d, f).astype(BF16)
    wd = w_down.reshape(N_EXPERTS, f, d).astype(BF16)

    half = RET_DK // 2
    inv_freq = 1.0 / (ROPE_BASE ** (jnp.arange(half, dtype=F32) / half))
    ang = jnp.arange(s, dtype=F32)[:, None] * inv_freq[None, :]
    cos_t = jnp.concatenate([jnp.cos(ang), jnp.cos(ang)], axis=1)
    sin_t = jnp.concatenate([-jnp.sin(ang), jnp.sin(ang)], axis=1)
    log_gamma = jnp.log(1.0 - 2.0 ** (-5.0 - jnp.arange(RET_HEADS, dtype=F32)))
    log_gamma = jnp.broadcast_to(log_gamma[:, None], (RET_HEADS, RET_DV))

    mod4 = _mod(c, mod_w, mod_b).reshape(b, N_MOD, 1, d)
    proj, gates = _inproj(x2, mod4, norm_mix_w, w_main, w_small, s)
    proj3 = proj.reshape(b, s, N_MAIN)
    ya = _retention(proj3, cos_t, sin_t, log_gamma)
    yb = _gdn(proj3, gates.reshape(b, s, LANES), alog_row, dtb_row, conv_w, gdn_norm_w)
    h2, xf2, comb = _outproj(ya.reshape(t, d), yb.reshape(t, d), x2, w_out.astype(BF16), mod4, norm_ffn_w,
                             w_route, b_route, s)
    return _moe(xf2, comb, wg, wu, wd, h2, mod4, norm_out_w, s).reshape(b, s, d)


def kernel(x, c, mod_w, mod_b, norm_mix_w, w_in, gdn_conv_w, gdn_a_log, gdn_dt_bias, gdn_norm_w, w_out, norm_ffn_w,
           w_group, b_group, w_router, b_router, w_gate, w_up, w_down, norm_out_w):
    assert mod_w.shape[0] == 1, "one residual layer"
    return _layer(x, c, mod_w[0], mod_b[0], norm_mix_w[0], w_in[0], gdn_conv_w[0], gdn_a_log[0], gdn_dt_bias[0],
                  gdn_norm_w[0], w_out[0], norm_ffn_w[0], w_group[0], b_group[0], w_router[0], b_router[0],
                  w_gate[0], w_up[0], w_down[0], norm_out_w)
```

```python
import functools

import jax
import jax.numpy as jnp
from jax import lax
from jax.experimental import pallas as pl
from jax.experimental.pallas import tpu as pltpu

F32 = jnp.float32
BF16 = jnp.bfloat16
HIGHEST = lax.Precision.HIGHEST

RET_HEADS = 4
RET_DK = 128
RET_DV = 256
GDN_HEADS = 4
GDN_DK = 128
GDN_DV = 256
GDN_CHUNK = 64
CONV_K = 4
N_GROUPS = 4
EXPERTS_PER_GROUP = 4
N_EXPERTS = N_GROUPS * EXPERTS_PER_GROUP
D_FF_EXPERT = 512
ROPE_BASE = 10000.0
NORM_EPS = 1e-6
L2_EPS = 1e-6
N_MOD = 6
LANES = 128
SEQ_TILE = 256
VMEM_LIMIT = 48 * 1024 * 1024
MOE_TILE = 1024
MOE_CHUNK = 320
MOE_SORTED = (MOE_TILE + N_GROUPS * (MOE_CHUNK - 1)) // MOE_CHUNK * MOE_CHUNK
MOE_VMEM_LIMIT = 54 * 1024 * 1024
MOE_ROW_ALIGN = 64
MOE_FF_SLAB = 256

COL_RQ, COL_RK, COL_RV, COL_RG = 0, 512, 1024, 2048
COL_GQ, COL_GK, COL_GV, COL_GZ = 3072, 3584, 4096, 5120
COL_GATE_A, COL_GATE_B = 6144, 7168
N_MAIN = 8192


def _silu(x):
    return x * jax.nn.sigmoid(x)


def _dot(a, b, **kw):
    return jnp.dot(a, b, preferred_element_type=F32, **kw)


def _dot_nt(a, b, **kw):
    return lax.dot_general(a, b, (((1,), (1,)), ((), ())), preferred_element_type=F32, **kw)


def _dot_tn(a, b, **kw):
    return lax.dot_general(a, b, (((0,), (0,)), ((), ())), preferred_element_type=F32, **kw)


def _params(*sem):
    return pltpu.CompilerParams(dimension_semantics=sem, vmem_limit_bytes=VMEM_LIMIT)


def _mod_kernel(c_ref, w_ref, b_ref, o_ref):
    a = _silu(c_ref[...])
    o_ref[...] = _dot(a, w_ref[...], precision=HIGHEST) + b_ref[...]


def _mod(c, mod_w, mod_b):
    b, d = c.shape
    n = mod_w.shape[1]
    tn = d
    return pl.pallas_call(
        _mod_kernel,
        grid=(n // tn,),
        in_specs=[
            pl.BlockSpec((b, d), lambda j: (0, 0)),
            pl.BlockSpec((d, tn), lambda j: (0, j)),
            pl.BlockSpec((1, tn), lambda j: (0, j)),
        ],
        out_specs=pl.BlockSpec((b, tn), lambda j: (0, j)),
        out_shape=jax.ShapeDtypeStruct((b, n), F32),
        compiler_params=_params("arbitrary"),
        name="mod",
    )(c, mod_w, mod_b.reshape(1, n))


def _inproj_kernel(x_ref, nw_ref, shift_ref, scale_ref, w_ref, ws_ref, o_ref, og_ref, xb_ref):
    @pl.when(pl.program_id(1) == 0)
    def _():
        x = x_ref[...]
        y = x * lax.rsqrt(jnp.mean(x * x, axis=-1, keepdims=True) + NORM_EPS) * nw_ref[...]
        xm = y * (1.0 + scale_ref[...]) + shift_ref[...]
        xb_ref[...] = xm.astype(BF16)
        og_ref[...] = _dot(xm, ws_ref[...], precision=HIGHEST)

    o_ref[...] = _dot(xb_ref[...], w_ref[...]).astype(BF16)


def _inproj(x2, mod4, norm_w, w_main, w_small, seq, tm=1024, tn=1024):
    t, d = x2.shape
    n = w_main.shape[1]
    per_b = seq // tm
    return pl.pallas_call(
        _inproj_kernel,
        grid=(t // tm, n // tn),
        in_specs=[
            pl.BlockSpec((tm, d), lambda i, j: (i, 0)),
            pl.BlockSpec((1, d), lambda i, j: (0, 0)),
            pl.BlockSpec((None, None, 1, d), lambda i, j: (i // per_b, 0, 0, 0)),
            pl.BlockSpec((None, None, 1, d), lambda i, j: (i // per_b, 1, 0, 0)),
            pl.BlockSpec((d, tn), lambda i, j: (0, j)),
            pl.BlockSpec((d, LANES), lambda i, j: (0, 0)),
        ],
        out_specs=[
            pl.BlockSpec((tm, tn), lambda i, j: (i, j)),
            pl.BlockSpec((tm, LANES), lambda i, j: (i, 0)),
        ],
        out_shape=[
            jax.ShapeDtypeStruct((t, n), BF16),
            jax.ShapeDtypeStruct((t, LANES), F32),
        ],
        scratch_shapes=[pltpu.VMEM((tm, d), BF16)],
        compiler_params=_params("arbitrary", "arbitrary"),
        name="inproj",
    )(x2, norm_w.reshape(1, d), mod4, mod4, w_main, w_small)


def _ret_kernel(lg_ref, q_ref, k_ref, v_ref, rg_ref, ga_ref, cos_ref, sin_ref, o_ref,
                state_ref, intra_ref, qd_ref, kd_ref):
    c = SEQ_TILE
    first = jnp.logical_and(pl.program_id(0) == 0, pl.program_id(1) == 0)

    @pl.when(first)
    def _():
        row = lax.broadcasted_iota(jnp.int32, (c, c), 0)
        col = lax.broadcasted_iota(jnp.int32, (c, c), 1)
        rel = (row - col).astype(F32)
        causal = row >= col
        pos = lax.broadcasted_iota(jnp.int32, (c, RET_DK), 0).astype(F32)
        for h in range(RET_HEADS):
            lg = lg_ref[h:h + 1, :]
            intra_ref[h] = jnp.where(causal, jnp.exp(jnp.where(causal, rel, 0.0) * lg), 0.0)
            qd_ref[h] = jnp.exp((pos + 1.0) * lg[:, :RET_DK])
            kd_ref[h] = jnp.exp((c - 1.0 - pos) * lg[:, :RET_DK])

    @pl.when(pl.program_id(1) == 0)
    def _():
        state_ref[...] = jnp.zeros_like(state_ref)

    cos = cos_ref[...]
    sin = sin_ref[...]
    for h in range(RET_HEADS):
        qs = slice(h * RET_DK, (h + 1) * RET_DK)
        vs = slice(h * RET_DV, (h + 1) * RET_DV)
        qr = q_ref[:, qs].astype(F32)
        kr = k_ref[:, qs].astype(F32)
        q = qr * cos + pltpu.roll(qr, RET_DK // 2, 1) * sin
        k = (kr * cos + pltpu.roll(kr, RET_DK // 2, 1) * sin) * (RET_DK ** -0.5)
        v = v_ref[:, vs]
        state = state_ref[h]
        chunk_decay = jnp.exp(float(c) * lg_ref[h:h + 1, :])
        scores = _dot_nt(q.astype(BF16), k.astype(BF16)) * intra_ref[h]
        o = _dot(scores.astype(BF16), v) + _dot((q * qd_ref[h]).astype(BF16), state.astype(BF16))
        state_ref[h] = state * chunk_decay + _dot_tn((k * kd_ref[h]).astype(BF16), v)
        o = o * lax.rsqrt(jnp.mean(o * o, axis=-1, keepdims=True) + NORM_EPS)
        y = _silu(rg_ref[:, vs].astype(F32)) * o
        o_ref[:, vs] = (jax.nn.sigmoid(ga_ref[:, vs].astype(F32)) * y).astype(BF16)


def _retention(proj3, cos_t, sin_t, log_gamma):
    b, s, _ = proj3.shape
    ts = SEQ_TILE
    qk_w = RET_HEADS * RET_DK
    v_w = RET_HEADS * RET_DV
    return pl.pallas_call(
        _ret_kernel,
        grid=(b, s // ts),
        in_specs=[
            pl.BlockSpec((RET_HEADS, RET_DV), lambda i, j: (0, 0)),
            pl.BlockSpec((None, ts, qk_w), lambda i, j: (i, j, COL_RQ // qk_w)),
            pl.BlockSpec((None, ts, qk_w), lambda i, j: (i, j, COL_RK // qk_w)),
            pl.BlockSpec((None, ts, v_w), lambda i, j: (i, j, COL_RV // v_w)),
            pl.BlockSpec((None, ts, v_w), lambda i, j: (i, j, COL_RG // v_w)),
            pl.BlockSpec((None, ts, v_w), lambda i, j: (i, j, COL_GATE_A // v_w)),
            pl.BlockSpec((ts, RET_DK), lambda i, j: (j, 0)),
            pl.BlockSpec((ts, RET_DK), lambda i, j: (j, 0)),
        ],
        out_specs=pl.BlockSpec((None, ts, v_w), lambda i, j: (i, j, 0)),
        out_shape=jax.ShapeDtypeStruct((b, s, v_w), BF16),
        scratch_shapes=[
            pltpu.VMEM((RET_HEADS, RET_DK, RET_DV), F32),
            pltpu.VMEM((RET_HEADS, ts, ts), F32),
            pltpu.VMEM((RET_HEADS, ts, RET_DK), F32),
            pltpu.VMEM((RET_HEADS, ts, RET_DK), F32),
        ],
        compiler_params=_params("arbitrary", "arbitrary"),
        name="retention",
    )(log_gamma, proj3, proj3, proj3, proj3, proj3, cos_t, sin_t)


def _unit_lower_inverse(a, eye, nilpotency):
    p = eye - a
    m = a
    for _ in range((nilpotency - 1).bit_length() - 1):
        m16 = m.astype(BF16)
        m = _dot(m16, m16)
        p = p + _dot(p.astype(BF16), m.astype(BF16))
    return p


def _gdn_kernel(q_ref, k_ref, v_ref, z_ref, gb_ref, gates_ref, alog_ref, dtb_ref, cwq_ref, cwk_ref, cwv_ref,
                nw_ref, o_ref, state_ref, tail_ref):
    ts = SEQ_TILE
    cc = GDN_CHUNK
    qk_w = GDN_HEADS * GDN_DK

    @pl.when(pl.program_id(1) == 0)
    def _():
        state_ref[...] = jnp.zeros_like(state_ref)
        tail_ref[...] = jnp.zeros_like(tail_ref)

    def conv_silu(cur, tail, cw_ref):
        ext = jnp.concatenate([tail, cur], axis=0)
        out = ext[8 - (CONV_K - 1):8 - (CONV_K - 1) + ts] * cw_ref[0:1, :]
        for i in range(1, CONV_K):
            d = CONV_K - 1 - i
            out = out + ext[8 - d:8 - d + ts] * cw_ref[i:i + 1, :]
        return _silu(out)

    q_raw = q_ref[...].astype(F32)
    k_raw = k_ref[...].astype(F32)
    v_raw = v_ref[...].astype(F32)
    q_all = conv_silu(q_raw, tail_ref[:, 0:qk_w], cwq_ref)
    k_all = conv_silu(k_raw, tail_ref[:, qk_w:2 * qk_w], cwk_ref)
    v_all = conv_silu(v_raw, tail_ref[:, 2 * qk_w:], cwv_ref)
    tail_ref[:, 0:qk_w] = q_raw[ts - 8:]
    tail_ref[:, qk_w:2 * qk_w] = k_raw[ts - 8:]
    tail_ref[:, 2 * qk_w:] = v_raw[ts - 8:]

    gates = gates_ref[...]
    x = gates + dtb_ref[...]
    softplus = jnp.maximum(x, 0.0) + jnp.log1p(jnp.exp(-jnp.abs(x)))
    g_all = -jnp.exp(alog_ref[...]) * softplus
    beta_all = jax.nn.sigmoid(gates)
    row = lax.broadcasted_iota(jnp.int32, (ts, ts), 0)
    col = lax.broadcasted_iota(jnp.int32, (ts, ts), 1)
    same_chunk = (row // cc) == (col // cc)
    causal = jnp.logical_and(same_chunk, row >= col)
    strict = jnp.logical_and(same_chunk, row > col)
    eye = jnp.where(row == col, 1.0, 0.0)
    tri_lower = jnp.where(causal, 1.0, 0.0)
    tri_upper = jnp.where(jnp.logical_and(same_chunk, row <= col), 1.0, 0.0)
    gc_col_all = _dot(tri_lower, g_all, precision=HIGHEST)
    gc_row_all = _dot_tn(g_all, tri_upper, precision=HIGHEST)

    heads = range(GDN_HEADS)
    qn, kn, k16, k_beta, gc, decay, a_mat = [], [], [], [], [], [], []
    for h in heads:
        qs = slice(h * GDN_DK, (h + 1) * GDN_DK)
        qh = q_all[:, qs]
        kh = k_all[:, qs]
        qn.append(qh * lax.rsqrt(jnp.sum(qh * qh, axis=-1, keepdims=True) + L2_EPS) * (GDN_DK ** -0.5))
        kn.append(kh * lax.rsqrt(jnp.sum(kh * kh, axis=-1, keepdims=True) + L2_EPS))
        gc.append(gc_col_all[:, h:h + 1])
        gcr = gc_row_all[h:h + 1, :]
        decay.append(jnp.exp(jnp.where(causal, gc[h] - gcr, -jnp.inf)))
        k_beta.append(kn[h] * beta_all[:, GDN_HEADS + h:GDN_HEADS + h + 1])
        k16.append(kn[h].astype(BF16))
    for h in heads:
        a_mat.append(jnp.where(strict, _dot_nt(k_beta[h].astype(BF16), k16[h]) * decay[h], 0.0))
    p = [eye - a_mat[h] for h in heads]
    m = a_mat
    for _ in range((cc - 1).bit_length() - 1):
        m16 = [m[h].astype(BF16) for h in heads]
        m = [_dot(m16[h], m16[h]) for h in heads]
        p = [p[h] + _dot(p[h].astype(BF16), m[h].astype(BF16)) for h in heads]
    t16 = [p[h].astype(BF16) for h in heads]
    egc = [jnp.exp(gc[h]) for h in heads]
    u_all, w_all, qk_all, qg_all = [], [], [], []
    for h in heads:
        vs = slice(h * GDN_DV, (h + 1) * GDN_DV)
        beta = beta_all[:, GDN_HEADS + h:GDN_HEADS + h + 1]
        u_all.append(_dot(t16[h], (v_all[:, vs] * beta).astype(BF16)))
        w_all.append(_dot(t16[h], (k_beta[h] * egc[h]).astype(BF16)).astype(BF16))
        qk_all.append((_dot_nt(qn[h].astype(BF16), k16[h]) * decay[h]).astype(BF16))
        qg_all.append((qn[h] * egc[h]).astype(BF16))
    state = [state_ref[h] for h in heads]
    outs = [[] for _ in heads]
    for n in range(ts // cc):
        rs = slice(n * cc, (n + 1) * cc)
        for h in heads:
            state_b = state[h].astype(BF16)
            v_new = (u_all[h][rs] - _dot(w_all[h][rs], state_b)).astype(BF16)
            outs[h].append(_dot(qg_all[h][rs], state_b) + _dot(qk_all[h][rs, rs], v_new))
            g_last = gc[h][(n + 1) * cc - 1:(n + 1) * cc, :]
            k_g = kn[h][rs] * jnp.exp(g_last - gc[h][rs])
            state[h] = state[h] * jnp.exp(g_last) + _dot_tn(k_g.astype(BF16), v_new)
    for h in heads:
        vs = slice(h * GDN_DV, (h + 1) * GDN_DV)
        state_ref[h] = state[h]
        o = jnp.concatenate(outs[h], axis=0)
        o = o * lax.rsqrt(jnp.mean(o * o, axis=-1, keepdims=True) + NORM_EPS) * nw_ref[...]
        o = o * _silu(z_ref[:, vs].astype(F32))
        o_ref[:, vs] = (jax.nn.sigmoid(gb_ref[:, vs].astype(F32)) * o).astype(BF16)


def _gdn(proj3, gates3, alog_row, dtb_row, conv_w, norm_w):
    b, s, _ = proj3.shape
    ts = SEQ_TILE
    qk_w = GDN_HEADS * GDN_DK
    v_w = GDN_HEADS * GDN_DV
    return pl.pallas_call(
        _gdn_kernel,
        grid=(b, s // ts),
        in_specs=[
            pl.BlockSpec((None, ts, qk_w), lambda i, j: (i, j, COL_GQ // qk_w)),
            pl.BlockSpec((None, ts, qk_w), lambda i, j: (i, j, COL_GK // qk_w)),
            pl.BlockSpec((None, ts, v_w), lambda i, j: (i, j, COL_GV // v_w)),
            pl.BlockSpec((None, ts, v_w), lambda i, j: (i, j, COL_GZ // v_w)),
            pl.BlockSpec((None, ts, v_w), lambda i, j: (i, j, COL_GATE_B // v_w)),
            pl.BlockSpec((None, ts, LANES), lambda i, j: (i, j, 0)),
            pl.BlockSpec((1, LANES), lambda i, j: (0, 0)),
            pl.BlockSpec((1, LANES), lambda i, j: (0, 0)),
            pl.BlockSpec((CONV_K, qk_w), lambda i, j: (0, 0)),
            pl.BlockSpec((CONV_K, qk_w), lambda i, j: (0, 1)),
            pl.BlockSpec((CONV_K, v_w), lambda i, j: (0, 1)),
            pl.BlockSpec((1, GDN_DV), lambda i, j: (0, 0)),
        ],
        out_specs=pl.BlockSpec((None, ts, v_w), lambda i, j: (i, j, 0)),
        out_shape=jax.ShapeDtypeStruct((b, s, v_w), BF16),
        scratch_shapes=[
            pltpu.VMEM((GDN_HEADS, GDN_DK, GDN_DV), F32),
            pltpu.VMEM((8, 2 * qk_w + v_w), F32),
        ],
        compiler_params=_params("arbitrary", "arbitrary"),
        name="gdn",
    )(proj3, proj3, proj3, proj3, proj3, gates3, alog_row, dtb_row, conv_w, conv_w, conv_w,
      norm_w.reshape(1, GDN_DV))


def _outproj_kernel(ya_ref, yb_ref, x_ref, w_ref, gate_ref, nw_ref, shift_ref, scale_ref, wr_ref, br_ref,
                    h_ref, xf_ref, comb_ref):
    merged = (ya_ref[...].astype(F32) + yb_ref[...].astype(F32)).astype(BF16)
    h = x_ref[...] + gate_ref[...] * _dot(merged, w_ref[...])
    h_ref[...] = h
    y = h * lax.rsqrt(jnp.mean(h * h, axis=-1, keepdims=True) + NORM_EPS) * nw_ref[...]
    xf = y * (1.0 + scale_ref[...]) + shift_ref[...]
    xf_ref[...] = xf.astype(BF16)

    logits = _dot(xf, wr_ref[...], precision=HIGHEST) + br_ref[...]
    lane = lax.broadcasted_iota(jnp.int32, logits.shape, 1)
    neg = jnp.float32(-jnp.inf)

    def masked_softmax(mask):
        m = jnp.max(jnp.where(mask, logits, neg), axis=-1, keepdims=True)
        e = jnp.where(mask, jnp.exp(jnp.where(mask, logits, m) - m), 0.0)
        return e / jnp.sum(e, axis=-1, keepdims=True)

    def first_argmax(p, mask):
        top = jnp.max(jnp.where(mask, p, -1.0), axis=-1, keepdims=True)
        idx = jnp.min(jnp.where(jnp.logical_and(mask, p == top), lane, LANES), axis=-1, keepdims=True)
        return top, idx

    gmask = lane < N_GROUPS
    g_top, g_idx = first_argmax(masked_softmax(gmask), gmask)
    lo = N_GROUPS + EXPERTS_PER_GROUP * g_idx
    emask = jnp.logical_and(lane >= lo, lane < lo + EXPERTS_PER_GROUP)
    pe = masked_softmax(emask)
    top1, i1 = first_argmax(pe, emask)
    emask2 = jnp.logical_and(emask, lane != i1)
    top2, i2 = first_argmax(pe, emask2)
    denom = top1 + top2
    comb = jnp.where(lane == i1, g_top * (top1 / denom), jnp.where(lane == i2, g_top * (top2 / denom), 0.0))
    comb_ref[...] = jnp.where(lane == 0, g_idx.astype(F32), comb)


def _outproj(ya2, yb2, x2, w_out_b, mod4, norm_w, w_route, b_route, seq, tm=512):
    t, d = x2.shape
    per_b = seq // tm
    row_spec = pl.BlockSpec((tm, d), lambda i: (i, 0))

    def mod_spec(k):
        return pl.BlockSpec((None, None, 1, d), lambda i: (i // per_b, k, 0, 0))

    return pl.pallas_call(
        _outproj_kernel,
        grid=(t // tm,),
        in_specs=[
            row_spec, row_spec, row_spec,
            pl.BlockSpec((d, d), lambda i: (0, 0)),
            mod_spec(2),
            pl.BlockSpec((1, d), lambda i: (0, 0)),
            mod_spec(3),
            mod_spec(4),
            pl.BlockSpec((d, LANES), lambda i: (0, 0)),
            pl.BlockSpec((1, LANES), lambda i: (0, 0)),
        ],
        out_specs=[row_spec, row_spec, pl.BlockSpec((tm, LANES), lambda i: (i, 0))],
        out_shape=[
            jax.ShapeDtypeStruct((t, d), F32),
            jax.ShapeDtypeStruct((t, d), BF16),
            jax.ShapeDtypeStruct((t, LANES), F32),
        ],
        compiler_params=_params("arbitrary"),
        name="outproj",
    )(ya2, yb2, x2, w_out_b, mod4, norm_w.reshape(1, d), mod4, mod4, w_route, b_route)


def _moe_kernel(xf_ref, comb_ref, wg_ref, wu_ref, wd_ref, h_ref, gate_ref, nw_ref, o_ref,
                xs_ref, cs_ref, acc_ref, posc_ref, posr_ref, ltri_ref, meta_ref):
    i = pl.program_id(0)
    e = pl.program_id(1)
    tm = MOE_TILE
    r = MOE_CHUNK
    g = e // EXPERTS_PER_GROUP

    @pl.when(jnp.logical_and(i == 0, e == 0))
    def _():
        row = lax.broadcasted_iota(jnp.int32, (tm, tm), 0)
        col = lax.broadcasted_iota(jnp.int32, (tm, tm), 1)
        ltri_ref[...] = jnp.where(row >= col, 1.0, 0.0).astype(BF16)

    def chunk_rows(c):
        return pl.ds(pl.multiple_of(c * r, MOE_ROW_ALIGN), r)

    @pl.when(e == 0)
    def _():
        comb = comb_ref[...]
        lane = lax.broadcasted_iota(jnp.int32, (tm, LANES), 1)
        lane_f = lane.astype(F32)
        gidx = comb[:, 0:1]
        mine = jnp.logical_and(lane_f == gidx, lane < N_GROUPS)
        onehot = jnp.where(mine, 1.0, 0.0)
        csum = _dot(ltri_ref[...], onehot.astype(BF16))
        counts = jnp.broadcast_to(csum[tm - 1:tm, :], (8, LANES))
        nch = jnp.floor((counts + (r - 0.5)) * (1.0 / r))
        lr = lax.broadcasted_iota(jnp.int32, (LANES, LANES), 0)
        lc = lax.broadcasted_iota(jnp.int32, (LANES, LANES), 1)
        cstart = _dot(nch, jnp.where(lr < lc, 1.0, 0.0), precision=HIGHEST)
        pos = jnp.sum(jnp.where(mine, cstart[0:1, :] * r + csum - 1.0, 0.0), axis=-1, keepdims=True)
        pos_b = jnp.broadcast_to(pos, (tm, LANES))
        posc_ref[...] = pos_b.astype(jnp.int32)
        posr_ref[...] = pos_b.T[0:8, :].astype(jnp.int32)
        lane1 = lax.broadcasted_iota(jnp.int32, (8, LANES), 1)
        for k in range(N_GROUPS):
            meta_ref[k] = jnp.sum(jnp.where(lane1 == k, nch, 0.0)[0:1, :]).astype(jnp.int32)
            meta_ref[N_GROUPS + k] = jnp.sum(jnp.where(lane1 == k, cstart, 0.0)[0:1, :]).astype(jnp.int32)
        meta_ref[2 * N_GROUPS] = jnp.sum(jnp.where(lane1 < N_GROUPS, nch, 0.0)[0:1, :]).astype(jnp.int32)

        c_hi = comb.astype(BF16)
        rem = comb - c_hi.astype(F32)
        c_mid = rem.astype(BF16)
        c_lo = (rem - c_mid.astype(F32)).astype(BF16)
        xf = xf_ref[...]
        pos_row = posr_ref[0:1, :]
        sub = lax.broadcasted_iota(jnp.int32, (r, tm), 0)

        def gather(c, carry):
            p_c = jnp.where(pos_row == sub + c * r, 1.0, 0.0).astype(BF16)
            rows = chunk_rows(c)
            xs_ref[rows, :] = _dot(p_c, xf).astype(BF16)
            cs_ref[rows, :] = _dot(p_c, c_hi) + _dot(p_c, c_mid) + _dot(p_c, c_lo)
            return carry

        lax.fori_loop(0, meta_ref[2 * N_GROUPS], gather, 0)

    lane_r = lax.broadcasted_iota(jnp.int32, (r, LANES), 1)
    first_of_group = e % EXPERTS_PER_GROUP == 0

    def expert_chunk(c, carry):
        rows = chunk_rows(c)
        xs = xs_ref[rows, :]
        wcol = jnp.sum(jnp.where(lane_r == N_GROUPS + e, cs_ref[rows, :], 0.0), axis=-1, keepdims=True)
        contrib = None
        for f0 in range(0, D_FF_EXPERT, MOE_FF_SLAB):
            fs = slice(f0, f0 + MOE_FF_SLAB)
            act = _silu(_dot(xs, wg_ref[:, fs])) * _dot(xs, wu_ref[:, fs]) * wcol
            part = _dot(act.astype(BF16), wd_ref[fs, :])
            contrib = part if contrib is None else contrib + part

        @pl.when(first_of_group)
        def _():
            acc_ref[rows, :] = contrib

        @pl.when(jnp.logical_not(first_of_group))
        def _():
            acc_ref[rows, :] += contrib

        return carry

    c0 = meta_ref[N_GROUPS + g]
    lax.fori_loop(c0, c0 + meta_ref[g], expert_chunk, 0)

    @pl.when(e == pl.num_programs(1) - 1)
    def _():
        pos_col = posc_ref[:, 0:1]
        lane_c = lax.broadcasted_iota(jnp.int32, (tm, r), 1)
        o_ref[...] = jnp.zeros_like(o_ref)

        def scatter(c, carry):
            pt_c = jnp.where(pos_col == lane_c + c * r, 1.0, 0.0).astype(BF16)
            o_ref[...] += _dot(pt_c, acc_ref[chunk_rows(c), :].astype(BF16))
            return carry

        lax.fori_loop(0, meta_ref[2 * N_GROUPS], scatter, 0)
        h = h_ref[...] + gate_ref[...] * o_ref[...]
        o_ref[...] = h * lax.rsqrt(jnp.mean(h * h, axis=-1, keepdims=True) + NORM_EPS) * nw_ref[...]


def _moe(xf2, comb, wg, wu, wd, h2, mod4, norm_out_w, seq):
    t, d = xf2.shape
    f = wg.shape[-1]
    tm = MOE_TILE
    per_b = seq // tm
    row_spec = pl.BlockSpec((tm, d), lambda i, e: (i, 0))
    return pl.pallas_call(
        _moe_kernel,
        grid=(t // tm, N_EXPERTS),
        in_specs=[
            row_spec,
            pl.BlockSpec((tm, LANES), lambda i, e: (i, 0)),
            pl.BlockSpec((None, d, f), lambda i, e: (e, 0, 0)),
            pl.BlockSpec((None, d, f), lambda i, e: (e, 0, 0)),
            pl.BlockSpec((None, f, d), lambda i, e: (e, 0, 0)),
            row_spec,
            pl.BlockSpec((None, None, 1, d), lambda i, e: (i // per_b, 5, 0, 0)),
            pl.BlockSpec((1, d), lambda i, e: (0, 0)),
        ],
        out_specs=row_spec,
        out_shape=jax.ShapeDtypeStruct((t, d), F32),
        scratch_shapes=[
            pltpu.VMEM((MOE_SORTED, d), BF16),
            pltpu.VMEM((MOE_SORTED, LANES), F32),
            pltpu.VMEM((MOE_SORTED, d), F32),
            pltpu.VMEM((tm, LANES), jnp.int32),
            pltpu.VMEM((8, tm), jnp.int32),
            pltpu.VMEM((tm, tm), BF16),
            pltpu.SMEM((2 * N_GROUPS + 1,), jnp.int32),
        ],
        compiler_params=pltpu.CompilerParams(dimension_semantics=("arbitrary", "arbitrary"),
                                             vmem_limit_bytes=MOE_VMEM_LIMIT),
        name="moe",
    )(xf2, comb, wg, wu, wd, h2, mod4, norm_out_w.reshape(1, d))


def _pad_lanes(a):
    return jnp.pad(a, ((0, 0), (0, LANES - a.shape[1])))


def _layer(h3, c, mod_w, mod_b, norm_mix_w, w_in, conv_w, a_log, dt_bias, gdn_norm_w, w_out, norm_ffn_w,
           w_group, b_group, w_router, b_router, w_gate, w_up, w_down, norm_out_w):
    b, s, d = h3.shape
    t = b * s
    x2 = h3.reshape(t, d)

    n_gate_cols = 2 * GDN_HEADS
    small_lo = COL_GATE_A
    w_main = jnp.concatenate([w_in[:, :small_lo], w_in[:, small_lo + n_gate_cols:]], axis=1).astype(BF16)
    w_small = _pad_lanes(w_in[:, small_lo:small_lo + n_gate_cols])
    w_route = _pad_lanes(jnp.concatenate([w_group, w_router], axis=1))
    b_route = _pad_lanes(jnp.concatenate([b_group, b_router.reshape(-1)])[None, :])
    alog_row = _pad_lanes(a_log[None, :])
    dtb_row = _pad_lanes(dt_bias[None, :])
    f = w_gate.shape[-1]
    wg = w_gate.reshape(N_EXPERTS, d, f).astype(BF16)
    wu = w_up.reshape(N_EXPERTS, d, f).astype(BF16)
    wd = w_down.reshape(N_EXPERTS, f, d).astype(BF16)

    half = RET_DK // 2
    inv_freq = 1.0 / (ROPE_BASE ** (jnp.arange(half, dtype=F32) / half))
    ang = jnp.arange(s, dtype=F32)[:, None] * inv_freq[None, :]
    cos_t = jnp.concatenate([jnp.cos(ang), jnp.cos(ang)], axis=1)
    sin_t = jnp.concatenate([-jnp.sin(ang), jnp.sin(ang)], axis=1)
    log_gamma = jnp.log(1.0 - 2.0 ** (-5.0 - jnp.arange(RET_HEADS, dtype=F32)))
    log_gamma = jnp.broadcast_to(log_gamma[:, None], (RET_HEADS, RET_DV))

    mod4 = _mod(c, mod_w, mod_b).reshape(b, N_MOD, 1, d)
    proj, gates = _inproj(x2, mod4, norm_mix_w, w_main, w_small, s)
    proj3 = proj.reshape(b, s, N_MAIN)
    ya = _retention(proj3, cos_t, sin_t, log_gamma)
    yb = _gdn(proj3, gates.reshape(b, s, LANES), alog_row, dtb_row, conv_w, gdn_norm_w)
    h2, xf2, comb = _outproj(ya.reshape(t, d), yb.reshape(t, d), x2, w_out.astype(BF16), mod4, norm_ffn_w,
                             w_route, b_route, s)
    return _moe(xf2, comb, wg, wu, wd, h2, mod4, norm_out_w, s).reshape(b, s, d)


def kernel(x, c, mod_w, mod_b, norm_mix_w, w_in, gdn_conv_w, gdn_a_log, gdn_dt_bias, gdn_norm_w, w_out, norm_ffn_w,
           w_group, b_group, w_router, b_router, w_gate, w_up, w_down, norm_out_w):
    assert mod_w.shape[0] == 1, "one residual layer"
    return _layer(x, c, mod_w[0], mod_b[0], norm_mix_w[0], w_in[0], gdn_conv_w[0], gdn_a_log[0], gdn_dt_bias[0],
                  gdn_norm_w[0], w_out[0], norm_ffn_w[0], w_group[0], b_group[0], w_router[0], b_router[0],
                  w_gate[0], w_up[0], w_down[0], norm_out_w)
```

```python
import functools

import jax
import jax.numpy as jnp
from jax import lax
from jax.experimental import pallas as pl
from jax.experimental.pallas import tpu as pltpu

F32 = jnp.float32
BF16 = jnp.bfloat16
HIGHEST = lax.Precision.HIGHEST

RET_HEADS = 4
RET_DK = 128
RET_DV = 256
GDN_HEADS = 4
GDN_DK = 128
GDN_DV = 256
GDN_CHUNK = 64
CONV_K = 4
N_GROUPS = 4
EXPERTS_PER_GROUP = 4
N_EXPERTS = N_GROUPS * EXPERTS_PER_GROUP
D_FF_EXPERT = 512
ROPE_BASE = 10000.0
NORM_EPS = 1e-6
L2_EPS = 1e-6
N_MOD = 6
LANES = 128
SEQ_TILE = 256
VMEM_LIMIT = 48 * 1024 * 1024
MOE_TILE = 1024
MOE_CHUNK = 304
MOE_ROW_ALIGN = 16
MOE_MAX_CHUNKS = (MOE_TILE + N_GROUPS * (MOE_CHUNK - 1)) // MOE_CHUNK
MOE_SORTED = MOE_MAX_CHUNKS * MOE_CHUNK
MOE_SCATTER_BLOCK = 256
MOE_EXPERTS_PER_STEP = 2
MOE_FF_SLAB = 256
MOE_VMEM_LIMIT = 56 * 1024 * 1024

COL_RQ, COL_RK, COL_RV, COL_RG = 0, 512, 1024, 2048
COL_GQ, COL_GK, COL_GV, COL_GZ = 3072, 3584, 4096, 5120
COL_GATE_A, COL_GATE_B = 6144, 7168
N_MAIN = 8192


def _silu(x):
    return x * jax.nn.sigmoid(x)


def _dot(a, b, **kw):
    return jnp.dot(a, b, preferred_element_type=F32, **kw)


def _dot_nt(a, b, **kw):
    return lax.dot_general(a, b, (((1,), (1,)), ((), ())), preferred_element_type=F32, **kw)


def _dot_tn(a, b, **kw):
    return lax.dot_general(a, b, (((0,), (0,)), ((), ())), preferred_element_type=F32, **kw)


def _params(*sem):
    return pltpu.CompilerParams(dimension_semantics=sem, vmem_limit_bytes=VMEM_LIMIT)


def _mod_kernel(c_ref, w_ref, b_ref, o_ref):
    a = _silu(c_ref[...])
    o_ref[...] = _dot(a, w_ref[...], precision=HIGHEST) + b_ref[...]


def _mod(c, mod_w, mod_b):
    b, d = c.shape
    n = mod_w.shape[1]
    tn = d
    return pl.pallas_call(
        _mod_kernel,
        grid=(n // tn,),
        in_specs=[
            pl.BlockSpec((b, d), lambda j: (0, 0)),
            pl.BlockSpec((d, tn), lambda j: (0, j)),
            pl.BlockSpec((1, tn), lambda j: (0, j)),
        ],
        out_specs=pl.BlockSpec((b, tn), lambda j: (0, j)),
        out_shape=jax.ShapeDtypeStruct((b, n), F32),
        compiler_params=_params("arbitrary"),
        name="mod",
    )(c, mod_w, mod_b.reshape(1, n))


def _inproj_kernel(x0_ref, shift0_ref, scale0_ref, xn_ref, shiftn_ref, scalen_ref, nw_ref, w_ref, ws_ref,
                   o_ref, og_ref, xb_even_ref, xb_odd_ref):
    i = pl.program_id(0)
    j = pl.program_id(1)
    tm = x0_ref.shape[0]
    slab = tm // pl.num_programs(1)

    def prepare(x, shift, scale):
        y = x * lax.rsqrt(jnp.mean(x * x, axis=-1, keepdims=True) + NORM_EPS) * nw_ref[...]
        return (y * (1.0 + scale) + shift).astype(BF16)

    @pl.when(jnp.logical_and(i == 0, j == 0))
    def _():
        xb_even_ref[...] = prepare(x0_ref[...], shift0_ref[...], scale0_ref[...])

    def step(cur_ref, nxt_ref):
        rows = pl.ds(pl.multiple_of(j * slab, slab), slab)
        nxt_ref[rows, :] = prepare(xn_ref[rows, :], shiftn_ref[...], scalen_ref[...])
        xb = cur_ref[...]
        o_ref[...] = _dot(xb, w_ref[...]).astype(BF16)

        @pl.when(j == 0)
        def _():
            og_ref[...] = _dot(xb, ws_ref[...])

    @pl.when(i % 2 == 0)
    def _():
        step(xb_even_ref, xb_odd_ref)

    @pl.when(i % 2 == 1)
    def _():
        step(xb_odd_ref, xb_even_ref)


def _inproj(x2, mod4, norm_w, w_main, w_small, seq, tm=1024, tn=1024):
    t, d = x2.shape
    n = w_main.shape[1]
    per_b = seq // tm
    last = t // tm - 1

    def nxt(i):
        return jnp.minimum(i + 1, last)

    return pl.pallas_call(
        _inproj_kernel,
        grid=(t // tm, n // tn),
        in_specs=[
            pl.BlockSpec((tm, d), lambda i, j: (0, 0)),
            pl.BlockSpec((None, None, 1, d), lambda i, j: (0, 0, 0, 0)),
            pl.BlockSpec((None, None, 1, d), lambda i, j: (0, 1, 0, 0)),
            pl.BlockSpec((tm, d), lambda i, j: (nxt(i), 0)),
            pl.BlockSpec((None, None, 1, d), lambda i, j: (nxt(i) // per_b, 0, 0, 0)),
            pl.BlockSpec((None, None, 1, d), lambda i, j: (nxt(i) // per_b, 1, 0, 0)),
            pl.BlockSpec((1, d), lambda i, j: (0, 0)),
            pl.BlockSpec((d, tn), lambda i, j: (0, j)),
            pl.BlockSpec((d, LANES), lambda i, j: (0, 0)),
        ],
        out_specs=[
            pl.BlockSpec((tm, tn), lambda i, j: (i, j)),
            pl.BlockSpec((tm, LANES), lambda i, j: (i, 0)),
        ],
        out_shape=[
            jax.ShapeDtypeStruct((t, n), BF16),
            jax.ShapeDtypeStruct((t, LANES), F32),
        ],
        scratch_shapes=[pltpu.VMEM((tm, d), BF16), pltpu.VMEM((tm, d), BF16)],
        compiler_params=_params("arbitrary", "arbitrary"),
        name="inproj",
    )(x2, mod4, mod4, x2, mod4, mod4, norm_w.reshape(1, d), w_main, w_small.astype(BF16))


def _ret_kernel(lg_ref, q_ref, k_ref, v_ref, rg_ref, ga_ref, cos_ref, sin_ref, o_ref,
                state_ref, intra_ref, qd_ref, kd_ref):
    c = SEQ_TILE
    first = jnp.logical_and(pl.program_id(0) == 0, pl.program_id(1) == 0)

    @pl.when(first)
    def _():
        row = lax.broadcasted_iota(jnp.int32, (c, c), 0)
        col = lax.broadcasted_iota(jnp.int32, (c, c), 1)
        rel = (row - col).astype(F32)
        causal = row >= col
        pos = lax.broadcasted_iota(jnp.int32, (c, RET_DK), 0).astype(F32)
        for h in range(RET_HEADS):
            lg = lg_ref[h:h + 1, :]
            intra_ref[h] = jnp.where(causal, jnp.exp(jnp.where(causal, rel, 0.0) * lg), 0.0)
            qd_ref[h] = jnp.exp((pos + 1.0) * lg[:, :RET_DK])
            kd_ref[h] = jnp.exp((c - 1.0 - pos) * lg[:, :RET_DK])

    @pl.when(pl.program_id(1) == 0)
    def _():
        state_ref[...] = jnp.zeros_like(state_ref)

    cos = cos_ref[...]
    sin = sin_ref[...]
    for h in range(RET_HEADS):
        qs = slice(h * RET_DK, (h + 1) * RET_DK)
        vs = slice(h * RET_DV, (h + 1) * RET_DV)
        qr = q_ref[:, qs].astype(F32)
        kr = k_ref[:, qs].astype(F32)
        q = qr * cos + pltpu.roll(qr, RET_DK // 2, 1) * sin
        k = (kr * cos + pltpu.roll(kr, RET_DK // 2, 1) * sin) * (RET_DK ** -0.5)
        v = v_ref[:, vs]
        state = state_ref[h]
        chunk_decay = jnp.exp(float(c) * lg_ref[h:h + 1, :])
        scores = _dot_nt(q.astype(BF16), k.astype(BF16)) * intra_ref[h]
        o = _dot(scores.astype(BF16), v) + _dot((q * qd_ref[h]).astype(BF16), state.astype(BF16))
        state_ref[h] = state * chunk_decay + _dot_tn((k * kd_ref[h]).astype(BF16), v)
        o = o * lax.rsqrt(jnp.mean(o * o, axis=-1, keepdims=True) + NORM_EPS)
        y = _silu(rg_ref[:, vs].astype(F32)) * o
        o_ref[:, vs] = (jax.nn.sigmoid(ga_ref[:, vs].astype(F32)) * y).astype(BF16)


def _retention(proj3, cos_t, sin_t, log_gamma):
    b, s, _ = proj3.shape
    ts = SEQ_TILE
    qk_w = RET_HEADS * RET_DK
    v_w = RET_HEADS * RET_DV
    return pl.pallas_call(
        _ret_kernel,
        grid=(b, s // ts),
        in_specs=[
            pl.BlockSpec((RET_HEADS, RET_DV), lambda i, j: (0, 0)),
            pl.BlockSpec((None, ts, qk_w), lambda i, j: (i, j, COL_RQ // qk_w)),
            pl.BlockSpec((None, ts, qk_w), lambda i, j: (i, j, COL_RK // qk_w)),
            pl.BlockSpec((None, ts, v_w), lambda i, j: (i, j, COL_RV // v_w)),
            pl.BlockSpec((None, ts, v_w), lambda i, j: (i, j, COL_RG // v_w)),
            pl.BlockSpec((None, ts, v_w), lambda i, j: (i, j, COL_GATE_A // v_w)),
            pl.BlockSpec((ts, RET_DK), lambda i, j: (j, 0)),
            pl.BlockSpec((ts, RET_DK), lambda i, j: (j, 0)),
        ],
        out_specs=pl.BlockSpec((None, ts, v_w), lambda i, j: (i, j, 0)),
        out_shape=jax.ShapeDtypeStruct((b, s, v_w), BF16),
        scratch_shapes=[
            pltpu.VMEM((RET_HEADS, RET_DK, RET_DV), F32),
            pltpu.VMEM((RET_HEADS, ts, ts), F32),
            pltpu.VMEM((RET_HEADS, ts, RET_DK), F32),
            pltpu.VMEM((RET_HEADS, ts, RET_DK), F32),
        ],
        compiler_params=_params("arbitrary", "arbitrary"),
        name="retention",
    )(log_gamma, proj3, proj3, proj3, proj3, proj3, cos_t, sin_t)


def _unit_lower_inverse(a, eye, nilpotency):
    p = eye - a
    m = a
    for _ in range((nilpotency - 1).bit_length() - 1):
        m16 = m.astype(BF16)
        m = _dot(m16, m16)
        p = p + _dot(p.astype(BF16), m.astype(BF16))
    return p


def _gdn_kernel(q_ref, k_ref, v_ref, z_ref, gb_ref, gates_ref, alog_ref, dtb_ref, cwq_ref, cwk_ref, cwv_ref,
                nw_ref, o_ref, state_ref, tail_ref):
    ts = SEQ_TILE
    cc = GDN_CHUNK
    qk_w = GDN_HEADS * GDN_DK

    @pl.when(pl.program_id(1) == 0)
    def _():
        state_ref[...] = jnp.zeros_like(state_ref)
        tail_ref[...] = jnp.zeros_like(tail_ref)

    def conv_silu(cur, tail, cw_ref):
        ext = jnp.concatenate([tail, cur], axis=0)
        out = ext[8 - (CONV_K - 1):8 - (CONV_K - 1) + ts] * cw_ref[0:1, :]
        for i in range(1, CONV_K):
            d = CONV_K - 1 - i
            out = out + ext[8 - d:8 - d + ts] * cw_ref[i:i + 1, :]
        return _silu(out)

    q_raw = q_ref[...].astype(F32)
    k_raw = k_ref[...].astype(F32)
    v_raw = v_ref[...].astype(F32)
    q_all = conv_silu(q_raw, tail_ref[:, 0:qk_w], cwq_ref)
    k_all = conv_silu(k_raw, tail_ref[:, qk_w:2 * qk_w], cwk_ref)
    v_all = conv_silu(v_raw, tail_ref[:, 2 * qk_w:], cwv_ref)
    tail_ref[:, 0:qk_w] = q_raw[ts - 8:]
    tail_ref[:, qk_w:2 * qk_w] = k_raw[ts - 8:]
    tail_ref[:, 2 * qk_w:] = v_raw[ts - 8:]

    gates = gates_ref[...]
    x = gates + dtb_ref[...]
    softplus = jnp.maximum(x, 0.0) + jnp.log1p(jnp.exp(-jnp.abs(x)))
    g_all = -jnp.exp(alog_ref[...]) * softplus
    beta_all = jax.nn.sigmoid(gates)
    row = lax.broadcasted_iota(jnp.int32, (ts, ts), 0)
    col = lax.broadcasted_iota(jnp.int32, (ts, ts), 1)
    same_chunk = (row // cc) == (col // cc)
    causal = jnp.logical_and(same_chunk, row >= col)
    strict = jnp.logical_and(same_chunk, row > col)
    eye = jnp.where(row == col, 1.0, 0.0)
    tri_lower = jnp.where(causal, 1.0, 0.0)
    tri_upper = jnp.where(jnp.logical_and(same_chunk, row <= col), 1.0, 0.0)
    gc_col_all = _dot(tri_lower, g_all, precision=HIGHEST)
    gc_row_all = _dot_tn(g_all, tri_upper, precision=HIGHEST)

    heads = range(GDN_HEADS)
    qn, kn, k16, k_beta, gc, decay, a_mat = [], [], [], [], [], [], []
    for h in heads:
        qs = slice(h * GDN_DK, (h + 1) * GDN_DK)
        qh = q_all[:, qs]
        kh = k_all[:, qs]
        qn.append(qh * lax.rsqrt(jnp.sum(qh * qh, axis=-1, keepdims=True) + L2_EPS) * (GDN_DK ** -0.5))
        kn.append(kh * lax.rsqrt(jnp.sum(kh * kh, axis=-1, keepdims=True) + L2_EPS))
        gc.append(gc_col_all[:, h:h + 1])
        gcr = gc_row_all[h:h + 1, :]
        decay.append(jnp.exp(jnp.where(causal, gc[h] - gcr, -jnp.inf)))
        k_beta.append(kn[h] * beta_all[:, GDN_HEADS + h:GDN_HEADS + h + 1])
        k16.append(kn[h].astype(BF16))
    for h in heads:
        a_mat.append(jnp.where(strict, _dot_nt(k_beta[h].astype(BF16), k16[h]) * decay[h], 0.0))
    p = [eye - a_mat[h] for h in heads]
    m = a_mat
    for _ in range((cc - 1).bit_length() - 1):
        m16 = [m[h].astype(BF16) for h in heads]
        m = [_dot(m16[h], m16[h]) for h in heads]
        p = [p[h] + _dot(p[h].astype(BF16), m[h].astype(BF16)) for h in heads]
    t16 = [p[h].astype(BF16) for h in heads]
    egc = [jnp.exp(gc[h]) for h in heads]
    u_all, w_all, qk_all, qg_all = [], [], [], []
    for h in heads:
        vs = slice(h * GDN_DV, (h + 1) * GDN_DV)
        beta = beta_all[:, GDN_HEADS + h:GDN_HEADS + h + 1]
        u_all.append(_dot(t16[h], (v_all[:, vs] * beta).astype(BF16)))
        w_all.append(_dot(t16[h], (k_beta[h] * egc[h]).astype(BF16)).astype(BF16))
        qk_all.append((_dot_nt(qn[h].astype(BF16), k16[h]) * decay[h]).astype(BF16))
        qg_all.append((qn[h] * egc[h]).astype(BF16))
    state = [state_ref[h] for h in heads]
    outs = [[] for _ in heads]
    for n in range(ts // cc):
        rs = slice(n * cc, (n + 1) * cc)
        for h in heads:
            state_b = state[h].astype(BF16)
            v_new = (u_all[h][rs] - _dot(w_all[h][rs], state_b)).astype(BF16)
            outs[h].append(_dot(qg_all[h][rs], state_b) + _dot(qk_all[h][rs, rs], v_new))
            g_last = gc[h][(n + 1) * cc - 1:(n + 1) * cc, :]
            k_g = kn[h][rs] * jnp.exp(g_last - gc[h][rs])
            state[h] = state[h] * jnp.exp(g_last) + _dot_tn(k_g.astype(BF16), v_new)
    for h in heads:
        vs = slice(h * GDN_DV, (h + 1) * GDN_DV)
        state_ref[h] = state[h]
        o = jnp.concatenate(outs[h], axis=0)
        o = o * lax.rsqrt(jnp.mean(o * o, axis=-1, keepdims=True) + NORM_EPS) * nw_ref[...]
        o = o * _silu(z_ref[:, vs].astype(F32))
        o_ref[:, vs] = (jax.nn.sigmoid(gb_ref[:, vs].astype(F32)) * o).astype(BF16)


def _gdn(proj3, gates3, alog_row, dtb_row, conv_w, norm_w):
    b, s, _ = proj3.shape
    ts = SEQ_TILE
    qk_w = GDN_HEADS * GDN_DK
    v_w = GDN_HEADS * GDN_DV
    return pl.pallas_call(
        _gdn_kernel,
        grid=(b, s // ts),
        in_specs=[
            pl.BlockSpec((None, ts, qk_w), lambda i, j: (i, j, COL_GQ // qk_w)),
            pl.BlockSpec((None, ts, qk_w), lambda i, j: (i, j, COL_GK // qk_w)),
            pl.BlockSpec((None, ts, v_w), lambda i, j: (i, j, COL_GV // v_w)),
            pl.BlockSpec((None, ts, v_w), lambda i, j: (i, j, COL_GZ // v_w)),
            pl.BlockSpec((None, ts, v_w), lambda i, j: (i, j, COL_GATE_B // v_w)),
            pl.BlockSpec((None, ts, LANES), lambda i, j: (i, j, 0)),
            pl.BlockSpec((1, LANES), lambda i, j: (0, 0)),
            pl.BlockSpec((1, LANES), lambda i, j: (0, 0)),
            pl.BlockSpec((CONV_K, qk_w), lambda i, j: (0, 0)),
            pl.BlockSpec((CONV_K, qk_w), lambda i, j: (0, 1)),
            pl.BlockSpec((CONV_K, v_w), lambda i, j: (0, 1)),
            pl.BlockSpec((1, GDN_DV), lambda i, j: (0, 0)),
        ],
        out_specs=pl.BlockSpec((None, ts, v_w), lambda i, j: (i, j, 0)),
        out_shape=jax.ShapeDtypeStruct((b, s, v_w), BF16),
        scratch_shapes=[
            pltpu.VMEM((GDN_HEADS, GDN_DK, GDN_DV), F32),
            pltpu.VMEM((8, 2 * qk_w + v_w), F32),
        ],
        compiler_params=_params("arbitrary", "arbitrary"),
        name="gdn",
    )(proj3, proj3, proj3, proj3, proj3, gates3, alog_row, dtb_row, conv_w, conv_w, conv_w,
      norm_w.reshape(1, GDN_DV))


def _outproj_kernel(ya_ref, yb_ref, x_ref, w_ref, gate_ref, nw_ref, shift_ref, scale_ref, wr_ref, br_ref,
                    h_ref, xf_ref, comb_ref):
    merged = (ya_ref[...].astype(F32) + yb_ref[...].astype(F32)).astype(BF16)
    h = x_ref[...] + gate_ref[...] * _dot(merged, w_ref[...])
    h_ref[...] = h
    y = h * lax.rsqrt(jnp.mean(h * h, axis=-1, keepdims=True) + NORM_EPS) * nw_ref[...]
    xf = y * (1.0 + scale_ref[...]) + shift_ref[...]
    xf_ref[...] = xf.astype(BF16)

    logits = _dot(xf, wr_ref[...], precision=HIGHEST) + br_ref[...]
    lane = lax.broadcasted_iota(jnp.int32, logits.shape, 1)
    neg = jnp.float32(-jnp.inf)

    def masked_softmax(mask):
        m = jnp.max(jnp.where(mask, logits, neg), axis=-1, keepdims=True)
        e = jnp.where(mask, jnp.exp(jnp.where(mask, logits, m) - m), 0.0)
        return e / jnp.sum(e, axis=-1, keepdims=True)

    def first_argmax(p, mask):
        top = jnp.max(jnp.where(mask, p, -1.0), axis=-1, keepdims=True)
        idx = jnp.min(jnp.where(jnp.logical_and(mask, p == top), lane, LANES), axis=-1, keepdims=True)
        return top, idx

    gmask = lane < N_GROUPS
    g_top, g_idx = first_argmax(masked_softmax(gmask), gmask)
    lo = N_GROUPS + EXPERTS_PER_GROUP * g_idx
    emask = jnp.logical_and(lane >= lo, lane < lo + EXPERTS_PER_GROUP)
    pe = masked_softmax(emask)
    top1, i1 = first_argmax(pe, emask)
    emask2 = jnp.logical_and(emask, lane != i1)
    top2, i2 = first_argmax(pe, emask2)
    denom = top1 + top2
    comb = jnp.where(lane == i1, g_top * (top1 / denom), jnp.where(lane == i2, g_top * (top2 / denom), 0.0))
    comb_ref[...] = jnp.where(lane == 0, g_idx.astype(F32), comb)


def _outproj(ya2, yb2, x2, w_out_b, mod4, norm_w, w_route, b_route, seq, tm=512):
    t, d = x2.shape
    per_b = seq // tm
    row_spec = pl.BlockSpec((tm, d), lambda i: (i, 0))

    def mod_spec(k):
        return pl.BlockSpec((None, None, 1, d), lambda i: (i // per_b, k, 0, 0))

    return pl.pallas_call(
        _outproj_kernel,
        grid=(t // tm,),
        in_specs=[
            row_spec, row_spec, row_spec,
            pl.BlockSpec((d, d), lambda i: (0, 0)),
            mod_spec(2),
            pl.BlockSpec((1, d), lambda i: (0, 0)),
            mod_spec(3),
            mod_spec(4),
            pl.BlockSpec((d, LANES), lambda i: (0, 0)),
            pl.BlockSpec((1, LANES), lambda i: (0, 0)),
        ],
        out_specs=[row_spec, row_spec, pl.BlockSpec((tm, LANES), lambda i: (i, 0))],
        out_shape=[
            jax.ShapeDtypeStruct((t, d), F32),
            jax.ShapeDtypeStruct((t, d), BF16),
            jax.ShapeDtypeStruct((t, LANES), F32),
        ],
        compiler_params=_params("arbitrary"),
        name="outproj",
    )(ya2, yb2, x2, w_out_b, mod4, norm_w.reshape(1, d), mod4, mod4, w_route, b_route)


def _moe_kernel(xf_ref, comb_ref, wg_ref, wu_ref, wd_ref, h_ref, gate_ref, nw_ref, o_ref,
                xs_ref, cs_ref, acc_ref, posc_ref, posr_ref, ltri_ref, meta_ref):
    i = pl.program_id(0)
    p = pl.program_id(1)
    tm = MOE_TILE
    r = MOE_CHUNK
    steps_per_group = EXPERTS_PER_GROUP // MOE_EXPERTS_PER_STEP
    g = p // steps_per_group

    @pl.when(jnp.logical_and(i == 0, p == 0))
    def _():
        row = lax.broadcasted_iota(jnp.int32, (tm, tm), 0)
        col = lax.broadcasted_iota(jnp.int32, (tm, tm), 1)
        ltri_ref[...] = jnp.where(row >= col, 1.0, 0.0).astype(BF16)

    def chunk_rows(c):
        return pl.ds(pl.multiple_of(c * r, MOE_ROW_ALIGN), r)

    @pl.when(p == 0)
    def _():
        comb = comb_ref[...]
        lane = lax.broadcasted_iota(jnp.int32, (tm, LANES), 1)
        gidx = comb[:, 0:1]
        mine = jnp.logical_and(lane.astype(F32) == gidx, lane < N_GROUPS)
        csum = _dot(ltri_ref[...], jnp.where(mine, 1.0, 0.0).astype(BF16))
        counts = jnp.broadcast_to(csum[tm - 1:tm, :], (8, LANES))
        nch = jnp.floor((counts + (r - 0.5)) * (1.0 / r))
        lr = lax.broadcasted_iota(jnp.int32, (LANES, LANES), 0)
        lc = lax.broadcasted_iota(jnp.int32, (LANES, LANES), 1)
        cstart = _dot(nch, jnp.where(lr < lc, 1.0, 0.0), precision=HIGHEST)
        pos = jnp.sum(jnp.where(mine, cstart[0:1, :] * r + csum - 1.0, 0.0), axis=-1, keepdims=True)
        pos_b = jnp.broadcast_to(pos, (tm, LANES))
        posc_ref[...] = pos_b.astype(jnp.int32)
        posr_ref[...] = pos_b.T[0:8, :].astype(jnp.int32)
        lane1 = lax.broadcasted_iota(jnp.int32, (8, LANES), 1)
        for k in range(N_GROUPS):
            meta_ref[k] = jnp.sum(jnp.where(lane1 == k, nch, 0.0)[0:1, :]).astype(jnp.int32)
            meta_ref[N_GROUPS + k] = jnp.sum(jnp.where(lane1 == k, cstart, 0.0)[0:1, :]).astype(jnp.int32)
        meta_ref[2 * N_GROUPS] = jnp.sum(jnp.where(lane1 < N_GROUPS, nch, 0.0)[0:1, :]).astype(jnp.int32)

        c_hi = comb.astype(BF16)
        c_lo = (comb - c_hi.astype(F32)).astype(BF16)
        xf = xf_ref[...]
        pos_row = posr_ref[0:1, :]
        sub = lax.broadcasted_iota(jnp.int32, (r, tm), 0)

        def gather(c, carry):
            p_c = jnp.where(pos_row == sub + c * r, 1.0, 0.0).astype(BF16)
            rows = chunk_rows(c)
            xs_ref[rows, :] = _dot(p_c, xf).astype(BF16)
            cs_ref[rows, :] = _dot(p_c, c_hi) + _dot(p_c, c_lo)
            return carry

        lax.fori_loop(0, meta_ref[2 * N_GROUPS], gather, 0)

    lane_r = lax.broadcasted_iota(jnp.int32, (r, LANES), 1)
    first_of_group = p % steps_per_group == 0

    def expert_chunk(c, carry):
        rows = chunk_rows(c)
        xs = xs_ref[rows, :]
        cs = cs_ref[rows, :]
        contrib = None
        for j in range(MOE_EXPERTS_PER_STEP):
            e = p * MOE_EXPERTS_PER_STEP + j
            wcol = jnp.sum(jnp.where(lane_r == N_GROUPS + e, cs, 0.0), axis=-1, keepdims=True)
            for f0 in range(0, D_FF_EXPERT, MOE_FF_SLAB):
                fs = slice(f0, f0 + MOE_FF_SLAB)
                act = _silu(_dot(xs, wg_ref[j, :, fs])) * _dot(xs, wu_ref[j, :, fs]) * wcol
                part = _dot(act.astype(BF16), wd_ref[j, fs, :])
                contrib = part if contrib is None else contrib + part

        @pl.when(first_of_group)
        def _():
            acc_ref[rows, :] = contrib

        @pl.when(jnp.logical_not(first_of_group))
        def _():
            acc_ref[rows, :] += contrib

        return carry

    c0 = meta_ref[N_GROUPS + g]
    lax.fori_loop(c0, c0 + meta_ref[g], expert_chunk, 0)

    @pl.when(p == pl.num_programs(1) - 1)
    def _():
        blk = MOE_SCATTER_BLOCK
        total = meta_ref[2 * N_GROUPS]
        acc_ref[chunk_rows(total), :] = jnp.zeros((r, acc_ref.shape[1]), F32)
        pos_col = posc_ref[:, 0:1]
        lane_c = lax.broadcasted_iota(jnp.int32, (tm, blk), 1)
        o_ref[...] = jnp.zeros_like(o_ref)

        def scatter(b, carry):
            pt_b = jnp.where(pos_col == lane_c + b * blk, 1.0, 0.0).astype(BF16)
            rows = pl.ds(pl.multiple_of(b * blk, blk), blk)
            o_ref[...] += _dot(pt_b, acc_ref[rows, :].astype(BF16))
            return carry

        lax.fori_loop(0, (total * r + (blk - 1)) // blk, scatter, 0)
        h = h_ref[...] + gate_ref[...] * o_ref[...]
        o_ref[...] = h * lax.rsqrt(jnp.mean(h * h, axis=-1, keepdims=True) + NORM_EPS) * nw_ref[...]


def _moe(xf2, comb, wg, wu, wd, h2, mod4, norm_out_w, seq):
    t, d = xf2.shape
    f = wg.shape[-1]
    tm = MOE_TILE
    per_b = seq // tm
    eps = MOE_EXPERTS_PER_STEP
    row_spec = pl.BlockSpec((tm, d), lambda i, p: (i, 0))
    return pl.pallas_call(
        _moe_kernel,
        grid=(t // tm, N_EXPERTS // eps),
        in_specs=[
            row_spec,
            pl.BlockSpec((tm, LANES), lambda i, p: (i, 0)),
            pl.BlockSpec((eps, d, f), lambda i, p: (p, 0, 0)),
            pl.BlockSpec((eps, d, f), lambda i, p: (p, 0, 0)),
            pl.BlockSpec((eps, f, d), lambda i, p: (p, 0, 0)),
            row_spec,
            pl.BlockSpec((None, None, 1, d), lambda i, p: (i // per_b, 5, 0, 0)),
            pl.BlockSpec((1, d), lambda i, p: (0, 0)),
        ],
        out_specs=row_spec,
        out_shape=jax.ShapeDtypeStruct((t, d), F32),
        scratch_shapes=[
            pltpu.VMEM((MOE_SORTED, d), BF16),
            pltpu.VMEM((MOE_SORTED, LANES), F32),
            pltpu.VMEM((MOE_SORTED + MOE_CHUNK, d), F32),
            pltpu.VMEM((tm, LANES), jnp.int32),
            pltpu.VMEM((8, tm), jnp.int32),
            pltpu.VMEM((tm, tm), BF16),
            pltpu.SMEM((2 * N_GROUPS + 1,), jnp.int32),
        ],
        compiler_params=pltpu.CompilerParams(dimension_semantics=("arbitrary", "arbitrary"),
                                             vmem_limit_bytes=MOE_VMEM_LIMIT),
        name="moe",
    )(xf2, comb, wg, wu, wd, h2, mod4, norm_out_w.reshape(1, d))


def _pad_lanes(a):
    return jnp.pad(a, ((0, 0), (0, LANES - a.shape[1])))


def _layer(h3, c, mod_w, mod_b, norm_mix_w, w_in, conv_w, a_log, dt_bias, gdn_norm_w, w_out, norm_ffn_w,
           w_group, b_group, w_router, b_router, w_gate, w_up, w_down, norm_out_w):
    b, s, d = h3.shape
    t = b * s
    x2 = h3.reshape(t, d)

    n_gate_cols = 2 * GDN_HEADS
    small_lo = COL_GATE_A
    w_main = jnp.concatenate([w_in[:, :small_lo], w_in[:, small_lo + n_gate_cols:]], axis=1).astype(BF16)
    w_small = _pad_lanes(w_in[:, small_lo:small_lo + n_gate_cols])
    w_route = _pad_lanes(jnp.concatenate([w_group, w_router], axis=1))
    b_route = _pad_lanes(jnp.concatenate([b_group, b_router.reshape(-1)])[None, :])
    alog_row = _pad_lanes(a_log[None, :])
    dtb_row = _pad_lanes(dt_bias[None, :])
    f = w_gate.shape[-1]
    wg = w_gate.reshape(N_EXPERTS, d, f).astype(BF16)
    wu = w_up.reshape(N_EXPERTS, d, f).astype(BF16)
    wd = w_down.reshape(N_EXPERTS, f, d).astype(BF16)

    half = RET_DK // 2
    inv_freq = 1.0 / (ROPE_BASE ** (jnp.arange(half, dtype=F32) / half))
    ang = jnp.arange(s, dtype=F32)[:, None] * inv_freq[None, :]
    cos_t = jnp.concatenate([jnp.cos(ang), jnp.cos(ang)], axis=1)
    sin_t = jnp.concatenate([-jnp.sin(ang), jnp.sin(ang)], axis=1)
    log_gamma = jnp.log(1.0 - 2.0 ** (-5.0 - jnp.arange(RET_HEADS, dtype=F32)))
    log_gamma = jnp.broadcast_to(log_gamma[:, None], (RET_HEADS, RET_DV))

    mod4 = _mod(c, mod_w, mod_b).reshape(b, N_MOD, 1, d)
    proj, gates = _inproj(x2, mod4, norm_mix_w, w_main, w_small, s)
    proj3 = proj.reshape(b, s, N_MAIN)
    ya = _retention(proj3, cos_t, sin_t, log_gamma)
    yb = _gdn(proj3, gates.reshape(b, s, LANES), alog_row, dtb_row, conv_w, gdn_norm_w)
    h2, xf2, comb = _outproj(ya.reshape(t, d), yb.reshape(t, d), x2, w_out.astype(BF16), mod4, norm_ffn_w,
                             w_route, b_route, s)
    return _moe(xf2, comb, wg, wu, wd, h2, mod4, norm_out_w, s).reshape(b, s, d)


def kernel(x, c, mod_w, mod_b, norm_mix_w, w_in, gdn_conv_w, gdn_a_log, gdn_dt_bias, gdn_norm_w, w_out, norm_ffn_w,
           w_group, b_group, w_router, b_router, w_gate, w_up, w_down, norm_out_w):
    assert mod_w.shape[0] == 1, "one residual layer"
    return _layer(x, c, mod_w[0], mod_b[0], norm_mix_w[0], w_in[0], gdn_conv_w[0], gdn_a_log[0], gdn_dt_bias[0],
                  gdn_norm_w[0], w_out[0], norm_ffn_w[0], w_group[0], b_group[0], w_router[0], b_router[0],
                  w_gate[0], w_up[0], w_down[0], norm_out_w)
```

```python
import functools

import jax
import jax.numpy as jnp
from jax import lax
from jax.experimental import pallas as pl
from jax.experimental.pallas import tpu as pltpu

F32 = jnp.float32
BF16 = jnp.bfloat16
HIGHEST = lax.Precision.HIGHEST

RET_HEADS = 4
RET_DK = 128
RET_DV = 256
GDN_HEADS = 4
GDN_DK = 128
GDN_DV = 256
GDN_CHUNK = 64
CONV_K = 4
N_GROUPS = 4
EXPERTS_PER_GROUP = 4
N_EXPERTS = N_GROUPS * EXPERTS_PER_GROUP
D_FF_EXPERT = 512
ROPE_BASE = 10000.0
NORM_EPS = 1e-6
L2_EPS = 1e-6
N_MOD = 6
LANES = 128
SEQ_TILE = 256
VMEM_LIMIT = 48 * 1024 * 1024
MOE_TILE = 1024
MOE_CHUNK = 304
MOE_ROW_ALIGN = 16
MOE_MAX_CHUNKS = (MOE_TILE + N_GROUPS * (MOE_CHUNK - 1)) // MOE_CHUNK
MOE_SORTED = MOE_MAX_CHUNKS * MOE_CHUNK
MOE_SCATTER_BLOCK = 256
MOE_EXPERTS_PER_STEP = 2
MOE_FF_SLAB = 256
MOE_VMEM_LIMIT = 56 * 1024 * 1024

COL_RQ, COL_RK, COL_RV, COL_RG = 0, 512, 1024, 2048
COL_GQ, COL_GK, COL_GV, COL_GZ = 3072, 3584, 4096, 5120
COL_GATE_A, COL_GATE_B = 6144, 7168
N_MAIN = 8192


def _silu(x):
    return x * jax.nn.sigmoid(x)


def _dot(a, b, **kw):
    return jnp.dot(a, b, preferred_element_type=F32, **kw)


def _dot_nt(a, b, **kw):
    return lax.dot_general(a, b, (((1,), (1,)), ((), ())), preferred_element_type=F32, **kw)


def _dot_tn(a, b, **kw):
    return lax.dot_general(a, b, (((0,), (0,)), ((), ())), preferred_element_type=F32, **kw)


def _params(*sem):
    return pltpu.CompilerParams(dimension_semantics=sem, vmem_limit_bytes=VMEM_LIMIT)


def _mod_kernel(c_ref, w_ref, b_ref, o_ref):
    a = _silu(c_ref[...])
    o_ref[...] = _dot(a, w_ref[...], precision=HIGHEST) + b_ref[...]


def _mod(c, mod_w, mod_b):
    b, d = c.shape
    n = mod_w.shape[1]
    tn = d
    return pl.pallas_call(
        _mod_kernel,
        grid=(n // tn,),
        in_specs=[
            pl.BlockSpec((b, d), lambda j: (0, 0)),
            pl.BlockSpec((d, tn), lambda j: (0, j)),
            pl.BlockSpec((1, tn), lambda j: (0, j)),
        ],
        out_specs=pl.BlockSpec((b, tn), lambda j: (0, j)),
        out_shape=jax.ShapeDtypeStruct((b, n), F32),
        compiler_params=_params("arbitrary"),
        name="mod",
    )(c, mod_w, mod_b.reshape(1, n))


def _inproj_kernel(x0_ref, shift0_ref, scale0_ref, xn_ref, shiftn_ref, scalen_ref, nw_ref, w_ref, ws_ref,
                   o_ref, og_ref, xb_even_ref, xb_odd_ref):
    i = pl.program_id(0)
    j = pl.program_id(1)
    tm = x0_ref.shape[0]
    slab = tm // pl.num_programs(1)

    def prepare(x, shift, scale):
        y = x * lax.rsqrt(jnp.mean(x * x, axis=-1, keepdims=True) + NORM_EPS) * nw_ref[...]
        return (y * (1.0 + scale) + shift).astype(BF16)

    @pl.when(jnp.logical_and(i == 0, j == 0))
    def _():
        xb_even_ref[...] = prepare(x0_ref[...], shift0_ref[...], scale0_ref[...])

    def step(cur_ref, nxt_ref):
        rows = pl.ds(pl.multiple_of(j * slab, slab), slab)
        nxt_ref[rows, :] = prepare(xn_ref[rows, :], shiftn_ref[...], scalen_ref[...])
        xb = cur_ref[...]
        o_ref[...] = _dot(xb, w_ref[...]).astype(BF16)

        @pl.when(j == 0)
        def _():
            og_ref[...] = _dot(xb, ws_ref[...])

    @pl.when(i % 2 == 0)
    def _():
        step(xb_even_ref, xb_odd_ref)

    @pl.when(i % 2 == 1)
    def _():
        step(xb_odd_ref, xb_even_ref)


def _inproj(x2, mod4, norm_w, w_main, w_small, seq, tm=1024, tn=2048):
    t, d = x2.shape
    n = w_main.shape[1]
    per_b = seq // tm
    last = t // tm - 1

    def nxt(i):
        return jnp.minimum(i + 1, last)

    return pl.pallas_call(
        _inproj_kernel,
        grid=(t // tm, n // tn),
        in_specs=[
            pl.BlockSpec((tm, d), lambda i, j: (0, 0)),
            pl.BlockSpec((None, None, 1, d), lambda i, j: (0, 0, 0, 0)),
            pl.BlockSpec((None, None, 1, d), lambda i, j: (0, 1, 0, 0)),
            pl.BlockSpec((tm, d), lambda i, j: (nxt(i), 0)),
            pl.BlockSpec((None, None, 1, d), lambda i, j: (nxt(i) // per_b, 0, 0, 0)),
            pl.BlockSpec((None, None, 1, d), lambda i, j: (nxt(i) // per_b, 1, 0, 0)),
            pl.BlockSpec((1, d), lambda i, j: (0, 0)),
            pl.BlockSpec((d, tn), lambda i, j: (0, j)),
            pl.BlockSpec((d, LANES), lambda i, j: (0, 0)),
        ],
        out_specs=[
            pl.BlockSpec((tm, tn), lambda i, j: (i, j)),
            pl.BlockSpec((tm, LANES), lambda i, j: (i, 0)),
        ],
        out_shape=[
            jax.ShapeDtypeStruct((t, n), BF16),
            jax.ShapeDtypeStruct((t, LANES), F32),
        ],
        scratch_shapes=[pltpu.VMEM((tm, d), BF16), pltpu.VMEM((tm, d), BF16)],
        compiler_params=_params("arbitrary", "arbitrary"),
        name="inproj",
    )(x2, mod4, mod4, x2, mod4, mod4, norm_w.reshape(1, d), w_main, w_small.astype(BF16))


def _ret_kernel(lg_ref, q_ref, k_ref, v_ref, rg_ref, ga_ref, cos_ref, sin_ref, o_ref,
                state_ref, intra_ref, qd_ref, kd_ref):
    c = SEQ_TILE
    first = jnp.logical_and(pl.program_id(0) == 0, pl.program_id(1) == 0)

    @pl.when(first)
    def _():
        row = lax.broadcasted_iota(jnp.int32, (c, c), 0)
        col = lax.broadcasted_iota(jnp.int32, (c, c), 1)
        rel = (row - col).astype(F32)
        causal = row >= col
        pos = lax.broadcasted_iota(jnp.int32, (c, RET_DK), 0).astype(F32)
        for h in range(RET_HEADS):
            lg = lg_ref[h:h + 1, :]
            intra_ref[h] = jnp.where(causal, jnp.exp(jnp.where(causal, rel, 0.0) * lg), 0.0)
            qd_ref[h] = jnp.exp((pos + 1.0) * lg[:, :RET_DK])
            kd_ref[h] = jnp.exp((c - 1.0 - pos) * lg[:, :RET_DK])

    @pl.when(pl.program_id(1) == 0)
    def _():
        state_ref[...] = jnp.zeros_like(state_ref)

    cos = cos_ref[...]
    sin = sin_ref[...]
    for h in range(RET_HEADS):
        qs = slice(h * RET_DK, (h + 1) * RET_DK)
        vs = slice(h * RET_DV, (h + 1) * RET_DV)
        qr = q_ref[:, qs].astype(F32)
        kr = k_ref[:, qs].astype(F32)
        q = qr * cos + pltpu.roll(qr, RET_DK // 2, 1) * sin
        k = (kr * cos + pltpu.roll(kr, RET_DK // 2, 1) * sin) * (RET_DK ** -0.5)
        v = v_ref[:, vs]
        state = state_ref[h]
        chunk_decay = jnp.exp(float(c) * lg_ref[h:h + 1, :])
        scores = _dot_nt(q.astype(BF16), k.astype(BF16)) * intra_ref[h]
        o = _dot(scores.astype(BF16), v) + _dot((q * qd_ref[h]).astype(BF16), state.astype(BF16))
        state_ref[h] = state * chunk_decay + _dot_tn((k * kd_ref[h]).astype(BF16), v)
        o = o * lax.rsqrt(jnp.mean(o * o, axis=-1, keepdims=True) + NORM_EPS)
        y = _silu(rg_ref[:, vs].astype(F32)) * o
        o_ref[:, vs] = (jax.nn.sigmoid(ga_ref[:, vs].astype(F32)) * y).astype(BF16)


def _retention(proj3, cos_t, sin_t, log_gamma):
    b, s, _ = proj3.shape
    ts = SEQ_TILE
    qk_w = RET_HEADS * RET_DK
    v_w = RET_HEADS * RET_DV
    return pl.pallas_call(
        _ret_kernel,
        grid=(b, s // ts),
        in_specs=[
            pl.BlockSpec((RET_HEADS, RET_DV), lambda i, j: (0, 0)),
            pl.BlockSpec((None, ts, qk_w), lambda i, j: (i, j, COL_RQ // qk_w)),
            pl.BlockSpec((None, ts, qk_w), lambda i, j: (i, j, COL_RK // qk_w)),
            pl.BlockSpec((None, ts, v_w), lambda i, j: (i, j, COL_RV // v_w)),
            pl.BlockSpec((None, ts, v_w), lambda i, j: (i, j, COL_RG // v_w)),
            pl.BlockSpec((None, ts, v_w), lambda i, j: (i, j, COL_GATE_A // v_w)),
            pl.BlockSpec((ts, RET_DK), lambda i, j: (j, 0)),
            pl.BlockSpec((ts, RET_DK), lambda i, j: (j, 0)),
        ],
        out_specs=pl.BlockSpec((None, ts, v_w), lambda i, j: (i, j, 0)),
        out_shape=jax.ShapeDtypeStruct((b, s, v_w), BF16),
        scratch_shapes=[
            pltpu.VMEM((RET_HEADS, RET_DK, RET_DV), F32),
            pltpu.VMEM((RET_HEADS, ts, ts), F32),
            pltpu.VMEM((RET_HEADS, ts, RET_DK), F32),
            pltpu.VMEM((RET_HEADS, ts, RET_DK), F32),
        ],
        compiler_params=_params("arbitrary", "arbitrary"),
        name="retention",
    )(log_gamma, proj3, proj3, proj3, proj3, proj3, cos_t, sin_t)


def _unit_lower_inverse(a, eye, nilpotency):
    p = eye - a
    m = a
    for _ in range((nilpotency - 1).bit_length() - 1):
        m16 = m.astype(BF16)
        m = _dot(m16, m16)
        p = p + _dot(p.astype(BF16), m.astype(BF16))
    return p


def _gdn_kernel(q_ref, k_ref, v_ref, z_ref, gb_ref, gates_ref, alog_ref, dtb_ref, cwq_ref, cwk_ref, cwv_ref,
                nw_ref, o_ref, state_ref, tail_ref):
    ts = SEQ_TILE
    cc = GDN_CHUNK
    qk_w = GDN_HEADS * GDN_DK

    @pl.when(pl.program_id(1) == 0)
    def _():
        state_ref[...] = jnp.zeros_like(state_ref)
        tail_ref[...] = jnp.zeros_like(tail_ref)

    def conv_silu(cur, tail, cw_ref):
        ext = jnp.concatenate([tail, cur], axis=0)
        out = ext[8 - (CONV_K - 1):8 - (CONV_K - 1) + ts] * cw_ref[0:1, :]
        for i in range(1, CONV_K):
            d = CONV_K - 1 - i
            out = out + ext[8 - d:8 - d + ts] * cw_ref[i:i + 1, :]
        return _silu(out)

    q_raw = q_ref[...].astype(F32)
    k_raw = k_ref[...].astype(F32)
    v_raw = v_ref[...].astype(F32)
    q_all = conv_silu(q_raw, tail_ref[:, 0:qk_w], cwq_ref)
    k_all = conv_silu(k_raw, tail_ref[:, qk_w:2 * qk_w], cwk_ref)
    v_all = conv_silu(v_raw, tail_ref[:, 2 * qk_w:], cwv_ref)
    tail_ref[:, 0:qk_w] = q_raw[ts - 8:]
    tail_ref[:, qk_w:2 * qk_w] = k_raw[ts - 8:]
    tail_ref[:, 2 * qk_w:] = v_raw[ts - 8:]

    gates = gates_ref[...]
    x = gates + dtb_ref[...]
    softplus = jnp.maximum(x, 0.0) + jnp.log1p(jnp.exp(-jnp.abs(x)))
    g_all = -jnp.exp(alog_ref[...]) * softplus
    beta_all = jax.nn.sigmoid(gates)
    row = lax.broadcasted_iota(jnp.int32, (ts, ts), 0)
    col = lax.broadcasted_iota(jnp.int32, (ts, ts), 1)
    same_chunk = (row // cc) == (col // cc)
    causal = jnp.logical_and(same_chunk, row >= col)
    strict = jnp.logical_and(same_chunk, row > col)
    eye = jnp.where(row == col, 1.0, 0.0)
    tri_lower = jnp.where(causal, 1.0, 0.0)
    tri_upper = jnp.where(jnp.logical_and(same_chunk, row <= col), 1.0, 0.0)
    gc_col_all = _dot(tri_lower, g_all, precision=HIGHEST)
    gc_row_all = _dot_tn(g_all, tri_upper, precision=HIGHEST)

    heads = range(GDN_HEADS)
    qn, kn, k16, k_beta, gc, decay, a_mat = [], [], [], [], [], [], []
    for h in heads:
        qs = slice(h * GDN_DK, (h + 1) * GDN_DK)
        qh = q_all[:, qs]
        kh = k_all[:, qs]
        qn.append(qh * lax.rsqrt(jnp.sum(qh * qh, axis=-1, keepdims=True) + L2_EPS) * (GDN_DK ** -0.5))
        kn.append(kh * lax.rsqrt(jnp.sum(kh * kh, axis=-1, keepdims=True) + L2_EPS))
        gc.append(gc_col_all[:, h:h + 1])
        gcr = gc_row_all[h:h + 1, :]
        decay.append(jnp.exp(jnp.where(causal, gc[h] - gcr, -jnp.inf)))
        k_beta.append(kn[h] * beta_all[:, GDN_HEADS + h:GDN_HEADS + h + 1])
        k16.append(kn[h].astype(BF16))
    for h in heads:
        a_mat.append(jnp.where(strict, _dot_nt(k_beta[h].astype(BF16), k16[h]) * decay[h], 0.0))
    p = [eye - a_mat[h] for h in heads]
    m = a_mat
    for _ in range((cc - 1).bit_length() - 1):
        m16 = [m[h].astype(BF16) for h in heads]
        m = [_dot(m16[h], m16[h]) for h in heads]
        p = [p[h] + _dot(p[h].astype(BF16), m[h].astype(BF16)) for h in heads]
    t16 = [p[h].astype(BF16) for h in heads]
    egc = [jnp.exp(gc[h]) for h in heads]
    u_all, w_all, qk_all, qg_all = [], [], [], []
    for h in heads:
        vs = slice(h * GDN_DV, (h + 1) * GDN_DV)
        beta = beta_all[:, GDN_HEADS + h:GDN_HEADS + h + 1]
        u_all.append(_dot(t16[h], (v_all[:, vs] * beta).astype(BF16)))
        w_all.append(_dot(t16[h], (k_beta[h] * egc[h]).astype(BF16)).astype(BF16))
        qk_all.append((_dot_nt(qn[h].astype(BF16), k16[h]) * decay[h]).astype(BF16))
        qg_all.append((qn[h] * egc[h]).astype(BF16))
    state = [state_ref[h] for h in heads]
    outs = [[] for _ in heads]
    for n in range(ts // cc):
        rs = slice(n * cc, (n + 1) * cc)
        for h in heads:
            state_b = state[h].astype(BF16)
            v_new = (u_all[h][rs] - _dot(w_all[h][rs], state_b)).astype(BF16)
            outs[h].append(_dot(qg_all[h][rs], state_b) + _dot(qk_all[h][rs, rs], v_new))
            g_last = gc[h][(n + 1) * cc - 1:(n + 1) * cc, :]
            k_g = kn[h][rs] * jnp.exp(g_last - gc[h][rs])
            state[h] = state[h] * jnp.exp(g_last) + _dot_tn(k_g.astype(BF16), v_new)
    for h in heads:
        vs = slice(h * GDN_DV, (h + 1) * GDN_DV)
        state_ref[h] = state[h]
        o = jnp.concatenate(outs[h], axis=0)
        o = o * lax.rsqrt(jnp.mean(o * o, axis=-1, keepdims=True) + NORM_EPS) * nw_ref[...]
        o = o * _silu(z_ref[:, vs].astype(F32))
        o_ref[:, vs] = (jax.nn.sigmoid(gb_ref[:, vs].astype(F32)) * o).astype(BF16)


def _gdn(proj3, gates3, alog_row, dtb_row, conv_w, norm_w):
    b, s, _ = proj3.shape
    ts = SEQ_TILE
    qk_w = GDN_HEADS * GDN_DK
    v_w = GDN_HEADS * GDN_DV
    return pl.pallas_call(
        _gdn_kernel,
        grid=(b, s // ts),
        in_specs=[
            pl.BlockSpec((None, ts, qk_w), lambda i, j: (i, j, COL_GQ // qk_w)),
            pl.BlockSpec((None, ts, qk_w), lambda i, j: (i, j, COL_GK // qk_w)),
            pl.BlockSpec((None, ts, v_w), lambda i, j: (i, j, COL_GV // v_w)),
            pl.BlockSpec((None, ts, v_w), lambda i, j: (i, j, COL_GZ // v_w)),
            pl.BlockSpec((None, ts, v_w), lambda i, j: (i, j, COL_GATE_B // v_w)),
            pl.BlockSpec((None, ts, LANES), lambda i, j: (i, j, 0)),
            pl.BlockSpec((1, LANES), lambda i, j: (0, 0)),
            pl.BlockSpec((1, LANES), lambda i, j: (0, 0)),
            pl.BlockSpec((CONV_K, qk_w), lambda i, j: (0, 0)),
            pl.BlockSpec((CONV_K, qk_w), lambda i, j: (0, 1)),
            pl.BlockSpec((CONV_K, v_w), lambda i, j: (0, 1)),
            pl.BlockSpec((1, GDN_DV), lambda i, j: (0, 0)),
        ],
        out_specs=pl.BlockSpec((None, ts, v_w), lambda i, j: (i, j, 0)),
        out_shape=jax.ShapeDtypeStruct((b, s, v_w), BF16),
        scratch_shapes=[
            pltpu.VMEM((GDN_HEADS, GDN_DK, GDN_DV), F32),
            pltpu.VMEM((8, 2 * qk_w + v_w), F32),
        ],
        compiler_params=_params("arbitrary", "arbitrary"),
        name="gdn",
    )(proj3, proj3, proj3, proj3, proj3, gates3, alog_row, dtb_row, conv_w, conv_w, conv_w,
      norm_w.reshape(1, GDN_DV))


def _route(logits):
    lane = lax.broadcasted_iota(jnp.int32, logits.shape, 1)
    neg = jnp.float32(-jnp.inf)

    def masked_softmax(mask):
        m = jnp.max(jnp.where(mask, logits, neg), axis=-1, keepdims=True)
        e = jnp.where(mask, jnp.exp(jnp.where(mask, logits, m) - m), 0.0)
        return e / jnp.sum(e, axis=-1, keepdims=True)

    def first_argmax(p, mask):
        top = jnp.max(jnp.where(mask, p, -1.0), axis=-1, keepdims=True)
        idx = jnp.min(jnp.where(jnp.logical_and(mask, p == top), lane, LANES), axis=-1, keepdims=True)
        return top, idx

    gmask = lane < N_GROUPS
    g_top, g_idx = first_argmax(masked_softmax(gmask), gmask)
    lo = N_GROUPS + EXPERTS_PER_GROUP * g_idx
    emask = jnp.logical_and(lane >= lo, lane < lo + EXPERTS_PER_GROUP)
    pe = masked_softmax(emask)
    top1, i1 = first_argmax(pe, emask)
    emask2 = jnp.logical_and(emask, lane != i1)
    top2, i2 = first_argmax(pe, emask2)
    denom = top1 + top2
    comb = jnp.where(lane == i1, g_top * (top1 / denom), jnp.where(lane == i2, g_top * (top2 / denom), 0.0))
    return jnp.where(lane == 0, g_idx.astype(F32), comb)


def _outproj_kernel(ya_ref, yb_ref, x_ref, w_ref, gate_ref, nw_ref, shift_ref, scale_ref, wr_ref, wrh_ref, br_ref,
                    h_ref, xf_ref, comb_ref):
    tm = x_ref.shape[0]
    halves = [slice(k * (tm // 2), (k + 1) * (tm // 2)) for k in range(2)]
    hs, xfs, logits = [], [], []
    for rs in halves:
        merged = (ya_ref[rs, :].astype(F32) + yb_ref[rs, :].astype(F32)).astype(BF16)
        hs.append(x_ref[rs, :] + gate_ref[...] * _dot(merged, w_ref[...]))
    for rs, h in zip(halves, hs):
        h_ref[rs, :] = h
        y = h * lax.rsqrt(jnp.mean(h * h, axis=-1, keepdims=True) + NORM_EPS) * nw_ref[...]
        xfs.append(y * (1.0 + scale_ref[...]) + shift_ref[...])
    for rs, xf in zip(halves, xfs):
        xf_hi = xf.astype(BF16)
        xf_ref[rs, :] = xf_hi
        xf_lo = (xf - xf_hi.astype(F32)).astype(BF16)
        both = _dot(xf_hi, wr_ref[...])
        logits.append(both[:, :LANES] + both[:, LANES:] + _dot(xf_lo, wrh_ref[...]) + br_ref[...])
    for rs, lg in zip(halves, logits):
        comb_ref[rs, :] = _route(lg)


def _outproj(ya2, yb2, x2, w_out_b, mod4, norm_w, w_route, b_route, seq, tm=512):
    t, d = x2.shape
    per_b = seq // tm
    row_spec = pl.BlockSpec((tm, d), lambda i: (i, 0))
    wr_hi = w_route.astype(BF16)
    wr_lo = (w_route - wr_hi.astype(F32)).astype(BF16)

    def mod_spec(k):
        return pl.BlockSpec((None, None, 1, d), lambda i: (i // per_b, k, 0, 0))

    return pl.pallas_call(
        _outproj_kernel,
        grid=(t // tm,),
        in_specs=[
            row_spec, row_spec, row_spec,
            pl.BlockSpec((d, d), lambda i: (0, 0)),
            mod_spec(2),
            pl.BlockSpec((1, d), lambda i: (0, 0)),
            mod_spec(3),
            mod_spec(4),
            pl.BlockSpec((d, 2 * LANES), lambda i: (0, 0)),
            pl.BlockSpec((d, LANES), lambda i: (0, 0)),
            pl.BlockSpec((1, LANES), lambda i: (0, 0)),
        ],
        out_specs=[row_spec, row_spec, pl.BlockSpec((tm, LANES), lambda i: (i, 0))],
        out_shape=[
            jax.ShapeDtypeStruct((t, d), F32),
            jax.ShapeDtypeStruct((t, d), BF16),
            jax.ShapeDtypeStruct((t, LANES), F32),
        ],
        compiler_params=_params("arbitrary"),
        name="outproj",
    )(ya2, yb2, x2, w_out_b, mod4, norm_w.reshape(1, d), mod4, mod4,
      jnp.concatenate([wr_hi, wr_lo], axis=1), wr_hi, b_route)


def _moe_kernel(xf_ref, comb_ref, wg_ref, wu_ref, wd_ref, h_ref, gate_ref, nw_ref, o_ref,
                xs_ref, cs_ref, acc_ref, posc_ref, posr_ref, ltri_ref, meta_ref):
    i = pl.program_id(0)
    p = pl.program_id(1)
    tm = MOE_TILE
    r = MOE_CHUNK
    steps_per_group = EXPERTS_PER_GROUP // MOE_EXPERTS_PER_STEP
    g = p // steps_per_group

    @pl.when(jnp.logical_and(i == 0, p == 0))
    def _():
        row = lax.broadcasted_iota(jnp.int32, (tm, tm), 0)
        col = lax.broadcasted_iota(jnp.int32, (tm, tm), 1)
        ltri_ref[...] = jnp.where(row >= col, 1.0, 0.0).astype(BF16)

    def chunk_rows(c):
        return pl.ds(pl.multiple_of(c * r, MOE_ROW_ALIGN), r)

    @pl.when(p == 0)
    def _():
        comb = comb_ref[...]
        lane = lax.broadcasted_iota(jnp.int32, (tm, LANES), 1)
        gidx = comb[:, 0:1]
        mine = jnp.logical_and(lane.astype(F32) == gidx, lane < N_GROUPS)
        csum = _dot(ltri_ref[...], jnp.where(mine, 1.0, 0.0).astype(BF16))
        counts = jnp.broadcast_to(csum[tm - 1:tm, :], (8, LANES))
        nch = jnp.floor((counts + (r - 0.5)) * (1.0 / r))
        lr = lax.broadcasted_iota(jnp.int32, (LANES, LANES), 0)
        lc = lax.broadcasted_iota(jnp.int32, (LANES, LANES), 1)
        cstart = _dot(nch, jnp.where(lr < lc, 1.0, 0.0), precision=HIGHEST)
        pos = jnp.sum(jnp.where(mine, cstart[0:1, :] * r + csum - 1.0, 0.0), axis=-1, keepdims=True)
        pos_b = jnp.broadcast_to(pos, (tm, LANES))
        posc_ref[...] = pos_b.astype(jnp.int32)
        posr_ref[...] = pos_b.T[0:8, :].astype(jnp.int32)
        lane1 = lax.broadcasted_iota(jnp.int32, (8, LANES), 1)
        for k in range(N_GROUPS):
            meta_ref[k] = jnp.sum(jnp.where(lane1 == k, nch, 0.0)[0:1, :]).astype(jnp.int32)
            meta_ref[N_GROUPS + k] = jnp.sum(jnp.where(lane1 == k, cstart, 0.0)[0:1, :]).astype(jnp.int32)
        meta_ref[2 * N_GROUPS] = jnp.sum(jnp.where(lane1 < N_GROUPS, nch, 0.0)[0:1, :]).astype(jnp.int32)

        c_hi = comb.astype(BF16)
        c_lo = (comb - c_hi.astype(F32)).astype(BF16)
        xf = xf_ref[...]
        pos_row = posr_ref[0:1, :]
        sub = lax.broadcasted_iota(jnp.int32, (r, tm), 0)

        def gather(c, carry):
            p_c = jnp.where(pos_row == sub + c * r, 1.0, 0.0).astype(BF16)
            rows = chunk_rows(c)
            xs_ref[rows, :] = _dot(p_c, xf).astype(BF16)
            cs_ref[rows, :] = _dot(p_c, c_hi) + _dot(p_c, c_lo)
            return carry

        lax.fori_loop(0, meta_ref[2 * N_GROUPS], gather, 0)

    lane_r = lax.broadcasted_iota(jnp.int32, (r, LANES), 1)
    first_of_group = p % steps_per_group == 0

    def expert_chunk(c, carry):
        rows = chunk_rows(c)
        xs = xs_ref[rows, :]
        cs = cs_ref[rows, :]
        contrib = None
        for j in range(MOE_EXPERTS_PER_STEP):
            e = p * MOE_EXPERTS_PER_STEP + j
            wcol = jnp.sum(jnp.where(lane_r == N_GROUPS + e, cs, 0.0), axis=-1, keepdims=True)
            for f0 in range(0, D_FF_EXPERT, MOE_FF_SLAB):
                fs = slice(f0, f0 + MOE_FF_SLAB)
                act = _silu(_dot(xs, wg_ref[j, :, fs])) * _dot(xs, wu_ref[j, :, fs]) * wcol
                part = _dot(act.astype(BF16), wd_ref[j, fs, :])
                contrib = part if contrib is None else contrib + part

        @pl.when(first_of_group)
        def _():
            acc_ref[rows, :] = contrib

        @pl.when(jnp.logical_not(first_of_group))
        def _():
            acc_ref[rows, :] += contrib

        return carry

    c0 = meta_ref[N_GROUPS + g]
    lax.fori_loop(c0, c0 + meta_ref[g], expert_chunk, 0)

    @pl.when(p == pl.num_programs(1) - 1)
    def _():
        blk = MOE_SCATTER_BLOCK
        total = meta_ref[2 * N_GROUPS]
        acc_ref[chunk_rows(total), :] = jnp.zeros((r, acc_ref.shape[1]), F32)
        pos_col = posc_ref[:, 0:1]
        lane_c = lax.broadcasted_iota(jnp.int32, (tm, blk), 1)
        o_ref[...] = jnp.zeros_like(o_ref)

        def scatter(b, carry):
            pt_b = jnp.where(pos_col == lane_c + b * blk, 1.0, 0.0).astype(BF16)
            rows = pl.ds(pl.multiple_of(b * blk, blk), blk)
            o_ref[...] += _dot(pt_b, acc_ref[rows, :].astype(BF16))
            return carry

        lax.fori_loop(0, (total * r + (blk - 1)) // blk, scatter, 0)
        h = h_ref[...] + gate_ref[...] * o_ref[...]
        o_ref[...] = h * lax.rsqrt(jnp.mean(h * h, axis=-1, keepdims=True) + NORM_EPS) * nw_ref[...]


def _moe(xf2, comb, wg, wu, wd, h2, mod4, norm_out_w, seq):
    t, d = xf2.shape
    f = wg.shape[-1]
    tm = MOE_TILE
    per_b = seq // tm
    eps = MOE_EXPERTS_PER_STEP
    row_spec = pl.BlockSpec((tm, d), lambda i, p: (i, 0))
    return pl.pallas_call(
        _moe_kernel,
        grid=(t // tm, N_EXPERTS // eps),
        in_specs=[
            row_spec,
            pl.BlockSpec((tm, LANES), lambda i, p: (i, 0)),
            pl.BlockSpec((eps, d, f), lambda i, p: (p, 0, 0)),
            pl.BlockSpec((eps, d, f), lambda i, p: (p, 0, 0)),
            pl.BlockSpec((eps, f, d), lambda i, p: (p, 0, 0)),
            row_spec,
            pl.BlockSpec((None, None, 1, d), lambda i, p: (i // per_b, 5, 0, 0)),
            pl.BlockSpec((1, d), lambda i, p: (0, 0)),
        ],
        out_specs=row_spec,
        out_shape=jax.ShapeDtypeStruct((t, d), F32),
        scratch_shapes=[
            pltpu.VMEM((MOE_SORTED, d), BF16),
            pltpu.VMEM((MOE_SORTED, LANES), F32),
            pltpu.VMEM((MOE_SORTED + MOE_CHUNK, d), F32),
            pltpu.VMEM((tm, LANES), jnp.int32),
            pltpu.VMEM((8, tm), jnp.int32),
            pltpu.VMEM((tm, tm), BF16),
            pltpu.SMEM((2 * N_GROUPS + 1,), jnp.int32),
        ],
        compiler_params=pltpu.CompilerParams(dimension_semantics=("arbitrary", "arbitrary"),
                                             vmem_limit_bytes=MOE_VMEM_LIMIT),
        name="moe",
    )(xf2, comb, wg, wu, wd, h2, mod4, norm_out_w.reshape(1, d))


def _pad_lanes(a):
    return jnp.pad(a, ((0, 0), (0, LANES - a.shape[1])))


def _layer(h3, c, mod_w, mod_b, norm_mix_w, w_in, conv_w, a_log, dt_bias, gdn_norm_w, w_out, norm_ffn_w,
           w_group, b_group, w_router, b_router, w_gate, w_up, w_down, norm_out_w):
    b, s, d = h3.shape
    t = b * s
    x2 = h3.reshape(t, d)

    n_gate_cols = 2 * GDN_HEADS
    small_lo = COL_GATE_A
    w_main = jnp.concatenate([w_in[:, :small_lo], w_in[:, small_lo + n_gate_cols:]], axis=1).astype(BF16)
    w_small = _pad_lanes(w_in[:, small_lo:small_lo + n_gate_cols])
    w_route = _pad_lanes(jnp.concatenate([w_group, w_router], axis=1))
    b_route = _pad_lanes(jnp.concatenate([b_group, b_router.reshape(-1)])[None, :])
    alog_row = _pad_lanes(a_log[None, :])
    dtb_row = _pad_lanes(dt_bias[None, :])
    f = w_gate.shape[-1]
    wg = w_gate.reshape(N_EXPERTS, d, f).astype(BF16)
    wu = w_up.reshape(N_EXPERTS, d, f).astype(BF16)
    wd = w_down.reshape(N_EXPERTS, f, d).astype(BF16)

    half = RET_DK // 2
    inv_freq = 1.0 / (ROPE_BASE ** (jnp.arange(half, dtype=F32) / half))
    ang = jnp.arange(s, dtype=F32)[:, None] * inv_freq[None, :]
    cos_t = jnp.concatenate([jnp.cos(ang), jnp.cos(ang)], axis=1)
    sin_t = jnp.concatenate([-jnp.sin(ang), jnp.sin(ang)], axis=1)
    log_gamma = jnp.log(1.0 - 2.0 ** (-5.0 - jnp.arange(RET_HEADS, dtype=F32)))
    log_gamma = jnp.broadcast_to(log_gamma[:, None], (RET_HEADS, RET_DV))

    mod4 = _mod(c, mod_w, mod_b).reshape(b, N_MOD, 1, d)
    proj, gates = _inproj(x2, mod4, norm_mix_w, w_main, w_small, s)
    proj3 = proj.reshape(b, s, N_MAIN)
    ya = _retention(proj3, cos_t, sin_t, log_gamma)
    yb = _gdn(proj3, gates.reshape(b, s, LANES), alog_row, dtb_row, conv_w, gdn_norm_w)
    h2, xf2, comb = _outproj(ya.reshape(t, d), yb.reshape(t, d), x2, w_out.astype(BF16), mod4, norm_ffn_w,
                             w_route, b_route, s)
    return _moe(xf2, comb, wg, wu, wd, h2, mod4, norm_out_w, s).reshape(b, s, d)


def kernel(x, c, mod_w, mod_b, norm_mix_w, w_in, gdn_conv_w, gdn_a_log, gdn_dt_bias, gdn_norm_w, w_out, norm_ffn_w,
           w_group, b_group, w_router, b_router, w_gate, w_up, w_down, norm_out_w):
    assert mod_w.shape[0] == 1, "one residual layer"
    return _layer(x, c, mod_w[0], mod_b[0], norm_mix_w[0], w_in[0], gdn_conv_w[0], gdn_a_log[0], gdn_dt_bias[0],
                  gdn_norm_w[0], w_out[0], norm_ffn_w[0], w_group[0], b_group[0], w_router[0], b_router[0],
                  w_gate[0], w_up[0], w_down[0], norm_out_w)
```

```python
import functools

import jax
import jax.numpy as jnp
from jax import lax
from jax.experimental import pallas as pl
from jax.experimental.pallas import tpu as pltpu

F32 = jnp.float32
BF16 = jnp.bfloat16
HIGHEST = lax.Precision.HIGHEST

RET_HEADS = 4
RET_DK = 128
RET_DV = 256
GDN_HEADS = 4
GDN_DK = 128
GDN_DV = 256
GDN_CHUNK = 64
CONV_K = 4
N_GROUPS = 4
EXPERTS_PER_GROUP = 4
N_EXPERTS = N_GROUPS * EXPERTS_PER_GROUP
D_FF_EXPERT = 512
ROPE_BASE = 10000.0
NORM_EPS = 1e-6
L2_EPS = 1e-6
N_MOD = 6
LANES = 128
SEQ_TILE = 256
VMEM_LIMIT = 48 * 1024 * 1024
MOE_TILE = 1024
MOE_ROW_ALIGN = 16
MOE_SORTED = MOE_TILE + N_GROUPS * MOE_ROW_ALIGN
MOE_PIECE = 512
MOE_BUCKET = 64
MOE_MIN_PIECE = 128
MOE_ROWS = MOE_SORTED + MOE_PIECE
MOE_EXPERTS_PER_STEP = 2
MOE_FF_SLAB = 256
MOE_VMEM_LIMIT = 56 * 1024 * 1024

COL_RQ, COL_RK, COL_RV, COL_RG = 0, 512, 1024, 2048
COL_GQ, COL_GK, COL_GV, COL_GZ = 3072, 3584, 4096, 5120
COL_GATE_A, COL_GATE_B = 6144, 7168
N_MAIN = 8192


def _silu(x):
    return x * jax.nn.sigmoid(x)


def _dot(a, b, **kw):
    return jnp.dot(a, b, preferred_element_type=F32, **kw)


def _dot_nt(a, b, **kw):
    return lax.dot_general(a, b, (((1,), (1,)), ((), ())), preferred_element_type=F32, **kw)


def _dot_tn(a, b, **kw):
    return lax.dot_general(a, b, (((0,), (0,)), ((), ())), preferred_element_type=F32, **kw)


def _params(*sem):
    return pltpu.CompilerParams(dimension_semantics=sem, vmem_limit_bytes=VMEM_LIMIT)


def _mod_kernel(c_ref, w_ref, b_ref, o_ref):
    a = _silu(c_ref[...])
    o_ref[...] = _dot(a, w_ref[...], precision=HIGHEST) + b_ref[...]


def _mod(c, mod_w, mod_b):
    b, d = c.shape
    n = mod_w.shape[1]
    tn = d
    return pl.pallas_call(
        _mod_kernel,
        grid=(n // tn,),
        in_specs=[
            pl.BlockSpec((b, d), lambda j: (0, 0)),
            pl.BlockSpec((d, tn), lambda j: (0, j)),
            pl.BlockSpec((1, tn), lambda j: (0, j)),
        ],
        out_specs=pl.BlockSpec((b, tn), lambda j: (0, j)),
        out_shape=jax.ShapeDtypeStruct((b, n), F32),
        compiler_params=_params("arbitrary"),
        name="mod",
    )(c, mod_w, mod_b.reshape(1, n))


def _inproj_kernel(x0_ref, shift0_ref, scale0_ref, xn_ref, shiftn_ref, scalen_ref, nw_ref, w_ref, ws_ref,
                   o_ref, og_ref, xb_even_ref, xb_odd_ref):
    i = pl.program_id(0)
    j = pl.program_id(1)
    tm = x0_ref.shape[0]
    slab = tm // pl.num_programs(1)

    def prepare(x, shift, scale):
        y = x * lax.rsqrt(jnp.mean(x * x, axis=-1, keepdims=True) + NORM_EPS) * nw_ref[...]
        return (y * (1.0 + scale) + shift).astype(BF16)

    @pl.when(jnp.logical_and(i == 0, j == 0))
    def _():
        xb_even_ref[...] = prepare(x0_ref[...], shift0_ref[...], scale0_ref[...])

    def step(cur_ref, nxt_ref):
        rows = pl.ds(pl.multiple_of(j * slab, slab), slab)
        nxt_ref[rows, :] = prepare(xn_ref[rows, :], shiftn_ref[...], scalen_ref[...])
        xb = cur_ref[...]
        o_ref[...] = _dot(xb, w_ref[...]).astype(BF16)

        @pl.when(j == 0)
        def _():
            og_ref[...] = _dot(xb, ws_ref[...])

    @pl.when(i % 2 == 0)
    def _():
        step(xb_even_ref, xb_odd_ref)

    @pl.when(i % 2 == 1)
    def _():
        step(xb_odd_ref, xb_even_ref)


def _inproj(x2, mod4, norm_w, w_main, w_small, seq, tm=1024, tn=2048):
    t, d = x2.shape
    n = w_main.shape[1]
    per_b = seq // tm
    last = t // tm - 1

    def nxt(i):
        return jnp.minimum(i + 1, last)

    return pl.pallas_call(
        _inproj_kernel,
        grid=(t // tm, n // tn),
        in_specs=[
            pl.BlockSpec((tm, d), lambda i, j: (0, 0)),
            pl.BlockSpec((None, None, 1, d), lambda i, j: (0, 0, 0, 0)),
            pl.BlockSpec((None, None, 1, d), lambda i, j: (0, 1, 0, 0)),
            pl.BlockSpec((tm, d), lambda i, j: (nxt(i), 0)),
            pl.BlockSpec((None, None, 1, d), lambda i, j: (nxt(i) // per_b, 0, 0, 0)),
            pl.BlockSpec((None, None, 1, d), lambda i, j: (nxt(i) // per_b, 1, 0, 0)),
            pl.BlockSpec((1, d), lambda i, j: (0, 0)),
            pl.BlockSpec((d, tn), lambda i, j: (0, j)),
            pl.BlockSpec((d, LANES), lambda i, j: (0, 0)),
        ],
        out_specs=[
            pl.BlockSpec((tm, tn), lambda i, j: (i, j)),
            pl.BlockSpec((tm, LANES), lambda i, j: (i, 0)),
        ],
        out_shape=[
            jax.ShapeDtypeStruct((t, n), BF16),
            jax.ShapeDtypeStruct((t, LANES), F32),
        ],
        scratch_shapes=[pltpu.VMEM((tm, d), BF16), pltpu.VMEM((tm, d), BF16)],
        compiler_params=_params("arbitrary", "arbitrary"),
        name="inproj",
    )(x2, mod4, mod4, x2, mod4, mod4, norm_w.reshape(1, d), w_main, w_small.astype(BF16))


def _ret_kernel(lg_ref, q_ref, k_ref, v_ref, rg_ref, ga_ref, cos_ref, sin_ref, o_ref,
                state_ref, intra_ref, qd_ref, kd_ref):
    c = SEQ_TILE
    first = jnp.logical_and(pl.program_id(0) == 0, pl.program_id(1) == 0)

    @pl.when(first)
    def _():
        row = lax.broadcasted_iota(jnp.int32, (c, c), 0)
        col = lax.broadcasted_iota(jnp.int32, (c, c), 1)
        rel = (row - col).astype(F32)
        causal = row >= col
        pos = lax.broadcasted_iota(jnp.int32, (c, RET_DK), 0).astype(F32)
        for h in range(RET_HEADS):
            lg = lg_ref[h:h + 1, :]
            intra_ref[h] = jnp.where(causal, jnp.exp(jnp.where(causal, rel, 0.0) * lg), 0.0)
            qd_ref[h] = jnp.exp((pos + 1.0) * lg[:, :RET_DK])
            kd_ref[h] = jnp.exp((c - 1.0 - pos) * lg[:, :RET_DK])

    @pl.when(pl.program_id(1) == 0)
    def _():
        state_ref[...] = jnp.zeros_like(state_ref)

    cos = cos_ref[...]
    sin = sin_ref[...]
    for h in range(RET_HEADS):
        qs = slice(h * RET_DK, (h + 1) * RET_DK)
        vs = slice(h * RET_DV, (h + 1) * RET_DV)
        qr = q_ref[:, qs].astype(F32)
        kr = k_ref[:, qs].astype(F32)
        q = qr * cos + pltpu.roll(qr, RET_DK // 2, 1) * sin
        k = (kr * cos + pltpu.roll(kr, RET_DK // 2, 1) * sin) * (RET_DK ** -0.5)
        v = v_ref[:, vs]
        state = state_ref[h]
        chunk_decay = jnp.exp(float(c) * lg_ref[h:h + 1, :])
        scores = _dot_nt(q.astype(BF16), k.astype(BF16)) * intra_ref[h]
        o = _dot(scores.astype(BF16), v) + _dot((q * qd_ref[h]).astype(BF16), state.astype(BF16))
        state_ref[h] = state * chunk_decay + _dot_tn((k * kd_ref[h]).astype(BF16), v)
        o = o * lax.rsqrt(jnp.mean(o * o, axis=-1, keepdims=True) + NORM_EPS)
        y = _silu(rg_ref[:, vs].astype(F32)) * o
        o_ref[:, vs] = (jax.nn.sigmoid(ga_ref[:, vs].astype(F32)) * y).astype(BF16)


def _retention(proj3, cos_t, sin_t, log_gamma):
    b, s, _ = proj3.shape
    ts = SEQ_TILE
    qk_w = RET_HEADS * RET_DK
    v_w = RET_HEADS * RET_DV
    return pl.pallas_call(
        _ret_kernel,
        grid=(b, s // ts),
        in_specs=[
            pl.BlockSpec((RET_HEADS, RET_DV), lambda i, j: (0, 0)),
            pl.BlockSpec((None, ts, qk_w), lambda i, j: (i, j, COL_RQ // qk_w)),
            pl.BlockSpec((None, ts, qk_w), lambda i, j: (i, j, COL_RK // qk_w)),
            pl.BlockSpec((None, ts, v_w), lambda i, j: (i, j, COL_RV // v_w)),
            pl.BlockSpec((None, ts, v_w), lambda i, j: (i, j, COL_RG // v_w)),
            pl.BlockSpec((None, ts, v_w), lambda i, j: (i, j, COL_GATE_A // v_w)),
            pl.BlockSpec((ts, RET_DK), lambda i, j: (j, 0)),
            pl.BlockSpec((ts, RET_DK), lambda i, j: (j, 0)),
        ],
        out_specs=pl.BlockSpec((None, ts, v_w), lambda i, j: (i, j, 0)),
        out_shape=jax.ShapeDtypeStruct((b, s, v_w), BF16),
        scratch_shapes=[
            pltpu.VMEM((RET_HEADS, RET_DK, RET_DV), F32),
            pltpu.VMEM((RET_HEADS, ts, ts), F32),
            pltpu.VMEM((RET_HEADS, ts, RET_DK), F32),
            pltpu.VMEM((RET_HEADS, ts, RET_DK), F32),
        ],
        compiler_params=_params("arbitrary", "arbitrary"),
        name="retention",
    )(log_gamma, proj3, proj3, proj3, proj3, proj3, cos_t, sin_t)


def _unit_lower_inverse(a, eye, nilpotency):
    p = eye - a
    m = a
    for _ in range((nilpotency - 1).bit_length() - 1):
        m16 = m.astype(BF16)
        m = _dot(m16, m16)
        p = p + _dot(p.astype(BF16), m.astype(BF16))
    return p


def _gdn_kernel(q_ref, k_ref, v_ref, z_ref, gb_ref, gates_ref, alog_ref, dtb_ref, cwq_ref, cwk_ref, cwv_ref,
                nw_ref, o_ref, state_ref, tail_ref):
    ts = SEQ_TILE
    cc = GDN_CHUNK
    qk_w = GDN_HEADS * GDN_DK

    @pl.when(pl.program_id(1) == 0)
    def _():
        state_ref[...] = jnp.zeros_like(state_ref)
        tail_ref[...] = jnp.zeros_like(tail_ref)

    def conv_silu(cur, tail, cw_ref):
        ext = jnp.concatenate([tail, cur], axis=0)
        out = ext[8 - (CONV_K - 1):8 - (CONV_K - 1) + ts] * cw_ref[0:1, :]
        for i in range(1, CONV_K):
            d = CONV_K - 1 - i
            out = out + ext[8 - d:8 - d + ts] * cw_ref[i:i + 1, :]
        return _silu(out)

    q_raw = q_ref[...].astype(F32)
    k_raw = k_ref[...].astype(F32)
    v_raw = v_ref[...].astype(F32)
    q_all = conv_silu(q_raw, tail_ref[:, 0:qk_w], cwq_ref)
    k_all = conv_silu(k_raw, tail_ref[:, qk_w:2 * qk_w], cwk_ref)
    v_all = conv_silu(v_raw, tail_ref[:, 2 * qk_w:], cwv_ref)
    tail_ref[:, 0:qk_w] = q_raw[ts - 8:]
    tail_ref[:, qk_w:2 * qk_w] = k_raw[ts - 8:]
    tail_ref[:, 2 * qk_w:] = v_raw[ts - 8:]

    gates = gates_ref[...]
    x = gates + dtb_ref[...]
    softplus = jnp.maximum(x, 0.0) + jnp.log1p(jnp.exp(-jnp.abs(x)))
    g_all = -jnp.exp(alog_ref[...]) * softplus
    beta_all = jax.nn.sigmoid(gates)
    row = lax.broadcasted_iota(jnp.int32, (ts, ts), 0)
    col = lax.broadcasted_iota(jnp.int32, (ts, ts), 1)
    same_chunk = (row // cc) == (col // cc)
    causal = jnp.logical_and(same_chunk, row >= col)
    strict = jnp.logical_and(same_chunk, row > col)
    eye = jnp.where(row == col, 1.0, 0.0)
    tri_lower = jnp.where(causal, 1.0, 0.0)
    tri_upper = jnp.where(jnp.logical_and(same_chunk, row <= col), 1.0, 0.0)
    gc_col_all = _dot(tri_lower, g_all, precision=HIGHEST)
    gc_row_all = _dot_tn(g_all, tri_upper, precision=HIGHEST)

    heads = range(GDN_HEADS)
    qn, kn, k16, k_beta, gc, decay, a_mat = [], [], [], [], [], [], []
    for h in heads:
        qs = slice(h * GDN_DK, (h + 1) * GDN_DK)
        qh = q_all[:, qs]
        kh = k_all[:, qs]
        qn.append(qh * lax.rsqrt(jnp.sum(qh * qh, axis=-1, keepdims=True) + L2_EPS) * (GDN_DK ** -0.5))
        kn.append(kh * lax.rsqrt(jnp.sum(kh * kh, axis=-1, keepdims=True) + L2_EPS))
        gc.append(gc_col_all[:, h:h + 1])
        gcr = gc_row_all[h:h + 1, :]
        decay.append(jnp.exp(jnp.where(causal, gc[h] - gcr, -jnp.inf)))
        k_beta.append(kn[h] * beta_all[:, GDN_HEADS + h:GDN_HEADS + h + 1])
        k16.append(kn[h].astype(BF16))
    for h in heads:
        a_mat.append(jnp.where(strict, _dot_nt(k_beta[h].astype(BF16), k16[h]) * decay[h], 0.0))
    p = [eye - a_mat[h] for h in heads]
    m = a_mat
    for _ in range((cc - 1).bit_length() - 1):
        m16 = [m[h].astype(BF16) for h in heads]
        m = [_dot(m16[h], m16[h]) for h in heads]
        p = [p[h] + _dot(p[h].astype(BF16), m[h].astype(BF16)) for h in heads]
    t16 = [p[h].astype(BF16) for h in heads]
    egc = [jnp.exp(gc[h]) for h in heads]
    u_all, w_all, qk_all, qg_all = [], [], [], []
    for h in heads:
        vs = slice(h * GDN_DV, (h + 1) * GDN_DV)
        beta = beta_all[:, GDN_HEADS + h:GDN_HEADS + h + 1]
        u_all.append(_dot(t16[h], (v_all[:, vs] * beta).astype(BF16)))
        w_all.append(_dot(t16[h], (k_beta[h] * egc[h]).astype(BF16)).astype(BF16))
        qk_all.append((_dot_nt(qn[h].astype(BF16), k16[h]) * decay[h]).astype(BF16))
        qg_all.append((qn[h] * egc[h]).astype(BF16))
    state = [state_ref[h] for h in heads]
    outs = [[] for _ in heads]
    for n in range(ts // cc):
        rs = slice(n * cc, (n + 1) * cc)
        for h in heads:
            state_b = state[h].astype(BF16)
            v_new = (u_all[h][rs] - _dot(w_all[h][rs], state_b)).astype(BF16)
            outs[h].append(_dot(qg_all[h][rs], state_b) + _dot(qk_all[h][rs, rs], v_new))
            g_last = gc[h][(n + 1) * cc - 1:(n + 1) * cc, :]
            k_g = kn[h][rs] * jnp.exp(g_last - gc[h][rs])
            state[h] = state[h] * jnp.exp(g_last) + _dot_tn(k_g.astype(BF16), v_new)
    for h in heads:
        vs = slice(h * GDN_DV, (h + 1) * GDN_DV)
        state_ref[h] = state[h]
        o = jnp.concatenate(outs[h], axis=0)
        o = o * lax.rsqrt(jnp.mean(o * o, axis=-1, keepdims=True) + NORM_EPS) * nw_ref[...]
        o = o * _silu(z_ref[:, vs].astype(F32))
        o_ref[:, vs] = (jax.nn.sigmoid(gb_ref[:, vs].astype(F32)) * o).astype(BF16)


def _gdn(proj3, gates3, alog_row, dtb_row, conv_w, norm_w):
    b, s, _ = proj3.shape
    ts = SEQ_TILE
    qk_w = GDN_HEADS * GDN_DK
    v_w = GDN_HEADS * GDN_DV
    return pl.pallas_call(
        _gdn_kernel,
        grid=(b, s // ts),
        in_specs=[
            pl.BlockSpec((None, ts, qk_w), lambda i, j: (i, j, COL_GQ // qk_w)),
            pl.BlockSpec((None, ts, qk_w), lambda i, j: (i, j, COL_GK // qk_w)),
            pl.BlockSpec((None, ts, v_w), lambda i, j: (i, j, COL_GV // v_w)),
            pl.BlockSpec((None, ts, v_w), lambda i, j: (i, j, COL_GZ // v_w)),
            pl.BlockSpec((None, ts, v_w), lambda i, j: (i, j, COL_GATE_B // v_w)),
            pl.BlockSpec((None, ts, LANES), lambda i, j: (i, j, 0)),
            pl.BlockSpec((1, LANES), lambda i, j: (0, 0)),
            pl.BlockSpec((1, LANES), lambda i, j: (0, 0)),
            pl.BlockSpec((CONV_K, qk_w), lambda i, j: (0, 0)),
            pl.BlockSpec((CONV_K, qk_w), lambda i, j: (0, 1)),
            pl.BlockSpec((CONV_K, v_w), lambda i, j: (0, 1)),
            pl.BlockSpec((1, GDN_DV), lambda i, j: (0, 0)),
        ],
        out_specs=pl.BlockSpec((None, ts, v_w), lambda i, j: (i, j, 0)),
        out_shape=jax.ShapeDtypeStruct((b, s, v_w), BF16),
        scratch_shapes=[
            pltpu.VMEM((GDN_HEADS, GDN_DK, GDN_DV), F32),
            pltpu.VMEM((8, 2 * qk_w + v_w), F32),
        ],
        compiler_params=_params("arbitrary", "arbitrary"),
        name="gdn",
    )(proj3, proj3, proj3, proj3, proj3, gates3, alog_row, dtb_row, conv_w, conv_w, conv_w,
      norm_w.reshape(1, GDN_DV))


def _route(logits):
    lane = lax.broadcasted_iota(jnp.int32, logits.shape, 1)
    neg = jnp.float32(-jnp.inf)

    def masked_softmax(mask):
        m = jnp.max(jnp.where(mask, logits, neg), axis=-1, keepdims=True)
        e = jnp.where(mask, jnp.exp(jnp.where(mask, logits, m) - m), 0.0)
        return e / jnp.sum(e, axis=-1, keepdims=True)

    def first_argmax(p, mask):
        top = jnp.max(jnp.where(mask, p, -1.0), axis=-1, keepdims=True)
        idx = jnp.min(jnp.where(jnp.logical_and(mask, p == top), lane, LANES), axis=-1, keepdims=True)
        return top, idx

    gmask = lane < N_GROUPS
    g_top, g_idx = first_argmax(masked_softmax(gmask), gmask)
    lo = N_GROUPS + EXPERTS_PER_GROUP * g_idx
    emask = jnp.logical_and(lane >= lo, lane < lo + EXPERTS_PER_GROUP)
    pe = masked_softmax(emask)
    top1, i1 = first_argmax(pe, emask)
    emask2 = jnp.logical_and(emask, lane != i1)
    top2, i2 = first_argmax(pe, emask2)
    denom = top1 + top2
    comb = jnp.where(lane == i1, g_top * (top1 / denom), jnp.where(lane == i2, g_top * (top2 / denom), 0.0))
    return jnp.where(lane == 0, g_idx.astype(F32), comb)


def _outproj_kernel(ya_ref, yb_ref, x_ref, w_ref, gate_ref, nw_ref, shift_ref, scale_ref, wr_ref, wrh_ref, br_ref,
                    h_ref, xf_ref, comb_ref):
    tm = x_ref.shape[0]
    halves = [slice(k * (tm // 2), (k + 1) * (tm // 2)) for k in range(2)]
    hs, xfs, logits = [], [], []
    for rs in halves:
        merged = (ya_ref[rs, :].astype(F32) + yb_ref[rs, :].astype(F32)).astype(BF16)
        hs.append(x_ref[rs, :] + gate_ref[...] * _dot(merged, w_ref[...]))
    for rs, h in zip(halves, hs):
        h_ref[rs, :] = h
        y = h * lax.rsqrt(jnp.mean(h * h, axis=-1, keepdims=True) + NORM_EPS) * nw_ref[...]
        xfs.append(y * (1.0 + scale_ref[...]) + shift_ref[...])
    for rs, xf in zip(halves, xfs):
        xf_hi = xf.astype(BF16)
        xf_ref[rs, :] = xf_hi
        xf_lo = (xf - xf_hi.astype(F32)).astype(BF16)
        both = _dot(xf_hi, wr_ref[...])
        logits.append(both[:, :LANES] + both[:, LANES:] + _dot(xf_lo, wrh_ref[...]) + br_ref[...])
    for rs, lg in zip(halves, logits):
        comb_ref[rs, :] = _route(lg)


def _outproj(ya2, yb2, x2, w_out_b, mod4, norm_w, w_route, b_route, seq, tm=512):
    t, d = x2.shape
    per_b = seq // tm
    row_spec = pl.BlockSpec((tm, d), lambda i: (i, 0))
    wr_hi = w_route.astype(BF16)
    wr_lo = (w_route - wr_hi.astype(F32)).astype(BF16)

    def mod_spec(k):
        return pl.BlockSpec((None, None, 1, d), lambda i: (i // per_b, k, 0, 0))

    return pl.pallas_call(
        _outproj_kernel,
        grid=(t // tm,),
        in_specs=[
            row_spec, row_spec, row_spec,
            pl.BlockSpec((d, d), lambda i: (0, 0)),
            mod_spec(2),
            pl.BlockSpec((1, d), lambda i: (0, 0)),
            mod_spec(3),
            mod_spec(4),
            pl.BlockSpec((d, 2 * LANES), lambda i: (0, 0)),
            pl.BlockSpec((d, LANES), lambda i: (0, 0)),
            pl.BlockSpec((1, LANES), lambda i: (0, 0)),
        ],
        out_specs=[row_spec, row_spec, pl.BlockSpec((tm, LANES), lambda i: (i, 0))],
        out_shape=[
            jax.ShapeDtypeStruct((t, d), F32),
            jax.ShapeDtypeStruct((t, d), BF16),
            jax.ShapeDtypeStruct((t, LANES), F32),
        ],
        compiler_params=_params("arbitrary"),
        name="outproj",
    )(ya2, yb2, x2, w_out_b, mod4, norm_w.reshape(1, d), mod4, mod4,
      jnp.concatenate([wr_hi, wr_lo], axis=1), wr_hi, b_route)


def _moe_kernel(xf_ref, comb_ref, wg_ref, wu_ref, wd_ref, h_ref, gate_ref, nw_ref, o_ref,
                xs_ref, cs_ref, acc_ref, pos_ref, ltri_ref, meta_ref):
    i = pl.program_id(0)
    p = pl.program_id(1)
    tm = MOE_TILE
    steps_per_group = EXPERTS_PER_GROUP // MOE_EXPERTS_PER_STEP
    g = p // steps_per_group

    @pl.when(jnp.logical_and(i == 0, p == 0))
    def _():
        row = lax.broadcasted_iota(jnp.int32, (tm, tm), 0)
        col = lax.broadcasted_iota(jnp.int32, (tm, tm), 1)
        ltri_ref[...] = jnp.where(row >= col, 1.0, 0.0).astype(BF16)
        xs_ref[MOE_SORTED:, :] = jnp.zeros((MOE_ROWS - MOE_SORTED, xs_ref.shape[1]), BF16)
        cs_ref[MOE_SORTED:, :] = jnp.zeros((MOE_ROWS - MOE_SORTED, LANES), F32)
        acc_ref[MOE_SORTED:, :] = jnp.zeros((MOE_ROWS - MOE_SORTED, acc_ref.shape[1]), F32)

    @pl.when(p == 0)
    def _():
        comb = comb_ref[...]
        lane = lax.broadcasted_iota(jnp.int32, (tm, LANES), 1)
        gidx = comb[:, 0:1]
        mine = jnp.logical_and(lane.astype(F32) == gidx, lane < N_GROUPS)
        csum = _dot(ltri_ref[...], jnp.where(mine, 1.0, 0.0).astype(BF16))
        counts = jnp.broadcast_to(csum[tm - 1:tm, :], (8, LANES))
        aligned = jnp.floor((counts + (MOE_ROW_ALIGN - 0.5)) * (1.0 / MOE_ROW_ALIGN)) * MOE_ROW_ALIGN
        lr = lax.broadcasted_iota(jnp.int32, (LANES, LANES), 0)
        lc = lax.broadcasted_iota(jnp.int32, (LANES, LANES), 1)
        seg_start = _dot(aligned, jnp.where(lr < lc, 1.0, 0.0), precision=HIGHEST)
        pos = jnp.sum(jnp.where(mine, seg_start[0:1, :] + csum - 1.0, 0.0), axis=-1, keepdims=True)
        pos_b = jnp.broadcast_to(pos, (tm, LANES))
        pos_ref[...] = pos_b.astype(jnp.int32)
        pos_row = pos_b.T[0:1, :].astype(jnp.int32)
        lane1 = lax.broadcasted_iota(jnp.int32, (8, LANES), 1)
        for k in range(N_GROUPS):
            meta_ref[k] = jnp.sum(jnp.where(lane1 == k, counts, 0.0)[0:1, :]).astype(jnp.int32)
            meta_ref[N_GROUPS + k] = jnp.sum(jnp.where(lane1 == k, seg_start, 0.0)[0:1, :]).astype(jnp.int32)

        c_hi = comb.astype(BF16)
        c_lo = (comb - c_hi.astype(F32)).astype(BF16)
        sub = lax.broadcasted_iota(jnp.int32, (MOE_SORTED, tm), 0)
        perm = jnp.where(pos_row == sub, 1.0, 0.0).astype(BF16)
        xs_ref[0:MOE_SORTED, :] = _dot(perm, xf_ref[...]).astype(BF16)
        cs_ref[0:MOE_SORTED, :] = _dot(perm, c_hi) + _dot(perm, c_lo)
        acc_ref[0:MOE_SORTED, :] = jnp.zeros((MOE_SORTED, acc_ref.shape[1]), F32)

    def expert_piece(start, m):
        rows = pl.ds(pl.multiple_of(start, MOE_ROW_ALIGN), m)
        xs = xs_ref[rows, :]
        cs = cs_ref[rows, :]
        lane_m = lax.broadcasted_iota(jnp.int32, (m, LANES), 1)
        contrib = None
        for j in range(MOE_EXPERTS_PER_STEP):
            e = p * MOE_EXPERTS_PER_STEP + j
            wcol = jnp.sum(jnp.where(lane_m == N_GROUPS + e, cs, 0.0), axis=-1, keepdims=True)
            for f0 in range(0, D_FF_EXPERT, MOE_FF_SLAB):
                fs = slice(f0, f0 + MOE_FF_SLAB)
                act = _silu(_dot(xs, wg_ref[j, :, fs])) * _dot(xs, wu_ref[j, :, fs]) * wcol
                part = _dot(act.astype(BF16), wd_ref[j, fs, :])
                contrib = part if contrib is None else contrib + part
        acc_ref[rows, :] += contrib

    count = meta_ref[g]
    seg = meta_ref[N_GROUPS + g]
    n_full = jnp.maximum(count - 1, 0) // MOE_PIECE

    def full_piece(k, carry):
        expert_piece(seg + k * MOE_PIECE, MOE_PIECE)
        return carry

    lax.fori_loop(0, n_full, full_piece, 0)
    last_start = seg + n_full * MOE_PIECE
    last_rows = count - n_full * MOE_PIECE
    for m in range(MOE_MIN_PIECE, MOE_PIECE + 1, MOE_BUCKET):
        lo = 0 if m == MOE_MIN_PIECE else m - MOE_BUCKET

        @pl.when(jnp.logical_and(last_rows > lo, last_rows <= m))
        def _(m=m):
            expert_piece(last_start, m)

    @pl.when(p == pl.num_programs(1) - 1)
    def _():
        pos = pos_ref[:, 0:1]
        lane_s = lax.broadcasted_iota(jnp.int32, (tm, MOE_SORTED), 1)
        inv = jnp.where(pos == lane_s, 1.0, 0.0).astype(BF16)
        moe = _dot(inv, acc_ref[0:MOE_SORTED, :].astype(BF16))
        h = h_ref[...] + gate_ref[...] * moe
        o_ref[...] = h * lax.rsqrt(jnp.mean(h * h, axis=-1, keepdims=True) + NORM_EPS) * nw_ref[...]


def _moe(xf2, comb, wg, wu, wd, h2, mod4, norm_out_w, seq):
    t, d = xf2.shape
    f = wg.shape[-1]
    tm = MOE_TILE
    per_b = seq // tm
    eps = MOE_EXPERTS_PER_STEP
    row_spec = pl.BlockSpec((tm, d), lambda i, p: (i, 0))
    return pl.pallas_call(
        _moe_kernel,
        grid=(t // tm, N_EXPERTS // eps),
        in_specs=[
            row_spec,
            pl.BlockSpec((tm, LANES), lambda i, p: (i, 0)),
            pl.BlockSpec((eps, d, f), lambda i, p: (p, 0, 0)),
            pl.BlockSpec((eps, d, f), lambda i, p: (p, 0, 0)),
            pl.BlockSpec((eps, f, d), lambda i, p: (p, 0, 0)),
            row_spec,
            pl.BlockSpec((None, None, 1, d), lambda i, p: (i // per_b, 5, 0, 0)),
            pl.BlockSpec((1, d), lambda i, p: (0, 0)),
        ],
        out_specs=row_spec,
        out_shape=jax.ShapeDtypeStruct((t, d), F32),
        scratch_shapes=[
            pltpu.VMEM((MOE_ROWS, d), BF16),
            pltpu.VMEM((MOE_ROWS, LANES), F32),
            pltpu.VMEM((MOE_ROWS, d), F32),
            pltpu.VMEM((tm, LANES), jnp.int32),
            pltpu.VMEM((tm, tm), BF16),
            pltpu.SMEM((2 * N_GROUPS,), jnp.int32),
        ],
        compiler_params=pltpu.CompilerParams(dimension_semantics=("arbitrary", "arbitrary"),
                                             vmem_limit_bytes=MOE_VMEM_LIMIT),
        name="moe",
    )(xf2, comb, wg, wu, wd, h2, mod4, norm_out_w.reshape(1, d))


def _pad_lanes(a):
    return jnp.pad(a, ((0, 0), (0, LANES - a.shape[1])))


def _layer(h3, c, mod_w, mod_b, norm_mix_w, w_in, conv_w, a_log, dt_bias, gdn_norm_w, w_out, norm_ffn_w,
           w_group, b_group, w_router, b_router, w_gate, w_up, w_down, norm_out_w):
    b, s, d = h3.shape
    t = b * s
    x2 = h3.reshape(t, d)

    n_gate_cols = 2 * GDN_HEADS
    small_lo = COL_GATE_A
    w_main = jnp.concatenate([w_in[:, :small_lo], w_in[:, small_lo + n_gate_cols:]], axis=1).astype(BF16)
    w_small = _pad_lanes(w_in[:, small_lo:small_lo + n_gate_cols])
    w_route = _pad_lanes(jnp.concatenate([w_group, w_router], axis=1))
    b_route = _pad_lanes(jnp.concatenate([b_group, b_router.reshape(-1)])[None, :])
    alog_row = _pad_lanes(a_log[None, :])
    dtb_row = _pad_lanes(dt_bias[None, :])
    f = w_gate.shape[-1]
    wg = w_gate.reshape(N_EXPERTS, d, f).astype(BF16)
    wu = w_up.reshape(N_EXPERTS, d, f).astype(BF16)
    wd = w_down.reshape(N_EXPERTS, f, d).astype(BF16)

    half = RET_DK // 2
    inv_freq = 1.0 / (ROPE_BASE ** (jnp.arange(half, dtype=F32) / half))
    ang = jnp.arange(s, dtype=F32)[:, None] * inv_freq[None, :]
    cos_t = jnp.concatenate([jnp.cos(ang), jnp.cos(ang)], axis=1)
    sin_t = jnp.concatenate([-jnp.sin(ang), jnp.sin(ang)], axis=1)
    log_gamma = jnp.log(1.0 - 2.0 ** (-5.0 - jnp.arange(RET_HEADS, dtype=F32)))
    log_gamma = jnp.broadcast_to(log_gamma[:, None], (RET_HEADS, RET_DV))

    mod4 = _mod(c, mod_w, mod_b).reshape(b, N_MOD, 1, d)
    proj, gates = _inproj(x2, mod4, norm_mix_w, w_main, w_small, s)
    proj3 = proj.reshape(b, s, N_MAIN)
    ya = _retention(proj3, cos_t, sin_t, log_gamma)
    yb = _gdn(proj3, gates.reshape(b, s, LANES), alog_row, dtb_row, conv_w, gdn_norm_w)
    h2, xf2, comb = _outproj(ya.reshape(t, d), yb.reshape(t, d), x2, w_out.astype(BF16), mod4, norm_ffn_w,
                             w_route, b_route, s)
    return _moe(xf2, comb, wg, wu, wd, h2, mod4, norm_out_w, s).reshape(b, s, d)


def kernel(x, c, mod_w, mod_b, norm_mix_w, w_in, gdn_conv_w, gdn_a_log, gdn_dt_bias, gdn_norm_w, w_out, norm_ffn_w,
           w_group, b_group, w_router, b_router, w_gate, w_up, w_down, norm_out_w):
    assert mod_w.shape[0] == 1, "one residual layer"
    return _layer(x, c, mod_w[0], mod_b[0], norm_mix_w[0], w_in[0], gdn_conv_w[0], gdn_a_log[0], gdn_dt_bias[0],
                  gdn_norm_w[0], w_out[0], norm_ffn_w[0], w_group[0], b_group[0], w_router[0], b_router[0],
                  w_gate[0], w_up[0], w_down[0], norm_out_w)
```

```python
import functools

import jax
import jax.numpy as jnp
from jax import lax
from jax.experimental import pallas as pl
from jax.experimental.pallas import tpu as pltpu

F32 = jnp.float32
BF16 = jnp.bfloat16
HIGHEST = lax.Precision.HIGHEST

RET_HEADS = 4
RET_DK = 128
RET_DV = 256
GDN_HEADS = 4
GDN_DK = 128
GDN_DV = 256
GDN_CHUNK = 64
GDN_BATCH = 2
CONV_K = 4
N_GROUPS = 4
EXPERTS_PER_GROUP = 4
N_EXPERTS = N_GROUPS * EXPERTS_PER_GROUP
D_FF_EXPERT = 512
ROPE_BASE = 10000.0
NORM_EPS = 1e-6
L2_EPS = 1e-6
N_MOD = 6
LANES = 128
SEQ_TILE = 256
VMEM_LIMIT = 48 * 1024 * 1024
MOE_TILE = 1024
MOE_ROW_ALIGN = 16
MOE_SORTED = MOE_TILE + N_GROUPS * MOE_ROW_ALIGN
MOE_PIECE = 512
MOE_BUCKET = 64
MOE_MIN_PIECE = 128
MOE_ROWS = MOE_SORTED + MOE_PIECE
MOE_EXPERTS_PER_STEP = 2
MOE_FF_SLAB = 256
MOE_VMEM_LIMIT = 56 * 1024 * 1024

COL_RQ, COL_RK, COL_RV, COL_RG = 0, 512, 1024, 2048
COL_GQ, COL_GK, COL_GV, COL_GZ = 3072, 3584, 4096, 5120
COL_GATE_A, COL_GATE_B = 6144, 7168
N_MAIN = 8192


def _silu(x):
    return x * jax.nn.sigmoid(x)


def _dot(a, b, **kw):
    return jnp.dot(a, b, preferred_element_type=F32, **kw)


def _dot_nt(a, b, **kw):
    return lax.dot_general(a, b, (((1,), (1,)), ((), ())), preferred_element_type=F32, **kw)


def _dot_tn(a, b, **kw):
    return lax.dot_general(a, b, (((0,), (0,)), ((), ())), preferred_element_type=F32, **kw)


def _params(*sem):
    return pltpu.CompilerParams(dimension_semantics=sem, vmem_limit_bytes=VMEM_LIMIT)


def _mod_kernel(c_ref, w_ref, b_ref, o_ref):
    a = _silu(c_ref[...])
    o_ref[...] = _dot(a, w_ref[...], precision=HIGHEST) + b_ref[...]


def _mod(c, mod_w, mod_b):
    b, d = c.shape
    n = mod_w.shape[1]
    tn = d
    return pl.pallas_call(
        _mod_kernel,
        grid=(n // tn,),
        in_specs=[
            pl.BlockSpec((b, d), lambda j: (0, 0)),
            pl.BlockSpec((d, tn), lambda j: (0, j)),
            pl.BlockSpec((1, tn), lambda j: (0, j)),
        ],
        out_specs=pl.BlockSpec((b, tn), lambda j: (0, j)),
        out_shape=jax.ShapeDtypeStruct((b, n), F32),
        compiler_params=_params("arbitrary"),
        name="mod",
    )(c, mod_w, mod_b.reshape(1, n))


def _inproj_kernel(x0_ref, shift0_ref, scale0_ref, xn_ref, shiftn_ref, scalen_ref, nw_ref, w_ref, ws_ref,
                   o_ref, og_ref, xb_even_ref, xb_odd_ref):
    i = pl.program_id(0)
    j = pl.program_id(1)
    tm = x0_ref.shape[0]
    slab = tm // pl.num_programs(1)

    def prepare(x, shift, scale):
        y = x * lax.rsqrt(jnp.mean(x * x, axis=-1, keepdims=True) + NORM_EPS) * nw_ref[...]
        return (y * (1.0 + scale) + shift).astype(BF16)

    @pl.when(jnp.logical_and(i == 0, j == 0))
    def _():
        xb_even_ref[...] = prepare(x0_ref[...], shift0_ref[...], scale0_ref[...])

    def step(cur_ref, nxt_ref):
        rows = pl.ds(pl.multiple_of(j * slab, slab), slab)
        nxt_ref[rows, :] = prepare(xn_ref[rows, :], shiftn_ref[...], scalen_ref[...])
        xb = cur_ref[...]
        o_ref[...] = _dot(xb, w_ref[...]).astype(BF16)

        @pl.when(j == 0)
        def _():
            og_ref[...] = _dot(xb, ws_ref[...])

    @pl.when(i % 2 == 0)
    def _():
        step(xb_even_ref, xb_odd_ref)

    @pl.when(i % 2 == 1)
    def _():
        step(xb_odd_ref, xb_even_ref)


def _inproj(x2, mod4, norm_w, w_main, w_small, seq, tm=1024, tn=2048):
    t, d = x2.shape
    n = w_main.shape[1]
    per_b = seq // tm
    last = t // tm - 1

    def nxt(i):
        return jnp.minimum(i + 1, last)

    return pl.pallas_call(
        _inproj_kernel,
        grid=(t // tm, n // tn),
        in_specs=[
            pl.BlockSpec((tm, d), lambda i, j: (0, 0)),
            pl.BlockSpec((None, None, 1, d), lambda i, j: (0, 0, 0, 0)),
            pl.BlockSpec((None, None, 1, d), lambda i, j: (0, 1, 0, 0)),
            pl.BlockSpec((tm, d), lambda i, j: (nxt(i), 0)),
            pl.BlockSpec((None, None, 1, d), lambda i, j: (nxt(i) // per_b, 0, 0, 0)),
            pl.BlockSpec((None, None, 1, d), lambda i, j: (nxt(i) // per_b, 1, 0, 0)),
            pl.BlockSpec((1, d), lambda i, j: (0, 0)),
            pl.BlockSpec((d, tn), lambda i, j: (0, j)),
            pl.BlockSpec((d, LANES), lambda i, j: (0, 0)),
        ],
        out_specs=[
            pl.BlockSpec((tm, tn), lambda i, j: (i, j)),
            pl.BlockSpec((tm, LANES), lambda i, j: (i, 0)),
        ],
        out_shape=[
            jax.ShapeDtypeStruct((t, n), BF16),
            jax.ShapeDtypeStruct((t, LANES), F32),
        ],
        scratch_shapes=[pltpu.VMEM((tm, d), BF16), pltpu.VMEM((tm, d), BF16)],
        compiler_params=_params("arbitrary", "arbitrary"),
        name="inproj",
    )(x2, mod4, mod4, x2, mod4, mod4, norm_w.reshape(1, d), w_main, w_small.astype(BF16))


def _ret_kernel(lg_ref, q_ref, k_ref, v_ref, rg_ref, ga_ref, cos_ref, sin_ref, o_ref,
                state_ref, intra_ref, qd_ref, kd_ref):
    c = SEQ_TILE
    first = jnp.logical_and(pl.program_id(0) == 0, pl.program_id(1) == 0)

    @pl.when(first)
    def _():
        row = lax.broadcasted_iota(jnp.int32, (c, c), 0)
        col = lax.broadcasted_iota(jnp.int32, (c, c), 1)
        rel = (row - col).astype(F32)
        causal = row >= col
        pos = lax.broadcasted_iota(jnp.int32, (c, RET_DK), 0).astype(F32)
        for h in range(RET_HEADS):
            lg = lg_ref[h:h + 1, :]
            intra_ref[h] = jnp.where(causal, jnp.exp(jnp.where(causal, rel, 0.0) * lg), 0.0)
            qd_ref[h] = jnp.exp((pos + 1.0) * lg[:, :RET_DK])
            kd_ref[h] = jnp.exp((c - 1.0 - pos) * lg[:, :RET_DK])

    @pl.when(pl.program_id(1) == 0)
    def _():
        state_ref[...] = jnp.zeros_like(state_ref)

    cos = cos_ref[...]
    sin = sin_ref[...]
    for h in range(RET_HEADS):
        qs = slice(h * RET_DK, (h + 1) * RET_DK)
        vs = slice(h * RET_DV, (h + 1) * RET_DV)
        qr = q_ref[:, qs].astype(F32)
        kr = k_ref[:, qs].astype(F32)
        q = qr * cos + pltpu.roll(qr, RET_DK // 2, 1) * sin
        k = (kr * cos + pltpu.roll(kr, RET_DK // 2, 1) * sin) * (RET_DK ** -0.5)
        v = v_ref[:, vs]
        state = state_ref[h]
        chunk_decay = jnp.exp(float(c) * lg_ref[h:h + 1, :])
        scores = _dot_nt(q.astype(BF16), k.astype(BF16)) * intra_ref[h]
        o = _dot(scores.astype(BF16), v) + _dot((q * qd_ref[h]).astype(BF16), state.astype(BF16))
        state_ref[h] = state * chunk_decay + _dot_tn((k * kd_ref[h]).astype(BF16), v)
        o = o * lax.rsqrt(jnp.mean(o * o, axis=-1, keepdims=True) + NORM_EPS)
        y = _silu(rg_ref[:, vs].astype(F32)) * o
        o_ref[:, vs] = (jax.nn.sigmoid(ga_ref[:, vs].astype(F32)) * y).astype(BF16)


def _retention(proj3, cos_t, sin_t, log_gamma):
    b, s, _ = proj3.shape
    ts = SEQ_TILE
    qk_w = RET_HEADS * RET_DK
    v_w = RET_HEADS * RET_DV
    return pl.pallas_call(
        _ret_kernel,
        grid=(b, s // ts),
        in_specs=[
            pl.BlockSpec((RET_HEADS, RET_DV), lambda i, j: (0, 0)),
            pl.BlockSpec((None, ts, qk_w), lambda i, j: (i, j, COL_RQ // qk_w)),
            pl.BlockSpec((None, ts, qk_w), lambda i, j: (i, j, COL_RK // qk_w)),
            pl.BlockSpec((None, ts, v_w), lambda i, j: (i, j, COL_RV // v_w)),
            pl.BlockSpec((None, ts, v_w), lambda i, j: (i, j, COL_RG // v_w)),
            pl.BlockSpec((None, ts, v_w), lambda i, j: (i, j, COL_GATE_A // v_w)),
            pl.BlockSpec((ts, RET_DK), lambda i, j: (j, 0)),
            pl.BlockSpec((ts, RET_DK), lambda i, j: (j, 0)),
        ],
        out_specs=pl.BlockSpec((None, ts, v_w), lambda i, j: (i, j, 0)),
        out_shape=jax.ShapeDtypeStruct((b, s, v_w), BF16),
        scratch_shapes=[
            pltpu.VMEM((RET_HEADS, RET_DK, RET_DV), F32),
            pltpu.VMEM((RET_HEADS, ts, ts), F32),
            pltpu.VMEM((RET_HEADS, ts, RET_DK), F32),
            pltpu.VMEM((RET_HEADS, ts, RET_DK), F32),
        ],
        compiler_params=_params("arbitrary", "arbitrary"),
        name="retention",
    )(log_gamma, proj3, proj3, proj3, proj3, proj3, cos_t, sin_t)


def _unit_lower_inverse(a, eye, nilpotency):
    p = eye - a
    m = a
    for _ in range((nilpotency - 1).bit_length() - 1):
        m16 = m.astype(BF16)
        m = _dot(m16, m16)
        p = p + _dot(p.astype(BF16), m.astype(BF16))
    return p


def _gdn_kernel(q_ref, k_ref, v_ref, z_ref, gb_ref, gates_ref, alog_ref, dtb_ref, cwq_ref, cwk_ref, cwv_ref,
                nw_ref, o_ref, state_ref, tail_ref):
    ts = SEQ_TILE
    cc = GDN_CHUNK
    qk_w = GDN_HEADS * GDN_DK

    @pl.when(pl.program_id(1) == 0)
    def _():
        state_ref[...] = jnp.zeros_like(state_ref)
        tail_ref[...] = jnp.zeros_like(tail_ref)

    def conv_silu(cur, tail, cw_ref):
        ext = jnp.concatenate([tail, cur], axis=0)
        out = ext[8 - (CONV_K - 1):8 - (CONV_K - 1) + ts] * cw_ref[0:1, :]
        for i in range(1, CONV_K):
            d = CONV_K - 1 - i
            out = out + ext[8 - d:8 - d + ts] * cw_ref[i:i + 1, :]
        return _silu(out)

    row = lax.broadcasted_iota(jnp.int32, (ts, ts), 0)
    col = lax.broadcasted_iota(jnp.int32, (ts, ts), 1)
    same_chunk = (row // cc) == (col // cc)
    causal = jnp.logical_and(same_chunk, row >= col)
    strict = jnp.logical_and(same_chunk, row > col)
    eye = jnp.where(row == col, 1.0, 0.0)
    tri_lower = jnp.where(causal, 1.0, 0.0)
    tri_upper = jnp.where(jnp.logical_and(same_chunk, row <= col), 1.0, 0.0)

    units = [(s, h) for s in range(GDN_BATCH) for h in range(GDN_HEADS)]
    q_all, k_all, v_all, beta_all, gc_col_all, gc_row_all = [], [], [], [], [], []
    for s in range(GDN_BATCH):
        q_raw = q_ref[s].astype(F32)
        k_raw = k_ref[s].astype(F32)
        v_raw = v_ref[s].astype(F32)
        q_all.append(conv_silu(q_raw, tail_ref[s, :, 0:qk_w], cwq_ref))
        k_all.append(conv_silu(k_raw, tail_ref[s, :, qk_w:2 * qk_w], cwk_ref))
        v_all.append(conv_silu(v_raw, tail_ref[s, :, 2 * qk_w:], cwv_ref))
        tail_ref[s, :, 0:qk_w] = q_raw[ts - 8:]
        tail_ref[s, :, qk_w:2 * qk_w] = k_raw[ts - 8:]
        tail_ref[s, :, 2 * qk_w:] = v_raw[ts - 8:]

        gates = gates_ref[s]
        x = gates + dtb_ref[...]
        softplus = jnp.maximum(x, 0.0) + jnp.log1p(jnp.exp(-jnp.abs(x)))
        g_all = -jnp.exp(alog_ref[...]) * softplus
        beta_all.append(jax.nn.sigmoid(gates))
        gc_col_all.append(_dot(tri_lower, g_all, precision=HIGHEST))
        gc_row_all.append(_dot_tn(g_all, tri_upper, precision=HIGHEST))

    qn, kn, k16, k_beta, gc, decay, beta = {}, {}, {}, {}, {}, {}, {}
    for u in units:
        s, h = u
        qs = slice(h * GDN_DK, (h + 1) * GDN_DK)
        qh = q_all[s][:, qs]
        kh = k_all[s][:, qs]
        qn[u] = qh * lax.rsqrt(jnp.sum(qh * qh, axis=-1, keepdims=True) + L2_EPS) * (GDN_DK ** -0.5)
        kn[u] = kh * lax.rsqrt(jnp.sum(kh * kh, axis=-1, keepdims=True) + L2_EPS)
        gc[u] = gc_col_all[s][:, h:h + 1]
        gcr = gc_row_all[s][h:h + 1, :]
        decay[u] = jnp.exp(jnp.where(causal, gc[u] - gcr, -jnp.inf))
        beta[u] = beta_all[s][:, GDN_HEADS + h:GDN_HEADS + h + 1]
        k_beta[u] = kn[u] * beta[u]
        k16[u] = kn[u].astype(BF16)
    a_mat = {u: jnp.where(strict, _dot_nt(k_beta[u].astype(BF16), k16[u]) * decay[u], 0.0) for u in units}
    p = {u: eye - a_mat[u] for u in units}
    m = a_mat
    for _ in range((cc - 1).bit_length() - 1):
        m16 = {u: m[u].astype(BF16) for u in units}
        m = {u: _dot(m16[u], m16[u]) for u in units}
        p = {u: p[u] + _dot(p[u].astype(BF16), m[u].astype(BF16)) for u in units}
    u_all, lhs_state, lhs_vnew, state_decay = {}, {}, {}, {}
    for u in units:
        s, h = u
        vs = slice(h * GDN_DV, (h + 1) * GDN_DV)
        t16 = p[u].astype(BF16)
        egc = jnp.exp(gc[u])
        u_all[u] = _dot(t16, (v_all[s][:, vs] * beta[u]).astype(BF16))
        w_all = _dot(t16, (k_beta[u] * egc).astype(BF16))
        qk = _dot_nt(qn[u].astype(BF16), k16[u]) * decay[u]
        qg = qn[u] * egc
        kn_t = kn[u].T
        gcr = gc_row_all[s][h:h + 1, :]
        for n in range(ts // cc):
            rs = slice(n * cc, (n + 1) * cc)
            g_last = gc[u][(n + 1) * cc - 1:(n + 1) * cc, :]
            kg_t = kn_t[:, rs] * jnp.exp(g_last - gcr[:, rs])
            lhs_state[u, n] = jnp.concatenate([w_all[rs], qg[rs]], axis=0).astype(BF16)
            lhs_vnew[u, n] = jnp.concatenate([qk[rs, rs], kg_t], axis=0).astype(BF16)
            state_decay[u, n] = jnp.exp(g_last)
    state = {u: state_ref[u[0], u[1]] for u in units}
    outs = {u: [] for u in units}
    for n in range(ts // cc):
        rs = slice(n * cc, (n + 1) * cc)
        for u in units:
            from_state = _dot(lhs_state[u, n], state[u].astype(BF16))
            v_new = (u_all[u][rs] - from_state[:cc]).astype(BF16)
            from_vnew = _dot(lhs_vnew[u, n], v_new)
            outs[u].append(from_state[cc:] + from_vnew[:cc])
            state[u] = state[u] * state_decay[u, n] + from_vnew[cc:]
    for u in units:
        s, h = u
        vs = slice(h * GDN_DV, (h + 1) * GDN_DV)
        state_ref[s, h] = state[u]
        o = jnp.concatenate(outs[u], axis=0)
        o = o * lax.rsqrt(jnp.mean(o * o, axis=-1, keepdims=True) + NORM_EPS) * nw_ref[...]
        o = o * _silu(z_ref[s, :, vs].astype(F32))
        o_ref[s, :, vs] = (jax.nn.sigmoid(gb_ref[s, :, vs].astype(F32)) * o).astype(BF16)


def _gdn(proj3, gates3, alog_row, dtb_row, conv_w, norm_w):
    b, s, _ = proj3.shape
    ts = SEQ_TILE
    nb = GDN_BATCH
    qk_w = GDN_HEADS * GDN_DK
    v_w = GDN_HEADS * GDN_DV
    return pl.pallas_call(
        _gdn_kernel,
        grid=(b // nb, s // ts),
        in_specs=[
            pl.BlockSpec((nb, ts, qk_w), lambda i, j: (i, j, COL_GQ // qk_w)),
            pl.BlockSpec((nb, ts, qk_w), lambda i, j: (i, j, COL_GK // qk_w)),
            pl.BlockSpec((nb, ts, v_w), lambda i, j: (i, j, COL_GV // v_w)),
            pl.BlockSpec((nb, ts, v_w), lambda i, j: (i, j, COL_GZ // v_w)),
            pl.BlockSpec((nb, ts, v_w), lambda i, j: (i, j, COL_GATE_B // v_w)),
            pl.BlockSpec((nb, ts, LANES), lambda i, j: (i, j, 0)),
            pl.BlockSpec((1, LANES), lambda i, j: (0, 0)),
            pl.BlockSpec((1, LANES), lambda i, j: (0, 0)),
            pl.BlockSpec((CONV_K, qk_w), lambda i, j: (0, 0)),
            pl.BlockSpec((CONV_K, qk_w), lambda i, j: (0, 1)),
            pl.BlockSpec((CONV_K, v_w), lambda i, j: (0, 1)),
            pl.BlockSpec((1, GDN_DV), lambda i, j: (0, 0)),
        ],
        out_specs=pl.BlockSpec((nb, ts, v_w), lambda i, j: (i, j, 0)),
        out_shape=jax.ShapeDtypeStruct((b, s, v_w), BF16),
        scratch_shapes=[
            pltpu.VMEM((nb, GDN_HEADS, GDN_DK, GDN_DV), F32),
            pltpu.VMEM((nb, 8, 2 * qk_w + v_w), F32),
        ],
        compiler_params=_params("arbitrary", "arbitrary"),
        name="gdn",
    )(proj3, proj3, proj3, proj3, proj3, gates3, alog_row, dtb_row, conv_w, conv_w, conv_w,
      norm_w.reshape(1, GDN_DV))


def _route(logits):
    lane = lax.broadcasted_iota(jnp.int32, logits.shape, 1)
    neg = jnp.float32(-jnp.inf)

    def masked_softmax(mask):
        m = jnp.max(jnp.where(mask, logits, neg), axis=-1, keepdims=True)
        e = jnp.where(mask, jnp.exp(jnp.where(mask, logits, m) - m), 0.0)
        return e / jnp.sum(e, axis=-1, keepdims=True)

    def first_argmax(p, mask):
        top = jnp.max(jnp.where(mask, p, -1.0), axis=-1, keepdims=True)
        idx = jnp.min(jnp.where(jnp.logical_and(mask, p == top), lane, LANES), axis=-1, keepdims=True)
        return top, idx

    gmask = lane < N_GROUPS
    g_top, g_idx = first_argmax(masked_softmax(gmask), gmask)
    lo = N_GROUPS + EXPERTS_PER_GROUP * g_idx
    emask = jnp.logical_and(lane >= lo, lane < lo + EXPERTS_PER_GROUP)
    pe = masked_softmax(emask)
    top1, i1 = first_argmax(pe, emask)
    emask2 = jnp.logical_and(emask, lane != i1)
    top2, i2 = first_argmax(pe, emask2)
    denom = top1 + top2
    comb = jnp.where(lane == i1, g_top * (top1 / denom), jnp.where(lane == i2, g_top * (top2 / denom), 0.0))
    return jnp.where(lane == 0, g_idx.astype(F32), comb)


def _outproj_kernel(ya_ref, yb_ref, x_ref, w_ref, gate_ref, nw_ref, shift_ref, scale_ref, wr_ref, wrh_ref, br_ref,
                    h_ref, xf_ref, comb_ref):
    tm = x_ref.shape[0]
    halves = [slice(k * (tm // 2), (k + 1) * (tm // 2)) for k in range(2)]
    hs, xfs, logits = [], [], []
    for rs in halves:
        merged = (ya_ref[rs, :].astype(F32) + yb_ref[rs, :].astype(F32)).astype(BF16)
        hs.append(x_ref[rs, :] + gate_ref[...] * _dot(merged, w_ref[...]))
    for rs, h in zip(halves, hs):
        h_ref[rs, :] = h
        y = h * lax.rsqrt(jnp.mean(h * h, axis=-1, keepdims=True) + NORM_EPS) * nw_ref[...]
        xfs.append(y * (1.0 + scale_ref[...]) + shift_ref[...])
    for rs, xf in zip(halves, xfs):
        xf_hi = xf.astype(BF16)
        xf_ref[rs, :] = xf_hi
        xf_lo = (xf - xf_hi.astype(F32)).astype(BF16)
        both = _dot(xf_hi, wr_ref[...])
        logits.append(both[:, :LANES] + both[:, LANES:] + _dot(xf_lo, wrh_ref[...]) + br_ref[...])
    for rs, lg in zip(halves, logits):
        comb_ref[rs, :] = _route(lg)


def _outproj(ya2, yb2, x2, w_out_b, mod4, norm_w, w_route, b_route, seq, tm=512):
    t, d = x2.shape
    per_b = seq // tm
    row_spec = pl.BlockSpec((tm, d), lambda i: (i, 0))
    wr_hi = w_route.astype(BF16)
    wr_lo = (w_route - wr_hi.astype(F32)).astype(BF16)

    def mod_spec(k):
        return pl.BlockSpec((None, None, 1, d), lambda i: (i // per_b, k, 0, 0))

    return pl.pallas_call(
        _outproj_kernel,
        grid=(t // tm,),
        in_specs=[
            row_spec, row_spec, row_spec,
            pl.BlockSpec((d, d), lambda i: (0, 0)),
            mod_spec(2),
            pl.BlockSpec((1, d), lambda i: (0, 0)),
            mod_spec(3),
            mod_spec(4),
            pl.BlockSpec((d, 2 * LANES), lambda i: (0, 0)),
            pl.BlockSpec((d, LANES), lambda i: (0, 0)),
            pl.BlockSpec((1, LANES), lambda i: (0, 0)),
        ],
        out_specs=[row_spec, row_spec, pl.BlockSpec((tm, LANES), lambda i: (i, 0))],
        out_shape=[
            jax.ShapeDtypeStruct((t, d), F32),
            jax.ShapeDtypeStruct((t, d), BF16),
            jax.ShapeDtypeStruct((t, LANES), F32),
        ],
        compiler_params=_params("arbitrary"),
        name="outproj",
    )(ya2, yb2, x2, w_out_b, mod4, norm_w.reshape(1, d), mod4, mod4,
      jnp.concatenate([wr_hi, wr_lo], axis=1), wr_hi, b_route)


def _moe_kernel(xf_ref, comb_ref, wg_ref, wu_ref, wd_ref, h_ref, gate_ref, nw_ref, o_ref,
                xs_ref, cs_ref, acc_ref, pos_ref, ltri_ref, meta_ref):
    i = pl.program_id(0)
    p = pl.program_id(1)
    tm = MOE_TILE
    steps_per_group = EXPERTS_PER_GROUP // MOE_EXPERTS_PER_STEP
    g = p // steps_per_group

    @pl.when(jnp.logical_and(i == 0, p == 0))
    def _():
        row = lax.broadcasted_iota(jnp.int32, (tm, tm), 0)
        col = lax.broadcasted_iota(jnp.int32, (tm, tm), 1)
        ltri_ref[...] = jnp.where(row >= col, 1.0, 0.0).astype(BF16)
        xs_ref[MOE_SORTED:, :] = jnp.zeros((MOE_ROWS - MOE_SORTED, xs_ref.shape[1]), BF16)
        cs_ref[MOE_SORTED:, :] = jnp.zeros((MOE_ROWS - MOE_SORTED, LANES), F32)
        acc_ref[MOE_SORTED:, :] = jnp.zeros((MOE_ROWS - MOE_SORTED, acc_ref.shape[1]), F32)

    @pl.when(p == 0)
    def _():
        comb = comb_ref[...]
        lane = lax.broadcasted_iota(jnp.int32, (tm, LANES), 1)
        gidx = comb[:, 0:1]
        mine = jnp.logical_and(lane.astype(F32) == gidx, lane < N_GROUPS)
        csum = _dot(ltri_ref[...], jnp.where(mine, 1.0, 0.0).astype(BF16))
        counts = jnp.broadcast_to(csum[tm - 1:tm, :], (8, LANES))
        aligned = jnp.floor((counts + (MOE_ROW_ALIGN - 0.5)) * (1.0 / MOE_ROW_ALIGN)) * MOE_ROW_ALIGN
        lr = lax.broadcasted_iota(jnp.int32, (LANES, LANES), 0)
        lc = lax.broadcasted_iota(jnp.int32, (LANES, LANES), 1)
        seg_start = _dot(aligned, jnp.where(lr < lc, 1.0, 0.0), precision=HIGHEST)
        pos = jnp.sum(jnp.where(mine, seg_start[0:1, :] + csum - 1.0, 0.0), axis=-1, keepdims=True)
        pos_b = jnp.broadcast_to(pos, (tm, LANES))
        pos_ref[...] = pos_b.astype(jnp.int32)
        pos_row = pos_b.T[0:1, :].astype(jnp.int32)
        lane1 = lax.broadcasted_iota(jnp.int32, (8, LANES), 1)
        for k in range(N_GROUPS):
            meta_ref[k] = jnp.sum(jnp.where(lane1 == k, counts, 0.0)[0:1, :]).astype(jnp.int32)
            meta_ref[N_GROUPS + k] = jnp.sum(jnp.where(lane1 == k, seg_start, 0.0)[0:1, :]).astype(jnp.int32)

        c_hi = comb.astype(BF16)
        c_lo = (comb - c_hi.astype(F32)).astype(BF16)
        sub = lax.broadcasted_iota(jnp.int32, (MOE_SORTED, tm), 0)
        perm = jnp.where(pos_row == sub, 1.0, 0.0).astype(BF16)
        xs_ref[0:MOE_SORTED, :] = _dot(perm, xf_ref[...]).astype(BF16)
        cs_ref[0:MOE_SORTED, :] = _dot(perm, c_hi) + _dot(perm, c_lo)
        acc_ref[0:MOE_SORTED, :] = jnp.zeros((MOE_SORTED, acc_ref.shape[1]), F32)

    def expert_piece(start, m):
        rows = pl.ds(pl.multiple_of(start, MOE_ROW_ALIGN), m)
        xs = xs_ref[rows, :]
        cs = cs_ref[rows, :]
        lane_m = lax.broadcasted_iota(jnp.int32, (m, LANES), 1)
        contrib = None
        for j in range(MOE_EXPERTS_PER_STEP):
            e = p * MOE_EXPERTS_PER_STEP + j
            wcol = jnp.sum(jnp.where(lane_m == N_GROUPS + e, cs, 0.0), axis=-1, keepdims=True)
            for f0 in range(0, D_FF_EXPERT, MOE_FF_SLAB):
                fs = slice(f0, f0 + MOE_FF_SLAB)
                act = _silu(_dot(xs, wg_ref[j, :, fs])) * _dot(xs, wu_ref[j, :, fs]) * wcol
                part = _dot(act.astype(BF16), wd_ref[j, fs, :])
                contrib = part if contrib is None else contrib + part
        acc_ref[rows, :] += contrib

    count = meta_ref[g]
    seg = meta_ref[N_GROUPS + g]
    n_full = jnp.maximum(count - 1, 0) // MOE_PIECE

    def full_piece(k, carry):
        expert_piece(seg + k * MOE_PIECE, MOE_PIECE)
        return carry

    lax.fori_loop(0, n_full, full_piece, 0)
    last_start = seg + n_full * MOE_PIECE
    last_rows = count - n_full * MOE_PIECE
    for m in range(MOE_MIN_PIECE, MOE_PIECE + 1, MOE_BUCKET):
        lo = 0 if m == MOE_MIN_PIECE else m - MOE_BUCKET

        @pl.when(jnp.logical_and(last_rows > lo, last_rows <= m))
        def _(m=m):
            expert_piece(last_start, m)

    @pl.when(p == pl.num_programs(1) - 1)
    def _():
        pos = pos_ref[:, 0:1]
        lane_s = lax.broadcasted_iota(jnp.int32, (tm, MOE_SORTED), 1)
        inv = jnp.where(pos == lane_s, 1.0, 0.0).astype(BF16)
        moe = _dot(inv, acc_ref[0:MOE_SORTED, :].astype(BF16))
        h = h_ref[...] + gate_ref[...] * moe
        o_ref[...] = h * lax.rsqrt(jnp.mean(h * h, axis=-1, keepdims=True) + NORM_EPS) * nw_ref[...]


def _moe(xf2, comb, wg, wu, wd, h2, mod4, norm_out_w, seq):
    t, d = xf2.shape
    f = wg.shape[-1]
    tm = MOE_TILE
    per_b = seq // tm
    eps = MOE_EXPERTS_PER_STEP
    row_spec = pl.BlockSpec((tm, d), lambda i, p: (i, 0))
    return pl.pallas_call(
        _moe_kernel,
        grid=(t // tm, N_EXPERTS // eps),
        in_specs=[
            row_spec,
            pl.BlockSpec((tm, LANES), lambda i, p: (i, 0)),
            pl.BlockSpec((eps, d, f), lambda i, p: (p, 0, 0)),
            pl.BlockSpec((eps, d, f), lambda i, p: (p, 0, 0)),
            pl.BlockSpec((eps, f, d), lambda i, p: (p, 0, 0)),
            row_spec,
            pl.BlockSpec((None, None, 1, d), lambda i, p: (i // per_b, 5, 0, 0)),
            pl.BlockSpec((1, d), lambda i, p: (0, 0)),
        ],
        out_specs=row_spec,
        out_shape=jax.ShapeDtypeStruct((t, d), F32),
        scratch_shapes=[
            pltpu.VMEM((MOE_ROWS, d), BF16),
            pltpu.VMEM((MOE_ROWS, LANES), F32),
            pltpu.VMEM((MOE_ROWS, d), F32),
            pltpu.VMEM((tm, LANES), jnp.int32),
            pltpu.VMEM((tm, tm), BF16),
            pltpu.SMEM((2 * N_GROUPS,), jnp.int32),
        ],
        compiler_params=pltpu.CompilerParams(dimension_semantics=("arbitrary", "arbitrary"),
                                             vmem_limit_bytes=MOE_VMEM_LIMIT),
        name="moe",
    )(xf2, comb, wg, wu, wd, h2, mod4, norm_out_w.reshape(1, d))


def _pad_lanes(a):
    return jnp.pad(a, ((0, 0), (0, LANES - a.shape[1])))


def _layer(h3, c, mod_w, mod_b, norm_mix_w, w_in, conv_w, a_log, dt_bias, gdn_norm_w, w_out, norm_ffn_w,
           w_group, b_group, w_router, b_router, w_gate, w_up, w_down, norm_out_w):
    b, s, d = h3.shape
    t = b * s
    x2 = h3.reshape(t, d)

    n_gate_cols = 2 * GDN_HEADS
    small_lo = COL_GATE_A
    w_main = jnp.concatenate([w_in[:, :small_lo], w_in[:, small_lo + n_gate_cols:]], axis=1).astype(BF16)
    w_small = _pad_lanes(w_in[:, small_lo:small_lo + n_gate_cols])
    w_route = _pad_lanes(jnp.concatenate([w_group, w_router], axis=1))
    b_route = _pad_lanes(jnp.concatenate([b_group, b_router.reshape(-1)])[None, :])
    alog_row = _pad_lanes(a_log[None, :])
    dtb_row = _pad_lanes(dt_bias[None, :])
    f = w_gate.shape[-1]
    wg = w_gate.reshape(N_EXPERTS, d, f).astype(BF16)
    wu = w_up.reshape(N_EXPERTS, d, f).astype(BF16)
    wd = w_down.reshape(N_EXPERTS, f, d).astype(BF16)

    half = RET_DK // 2
    inv_freq = 1.0 / (ROPE_BASE ** (jnp.arange(half, dtype=F32) / half))
    ang = jnp.arange(s, dtype=F32)[:, None] * inv_freq[None, :]
    cos_t = jnp.concatenate([jnp.cos(ang), jnp.cos(ang)], axis=1)
    sin_t = jnp.concatenate([-jnp.sin(ang), jnp.sin(ang)], axis=1)
    log_gamma = jnp.log(1.0 - 2.0 ** (-5.0 - jnp.arange(RET_HEADS, dtype=F32)))
    log_gamma = jnp.broadcast_to(log_gamma[:, None], (RET_HEADS, RET_DV))

    mod4 = _mod(c, mod_w, mod_b).reshape(b, N_MOD, 1, d)
    proj, gates = _inproj(x2, mod4, norm_mix_w, w_main, w_small, s)
    proj3 = proj.reshape(b, s, N_MAIN)
    ya = _retention(proj3, cos_t, sin_t, log_gamma)
    yb = _gdn(proj3, gates.reshape(b, s, LANES), alog_row, dtb_row, conv_w, gdn_norm_w)
    h2, xf2, comb = _outproj(ya.reshape(t, d), yb.reshape(t, d), x2, w_out.astype(BF16), mod4, norm_ffn_w,
                             w_route, b_route, s)
    return _moe(xf2, comb, wg, wu, wd, h2, mod4, norm_out_w, s).reshape(b, s, d)


def kernel(x, c, mod_w, mod_b, norm_mix_w, w_in, gdn_conv_w, gdn_a_log, gdn_dt_bias, gdn_norm_w, w_out, norm_ffn_w,
           w_group, b_group, w_router, b_router, w_gate, w_up, w_down, norm_out_w):
    assert mod_w.shape[0] == 1, "one residual layer"
    return _layer(x, c, mod_w[0], mod_b[0], norm_mix_w[0], w_in[0], gdn_conv_w[0], gdn_a_log[0], gdn_dt_bias[0],
                  gdn_norm_w[0], w_out[0], norm_ffn_w[0], w_group[0], b_group[0], w_router[0], b_router[0],
                  w_gate[0], w_up[0], w_down[0], norm_out_w)
```

```python
import functools

import jax
import jax.numpy as jnp
from jax import lax
from jax.experimental import pallas as pl
from jax.experimental.pallas import tpu as pltpu

F32 = jnp.float32
BF16 = jnp.bfloat16
HIGHEST = lax.Precision.HIGHEST

RET_HEADS = 4
RET_DK = 128
RET_DV = 256
GDN_HEADS = 4
GDN_DK = 128
GDN_DV = 256
GDN_CHUNK = 64
GDN_BATCH = 2
CONV_K = 4
N_GROUPS = 4
EXPERTS_PER_GROUP = 4
N_EXPERTS = N_GROUPS * EXPERTS_PER_GROUP
D_FF_EXPERT = 512
ROPE_BASE = 10000.0
NORM_EPS = 1e-6
L2_EPS = 1e-6
N_MOD = 6
LANES = 128
SEQ_TILE = 256
VMEM_LIMIT = 48 * 1024 * 1024
MOE_TILE = 1024
MOE_ROW_ALIGN = 16
MOE_SORTED = MOE_TILE + N_GROUPS * MOE_ROW_ALIGN
MOE_PIECE = 512
MOE_BUCKET = 64
MOE_MIN_PIECE = 128
MOE_ROWS = MOE_SORTED + MOE_PIECE
MOE_EXPERTS_PER_STEP = 2
MOE_FF_SLAB = 256
MOE_VMEM_LIMIT = 56 * 1024 * 1024

COL_RQ, COL_RK, COL_RV, COL_RG = 0, 512, 1024, 2048
COL_GQ, COL_GK, COL_GV, COL_GZ = 3072, 3584, 4096, 5120
COL_GATE_A, COL_GATE_B = 6144, 7168
N_MAIN = 8192


def _sigmoid(x):
    return 0.5 * jnp.tanh(0.5 * x) + 0.5


def _silu(x):
    h = 0.5 * x
    return h + h * jnp.tanh(h)


def _dot(a, b, **kw):
    return jnp.dot(a, b, preferred_element_type=F32, **kw)


def _dot_nt(a, b, **kw):
    return lax.dot_general(a, b, (((1,), (1,)), ((), ())), preferred_element_type=F32, **kw)


def _dot_tn(a, b, **kw):
    return lax.dot_general(a, b, (((0,), (0,)), ((), ())), preferred_element_type=F32, **kw)


def _params(*sem):
    return pltpu.CompilerParams(dimension_semantics=sem, vmem_limit_bytes=VMEM_LIMIT)


def _mod_kernel(c_ref, w_ref, b_ref, o_ref):
    a = _silu(c_ref[...])
    o_ref[...] = _dot(a, w_ref[...], precision=HIGHEST) + b_ref[...]


def _mod(c, mod_w, mod_b):
    b, d = c.shape
    n = mod_w.shape[1]
    tn = d
    return pl.pallas_call(
        _mod_kernel,
        grid=(n // tn,),
        in_specs=[
            pl.BlockSpec((b, d), lambda j: (0, 0)),
            pl.BlockSpec((d, tn), lambda j: (0, j)),
            pl.BlockSpec((1, tn), lambda j: (0, j)),
        ],
        out_specs=pl.BlockSpec((b, tn), lambda j: (0, j)),
        out_shape=jax.ShapeDtypeStruct((b, n), F32),
        compiler_params=_params("arbitrary"),
        name="mod",
    )(c, mod_w, mod_b.reshape(1, n))


def _inproj_kernel(x0_ref, shift0_ref, scale0_ref, xn_ref, shiftn_ref, scalen_ref, nw_ref, w_ref, ws_ref,
                   o_ref, og_ref, xb_even_ref, xb_odd_ref):
    i = pl.program_id(0)
    j = pl.program_id(1)
    tm = x0_ref.shape[0]
    slab = tm // pl.num_programs(1)

    def prepare(x, shift, scale):
        y = x * lax.rsqrt(jnp.mean(x * x, axis=-1, keepdims=True) + NORM_EPS) * nw_ref[...]
        return (y * (1.0 + scale) + shift).astype(BF16)

    @pl.when(jnp.logical_and(i == 0, j == 0))
    def _():
        xb_even_ref[...] = prepare(x0_ref[...], shift0_ref[...], scale0_ref[...])

    def step(cur_ref, nxt_ref):
        rows = pl.ds(pl.multiple_of(j * slab, slab), slab)
        nxt_ref[rows, :] = prepare(xn_ref[rows, :], shiftn_ref[...], scalen_ref[...])
        xb = cur_ref[...]
        o_ref[...] = _dot(xb, w_ref[...]).astype(BF16)

        @pl.when(j == 0)
        def _():
            og_ref[...] = _dot(xb, ws_ref[...])

    @pl.when(i % 2 == 0)
    def _():
        step(xb_even_ref, xb_odd_ref)

    @pl.when(i % 2 == 1)
    def _():
        step(xb_odd_ref, xb_even_ref)


def _inproj(x2, mod4, norm_w, w_main, w_small, seq, tm=1024, tn=2048):
    t, d = x2.shape
    n = w_main.shape[1]
    per_b = seq // tm
    last = t // tm - 1

    def nxt(i):
        return jnp.minimum(i + 1, last)

    return pl.pallas_call(
        _inproj_kernel,
        grid=(t // tm, n // tn),
        in_specs=[
            pl.BlockSpec((tm, d), lambda i, j: (0, 0)),
            pl.BlockSpec((None, None, 1, d), lambda i, j: (0, 0, 0, 0)),
            pl.BlockSpec((None, None, 1, d), lambda i, j: (0, 1, 0, 0)),
            pl.BlockSpec((tm, d), lambda i, j: (nxt(i), 0)),
            pl.BlockSpec((None, None, 1, d), lambda i, j: (nxt(i) // per_b, 0, 0, 0)),
            pl.BlockSpec((None, None, 1, d), lambda i, j: (nxt(i) // per_b, 1, 0, 0)),
            pl.BlockSpec((1, d), lambda i, j: (0, 0)),
            pl.BlockSpec((d, tn), lambda i, j: (0, j)),
            pl.BlockSpec((d, LANES), lambda i, j: (0, 0)),
        ],
        out_specs=[
            pl.BlockSpec((tm, tn), lambda i, j: (i, j)),
            pl.BlockSpec((tm, LANES), lambda i, j: (i, 0)),
        ],
        out_shape=[
            jax.ShapeDtypeStruct((t, n), BF16),
            jax.ShapeDtypeStruct((t, LANES), F32),
        ],
        scratch_shapes=[pltpu.VMEM((tm, d), BF16), pltpu.VMEM((tm, d), BF16)],
        compiler_params=_params("arbitrary", "arbitrary"),
        name="inproj",
    )(x2, mod4, mod4, x2, mod4, mod4, norm_w.reshape(1, d), w_main, w_small.astype(BF16))


def _ret_kernel(lg_ref, q_ref, k_ref, v_ref, rg_ref, ga_ref, cos_ref, sin_ref, o_ref,
                state_ref, intra_ref, qd_ref, kd_ref):
    c = SEQ_TILE
    first = jnp.logical_and(pl.program_id(0) == 0, pl.program_id(1) == 0)

    @pl.when(first)
    def _():
        row = lax.broadcasted_iota(jnp.int32, (c, c), 0)
        col = lax.broadcasted_iota(jnp.int32, (c, c), 1)
        rel = (row - col).astype(F32)
        causal = row >= col
        pos = lax.broadcasted_iota(jnp.int32, (c, RET_DK), 0).astype(F32)
        for h in range(RET_HEADS):
            lg = lg_ref[h:h + 1, :]
            intra_ref[h] = jnp.where(causal, jnp.exp(jnp.where(causal, rel, 0.0) * lg), 0.0)
            qd_ref[h] = jnp.exp((pos + 1.0) * lg[:, :RET_DK])
            kd_ref[h] = jnp.exp((c - 1.0 - pos) * lg[:, :RET_DK])

    @pl.when(pl.program_id(1) == 0)
    def _():
        state_ref[...] = jnp.zeros_like(state_ref)

    cos = cos_ref[...]
    sin = sin_ref[...]
    for h in range(RET_HEADS):
        qs = slice(h * RET_DK, (h + 1) * RET_DK)
        vs = slice(h * RET_DV, (h + 1) * RET_DV)
        qr = q_ref[:, qs].astype(F32)
        kr = k_ref[:, qs].astype(F32)
        q = qr * cos + pltpu.roll(qr, RET_DK // 2, 1) * sin
        k = (kr * cos + pltpu.roll(kr, RET_DK // 2, 1) * sin) * (RET_DK ** -0.5)
        v = v_ref[:, vs]
        state = state_ref[h]
        chunk_decay = jnp.exp(float(c) * lg_ref[h:h + 1, :])
        scores = _dot_nt(q.astype(BF16), k.astype(BF16)) * intra_ref[h]
        o = _dot(scores.astype(BF16), v) + _dot((q * qd_ref[h]).astype(BF16), state.astype(BF16))
        state_ref[h] = state * chunk_decay + _dot_tn((k * kd_ref[h]).astype(BF16), v)
        o = o * lax.rsqrt(jnp.mean(o * o, axis=-1, keepdims=True) + NORM_EPS)
        y = _silu(rg_ref[:, vs].astype(F32)) * o
        o_ref[:, vs] = (_sigmoid(ga_ref[:, vs].astype(F32)) * y).astype(BF16)


def _retention(proj3, cos_t, sin_t, log_gamma):
    b, s, _ = proj3.shape
    ts = SEQ_TILE
    qk_w = RET_HEADS * RET_DK
    v_w = RET_HEADS * RET_DV
    return pl.pallas_call(
        _ret_kernel,
        grid=(b, s // ts),
        in_specs=[
            pl.BlockSpec((RET_HEADS, RET_DV), lambda i, j: (0, 0)),
            pl.BlockSpec((None, ts, qk_w), lambda i, j: (i, j, COL_RQ // qk_w)),
            pl.BlockSpec((None, ts, qk_w), lambda i, j: (i, j, COL_RK // qk_w)),
            pl.BlockSpec((None, ts, v_w), lambda i, j: (i, j, COL_RV // v_w)),
            pl.BlockSpec((None, ts, v_w), lambda i, j: (i, j, COL_RG // v_w)),
            pl.BlockSpec((None, ts, v_w), lambda i, j: (i, j, COL_GATE_A // v_w)),
            pl.BlockSpec((ts, RET_DK), lambda i, j: (j, 0)),
            pl.BlockSpec((ts, RET_DK), lambda i, j: (j, 0)),
        ],
        out_specs=pl.BlockSpec((None, ts, v_w), lambda i, j: (i, j, 0)),
        out_shape=jax.ShapeDtypeStruct((b, s, v_w), BF16),
        scratch_shapes=[
            pltpu.VMEM((RET_HEADS, RET_DK, RET_DV), F32),
            pltpu.VMEM((RET_HEADS, ts, ts), F32),
            pltpu.VMEM((RET_HEADS, ts, RET_DK), F32),
            pltpu.VMEM((RET_HEADS, ts, RET_DK), F32),
        ],
        compiler_params=_params("arbitrary", "arbitrary"),
        name="retention",
    )(log_gamma, proj3, proj3, proj3, proj3, proj3, cos_t, sin_t)


def _unit_lower_inverse(a, eye, nilpotency):
    p = eye - a
    m = a
    for _ in range((nilpotency - 1).bit_length() - 1):
        m16 = m.astype(BF16)
        m = _dot(m16, m16)
        p = p + _dot(p.astype(BF16), m.astype(BF16))
    return p


def _gdn_kernel(q_ref, k_ref, v_ref, z_ref, gb_ref, gates_ref, alog_ref, dtb_ref, cwq_ref, cwk_ref, cwv_ref,
                nw_ref, o_ref, state_ref, tail_ref):
    ts = SEQ_TILE
    cc = GDN_CHUNK
    qk_w = GDN_HEADS * GDN_DK

    @pl.when(pl.program_id(1) == 0)
    def _():
        state_ref[...] = jnp.zeros_like(state_ref)
        tail_ref[...] = jnp.zeros_like(tail_ref)

    def conv_silu(cur, tail, cw_ref):
        assert CONV_K == 4
        ext = jnp.concatenate([tail, cur], axis=0)
        ext1 = pltpu.roll(ext, 1, 0)
        near = ext * cw_ref[3:4, :] + ext1 * cw_ref[2:3, :]
        far = ext * cw_ref[1:2, :] + ext1 * cw_ref[0:1, :]
        return _silu((near + pltpu.roll(far, 2, 0))[8:])

    row = lax.broadcasted_iota(jnp.int32, (ts, ts), 0)
    col = lax.broadcasted_iota(jnp.int32, (ts, ts), 1)
    same_chunk = (row // cc) == (col // cc)
    causal = jnp.logical_and(same_chunk, row >= col)
    strict = jnp.logical_and(same_chunk, row > col)
    eye = jnp.where(row == col, 1.0, 0.0)
    tri_lower = jnp.where(causal, 1.0, 0.0)
    tri_upper = jnp.where(jnp.logical_and(same_chunk, row <= col), 1.0, 0.0)

    units = [(s, h) for s in range(GDN_BATCH) for h in range(GDN_HEADS)]
    q_all, k_all, v_all, beta_all, gc_col_all, gc_row_all = [], [], [], [], [], []
    for s in range(GDN_BATCH):
        q_raw = q_ref[s].astype(F32)
        k_raw = k_ref[s].astype(F32)
        v_raw = v_ref[s].astype(F32)
        q_all.append(conv_silu(q_raw, tail_ref[s, :, 0:qk_w], cwq_ref))
        k_all.append(conv_silu(k_raw, tail_ref[s, :, qk_w:2 * qk_w], cwk_ref))
        v_all.append(conv_silu(v_raw, tail_ref[s, :, 2 * qk_w:], cwv_ref))
        tail_ref[s, :, 0:qk_w] = q_raw[ts - 8:]
        tail_ref[s, :, qk_w:2 * qk_w] = k_raw[ts - 8:]
        tail_ref[s, :, 2 * qk_w:] = v_raw[ts - 8:]

        gates = gates_ref[s]
        x = gates + dtb_ref[...]
        softplus = jnp.maximum(x, 0.0) + jnp.log1p(jnp.exp(-jnp.abs(x)))
        g_all = -jnp.exp(alog_ref[...]) * softplus
        beta_all.append(_sigmoid(gates))
        gc_col_all.append(_dot(tri_lower, g_all, precision=HIGHEST))
        gc_row_all.append(_dot_tn(g_all, tri_upper, precision=HIGHEST))

    qn, kn, k16, k_beta, gc, decay, beta = {}, {}, {}, {}, {}, {}, {}
    for u in units:
        s, h = u
        qs = slice(h * GDN_DK, (h + 1) * GDN_DK)
        qh = q_all[s][:, qs]
        kh = k_all[s][:, qs]
        qn[u] = qh * lax.rsqrt(jnp.sum(qh * qh, axis=-1, keepdims=True) + L2_EPS) * (GDN_DK ** -0.5)
        kn[u] = kh * lax.rsqrt(jnp.sum(kh * kh, axis=-1, keepdims=True) + L2_EPS)
        gc[u] = gc_col_all[s][:, h:h + 1]
        gcr = gc_row_all[s][h:h + 1, :]
        decay[u] = jnp.exp(jnp.where(causal, gc[u] - gcr, -jnp.inf))
        beta[u] = beta_all[s][:, GDN_HEADS + h:GDN_HEADS + h + 1]
        k_beta[u] = kn[u] * beta[u]
        k16[u] = kn[u].astype(BF16)
    a_mat = {u: jnp.where(strict, _dot_nt(k_beta[u].astype(BF16), k16[u]) * decay[u], 0.0) for u in units}
    p = {u: eye - a_mat[u] for u in units}
    m = a_mat
    for _ in range((cc - 1).bit_length() - 1):
        m16 = {u: m[u].astype(BF16) for u in units}
        m = {u: _dot(m16[u], m16[u]) for u in units}
        p = {u: p[u] + _dot(p[u].astype(BF16), m[u].astype(BF16)) for u in units}
    u_all, lhs_state, lhs_vnew, state_decay = {}, {}, {}, {}
    for u in units:
        s, h = u
        vs = slice(h * GDN_DV, (h + 1) * GDN_DV)
        t16 = p[u].astype(BF16)
        egc = jnp.exp(gc[u])
        u_all[u] = _dot(t16, (v_all[s][:, vs] * beta[u]).astype(BF16))
        w_all = _dot(t16, (k_beta[u] * egc).astype(BF16))
        qk = _dot_nt(qn[u].astype(BF16), k16[u]) * decay[u]
        qg = qn[u] * egc
        kn_t = kn[u].T
        gcr = gc_row_all[s][h:h + 1, :]
        for n in range(ts // cc):
            rs = slice(n * cc, (n + 1) * cc)
            g_last = gc[u][(n + 1) * cc - 1:(n + 1) * cc, :]
            kg_t = kn_t[:, rs] * jnp.exp(g_last - gcr[:, rs])
            lhs_state[u, n] = jnp.concatenate([w_all[rs], qg[rs]], axis=0).astype(BF16)
            lhs_vnew[u, n] = jnp.concatenate([qk[rs, rs], kg_t], axis=0).astype(BF16)
            state_decay[u, n] = jnp.exp(g_last)
    state = {u: state_ref[u[0], u[1]] for u in units}
    outs = {u: [] for u in units}
    for n in range(ts // cc):
        rs = slice(n * cc, (n + 1) * cc)
        for u in units:
            from_state = _dot(lhs_state[u, n], state[u].astype(BF16))
            v_new = (u_all[u][rs] - from_state[:cc]).astype(BF16)
            from_vnew = _dot(lhs_vnew[u, n], v_new)
            outs[u].append(from_state[cc:] + from_vnew[:cc])
            state[u] = state[u] * state_decay[u, n] + from_vnew[cc:]
    for u in units:
        s, h = u
        vs = slice(h * GDN_DV, (h + 1) * GDN_DV)
        state_ref[s, h] = state[u]
        o = jnp.concatenate(outs[u], axis=0)
        o = o * lax.rsqrt(jnp.mean(o * o, axis=-1, keepdims=True) + NORM_EPS) * nw_ref[...]
        o = o * _silu(z_ref[s, :, vs].astype(F32))
        o_ref[s, :, vs] = (_sigmoid(gb_ref[s, :, vs].astype(F32)) * o).astype(BF16)


def _gdn(proj3, gates3, alog_row, dtb_row, conv_w, norm_w):
    b, s, _ = proj3.shape
    ts = SEQ_TILE
    nb = GDN_BATCH
    qk_w = GDN_HEADS * GDN_DK
    v_w = GDN_HEADS * GDN_DV
    return pl.pallas_call(
        _gdn_kernel,
        grid=(b // nb, s // ts),
        in_specs=[
            pl.BlockSpec((nb, ts, qk_w), lambda i, j: (i, j, COL_GQ // qk_w)),
            pl.BlockSpec((nb, ts, qk_w), lambda i, j: (i, j, COL_GK // qk_w)),
            pl.BlockSpec((nb, ts, v_w), lambda i, j: (i, j, COL_GV // v_w)),
            pl.BlockSpec((nb, ts, v_w), lambda i, j: (i, j, COL_GZ // v_w)),
            pl.BlockSpec((nb, ts, v_w), lambda i, j: (i, j, COL_GATE_B // v_w)),
            pl.BlockSpec((nb, ts, LANES), lambda i, j: (i, j, 0)),
            pl.BlockSpec((1, LANES), lambda i, j: (0, 0)),
            pl.BlockSpec((1, LANES), lambda i, j: (0, 0)),
            pl.BlockSpec((CONV_K, qk_w), lambda i, j: (0, 0)),
            pl.BlockSpec((CONV_K, qk_w), lambda i, j: (0, 1)),
            pl.BlockSpec((CONV_K, v_w), lambda i, j: (0, 1)),
            pl.BlockSpec((1, GDN_DV), lambda i, j: (0, 0)),
        ],
        out_specs=pl.BlockSpec((nb, ts, v_w), lambda i, j: (i, j, 0)),
        out_shape=jax.ShapeDtypeStruct((b, s, v_w), BF16),
        scratch_shapes=[
            pltpu.VMEM((nb, GDN_HEADS, GDN_DK, GDN_DV), F32),
            pltpu.VMEM((nb, 8, 2 * qk_w + v_w), F32),
        ],
        compiler_params=_params("arbitrary", "arbitrary"),
        name="gdn",
    )(proj3, proj3, proj3, proj3, proj3, gates3, alog_row, dtb_row, conv_w, conv_w, conv_w,
      norm_w.reshape(1, GDN_DV))


def _route(logits):
    lane = lax.broadcasted_iota(jnp.int32, logits.shape, 1)
    neg = jnp.float32(-jnp.inf)

    def masked_softmax(mask):
        m = jnp.max(jnp.where(mask, logits, neg), axis=-1, keepdims=True)
        e = jnp.where(mask, jnp.exp(jnp.where(mask, logits, m) - m), 0.0)
        return e / jnp.sum(e, axis=-1, keepdims=True)

    def first_argmax(p, mask):
        top = jnp.max(jnp.where(mask, p, -1.0), axis=-1, keepdims=True)
        idx = jnp.min(jnp.where(jnp.logical_and(mask, p == top), lane, LANES), axis=-1, keepdims=True)
        return top, idx

    gmask = lane < N_GROUPS
    g_top, g_idx = first_argmax(masked_softmax(gmask), gmask)
    lo = N_GROUPS + EXPERTS_PER_GROUP * g_idx
    emask = jnp.logical_and(lane >= lo, lane < lo + EXPERTS_PER_GROUP)
    pe = masked_softmax(emask)
    top1, i1 = first_argmax(pe, emask)
    emask2 = jnp.logical_and(emask, lane != i1)
    top2, i2 = first_argmax(pe, emask2)
    denom = top1 + top2
    comb = jnp.where(lane == i1, g_top * (top1 / denom), jnp.where(lane == i2, g_top * (top2 / denom), 0.0))
    return jnp.where(lane == 0, g_idx.astype(F32), comb)


def _outproj_kernel(ya_ref, yb_ref, x_ref, w_ref, gate_ref, nw_ref, shift_ref, scale_ref, wr_ref, wrh_ref, br_ref,
                    h_ref, xf_ref, comb_ref):
    tm = x_ref.shape[0]
    halves = [slice(k * (tm // 2), (k + 1) * (tm // 2)) for k in range(2)]
    hs, xfs, logits = [], [], []
    for rs in halves:
        merged = (ya_ref[rs, :].astype(F32) + yb_ref[rs, :].astype(F32)).astype(BF16)
        hs.append(x_ref[rs, :] + gate_ref[...] * _dot(merged, w_ref[...]))
    for rs, h in zip(halves, hs):
        h_ref[rs, :] = h
        y = h * lax.rsqrt(jnp.mean(h * h, axis=-1, keepdims=True) + NORM_EPS) * nw_ref[...]
        xfs.append(y * (1.0 + scale_ref[...]) + shift_ref[...])
    for rs, xf in zip(halves, xfs):
        xf_hi = xf.astype(BF16)
        xf_ref[rs, :] = xf_hi
        xf_lo = (xf - xf_hi.astype(F32)).astype(BF16)
        both = _dot(xf_hi, wr_ref[...])
        logits.append(both[:, :LANES] + both[:, LANES:] + _dot(xf_lo, wrh_ref[...]) + br_ref[...])
    for rs, lg in zip(halves, logits):
        comb_ref[rs, :] = _route(lg)


def _outproj(ya2, yb2, x2, w_out_b, mod4, norm_w, w_route, b_route, seq, tm=512):
    t, d = x2.shape
    per_b = seq // tm
    row_spec = pl.BlockSpec((tm, d), lambda i: (i, 0))
    wr_hi = w_route.astype(BF16)
    wr_lo = (w_route - wr_hi.astype(F32)).astype(BF16)

    def mod_spec(k):
        return pl.BlockSpec((None, None, 1, d), lambda i: (i // per_b, k, 0, 0))

    return pl.pallas_call(
        _outproj_kernel,
        grid=(t // tm,),
        in_specs=[
            row_spec, row_spec, row_spec,
            pl.BlockSpec((d, d), lambda i: (0, 0)),
            mod_spec(2),
            pl.BlockSpec((1, d), lambda i: (0, 0)),
            mod_spec(3),
            mod_spec(4),
            pl.BlockSpec((d, 2 * LANES), lambda i: (0, 0)),
            pl.BlockSpec((d, LANES), lambda i: (0, 0)),
            pl.BlockSpec((1, LANES), lambda i: (0, 0)),
        ],
        out_specs=[row_spec, row_spec, pl.BlockSpec((tm, LANES), lambda i: (i, 0))],
        out_shape=[
            jax.ShapeDtypeStruct((t, d), F32),
            jax.ShapeDtypeStruct((t, d), BF16),
            jax.ShapeDtypeStruct((t, LANES), F32),
        ],
        compiler_params=_params("arbitrary"),
        name="outproj",
    )(ya2, yb2, x2, w_out_b, mod4, norm_w.reshape(1, d), mod4, mod4,
      jnp.concatenate([wr_hi, wr_lo], axis=1), wr_hi, b_route)


def _moe_kernel(xf_ref, comb_ref, wg_ref, wu_ref, wd_ref, h_ref, gate_ref, nw_ref, o_ref,
                xs_ref, cs_ref, acc_ref, pos_ref, ltri_ref, meta_ref):
    i = pl.program_id(0)
    p = pl.program_id(1)
    tm = MOE_TILE
    steps_per_group = EXPERTS_PER_GROUP // MOE_EXPERTS_PER_STEP
    g = p // steps_per_group

    @pl.when(jnp.logical_and(i == 0, p == 0))
    def _():
        row = lax.broadcasted_iota(jnp.int32, (tm, tm), 0)
        col = lax.broadcasted_iota(jnp.int32, (tm, tm), 1)
        ltri_ref[...] = jnp.where(row >= col, 1.0, 0.0).astype(BF16)
        xs_ref[MOE_SORTED:, :] = jnp.zeros((MOE_ROWS - MOE_SORTED, xs_ref.shape[1]), BF16)
        cs_ref[MOE_SORTED:, :] = jnp.zeros((MOE_ROWS - MOE_SORTED, LANES), F32)
        acc_ref[MOE_SORTED:, :] = jnp.zeros((MOE_ROWS - MOE_SORTED, acc_ref.shape[1]), F32)

    @pl.when(p == 0)
    def _():
        comb = comb_ref[...]
        lane = lax.broadcasted_iota(jnp.int32, (tm, LANES), 1)
        gidx = comb[:, 0:1]
        mine = jnp.logical_and(lane.astype(F32) == gidx, lane < N_GROUPS)
        csum = _dot(ltri_ref[...], jnp.where(mine, 1.0, 0.0).astype(BF16))
        counts = jnp.broadcast_to(csum[tm - 1:tm, :], (8, LANES))
        aligned = jnp.floor((counts + (MOE_ROW_ALIGN - 0.5)) * (1.0 / MOE_ROW_ALIGN)) * MOE_ROW_ALIGN
        lr = lax.broadcasted_iota(jnp.int32, (LANES, LANES), 0)
        lc = lax.broadcasted_iota(jnp.int32, (LANES, LANES), 1)
        seg_start = _dot(aligned, jnp.where(lr < lc, 1.0, 0.0), precision=HIGHEST)
        pos = jnp.sum(jnp.where(mine, seg_start[0:1, :] + csum - 1.0, 0.0), axis=-1, keepdims=True)
        pos_b = jnp.broadcast_to(pos, (tm, LANES))
        pos_ref[...] = pos_b.astype(jnp.int32)
        pos_row = pos_b.T[0:1, :].astype(jnp.int32)
        lane1 = lax.broadcasted_iota(jnp.int32, (8, LANES), 1)
        for k in range(N_GROUPS):
            meta_ref[k] = jnp.sum(jnp.where(lane1 == k, counts, 0.0)[0:1, :]).astype(jnp.int32)
            meta_ref[N_GROUPS + k] = jnp.sum(jnp.where(lane1 == k, seg_start, 0.0)[0:1, :]).astype(jnp.int32)

        c_hi = comb.astype(BF16)
        c_lo = (comb - c_hi.astype(F32)).astype(BF16)
        sub = lax.broadcasted_iota(jnp.int32, (MOE_SORTED, tm), 0)
        perm = jnp.where(pos_row == sub, 1.0, 0.0).astype(BF16)
        xs_ref[0:MOE_SORTED, :] = _dot(perm, xf_ref[...]).astype(BF16)
        cs_ref[0:MOE_SORTED, :] = _dot(perm, c_hi) + _dot(perm, c_lo)
        acc_ref[0:MOE_SORTED, :] = jnp.zeros((MOE_SORTED, acc_ref.shape[1]), F32)

    def expert_piece(start, m):
        rows = pl.ds(pl.multiple_of(start, MOE_ROW_ALIGN), m)
        xs = xs_ref[rows, :]
        cs = cs_ref[rows, :]
        lane_m = lax.broadcasted_iota(jnp.int32, (m, LANES), 1)
        contrib = None
        for j in range(MOE_EXPERTS_PER_STEP):
            e = p * MOE_EXPERTS_PER_STEP + j
            wcol = jnp.sum(jnp.where(lane_m == N_GROUPS + e, cs, 0.0), axis=-1, keepdims=True)
            for f0 in range(0, D_FF_EXPERT, MOE_FF_SLAB):
                fs = slice(f0, f0 + MOE_FF_SLAB)
                act = _silu(_dot(xs, wg_ref[j, :, fs])) * _dot(xs, wu_ref[j, :, fs]) * wcol
                part = _dot(act.astype(BF16), wd_ref[j, fs, :])
                contrib = part if contrib is None else contrib + part
        acc_ref[rows, :] += contrib

    count = meta_ref[g]
    seg = meta_ref[N_GROUPS + g]
    n_full = jnp.maximum(count - 1, 0) // MOE_PIECE

    def full_piece(k, carry):
        expert_piece(seg + k * MOE_PIECE, MOE_PIECE)
        return carry

    lax.fori_loop(0, n_full, full_piece, 0)
    last_start = seg + n_full * MOE_PIECE
    last_rows = count - n_full * MOE_PIECE
    for m in range(MOE_MIN_PIECE, MOE_PIECE + 1, MOE_BUCKET):
        lo = 0 if m == MOE_MIN_PIECE else m - MOE_BUCKET

        @pl.when(jnp.logical_and(last_rows > lo, last_rows <= m))
        def _(m=m):
            expert_piece(last_start, m)

    @pl.when(p == pl.num_programs(1) - 1)
    def _():
        pos = pos_ref[:, 0:1]
        lane_s = lax.broadcasted_iota(jnp.int32, (tm, MOE_SORTED), 1)
        inv = jnp.where(pos == lane_s, 1.0, 0.0).astype(BF16)
        moe = _dot(inv, acc_ref[0:MOE_SORTED, :].astype(BF16))
        h = h_ref[...] + gate_ref[...] * moe
        o_ref[...] = h * lax.rsqrt(jnp.mean(h * h, axis=-1, keepdims=True) + NORM_EPS) * nw_ref[...]


def _moe(xf2, comb, wg, wu, wd, h2, mod4, norm_out_w, seq):
    t, d = xf2.shape
    f = wg.shape[-1]
    tm = MOE_TILE
    per_b = seq // tm
    eps = MOE_EXPERTS_PER_STEP
    row_spec = pl.BlockSpec((tm, d), lambda i, p: (i, 0))
    return pl.pallas_call(
        _moe_kernel,
        grid=(t // tm, N_EXPERTS // eps),
        in_specs=[
            row_spec,
            pl.BlockSpec((tm, LANES), lambda i, p: (i, 0)),
            pl.BlockSpec((eps, d, f), lambda i, p: (p, 0, 0)),
            pl.BlockSpec((eps, d, f), lambda i, p: (p, 0, 0)),
            pl.BlockSpec((eps, f, d), lambda i, p: (p, 0, 0)),
            row_spec,
            pl.BlockSpec((None, None, 1, d), lambda i, p: (i // per_b, 5, 0, 0)),
            pl.BlockSpec((1, d), lambda i, p: (0, 0)),
        ],
        out_specs=row_spec,
        out_shape=jax.ShapeDtypeStruct((t, d), F32),
        scratch_shapes=[
            pltpu.VMEM((MOE_ROWS, d), BF16),
            pltpu.VMEM((MOE_ROWS, LANES), F32),
            pltpu.VMEM((MOE_ROWS, d), F32),
            pltpu.VMEM((tm, LANES), jnp.int32),
            pltpu.VMEM((tm, tm), BF16),
            pltpu.SMEM((2 * N_GROUPS,), jnp.int32),
        ],
        compiler_params=pltpu.CompilerParams(dimension_semantics=("arbitrary", "arbitrary"),
                                             vmem_limit_bytes=MOE_VMEM_LIMIT),
        name="moe",
    )(xf2, comb, wg, wu, wd, h2, mod4, norm_out_w.reshape(1, d))


def _pad_lanes(a):
    return jnp.pad(a, ((0, 0), (0, LANES - a.shape[1])))


def _layer(h3, c, mod_w, mod_b, norm_mix_w, w_in, conv_w, a_log, dt_bias, gdn_norm_w, w_out, norm_ffn_w,
           w_group, b_group, w_router, b_router, w_gate, w_up, w_down, norm_out_w):
    b, s, d = h3.shape
    t = b * s
    x2 = h3.reshape(t, d)

    n_gate_cols = 2 * GDN_HEADS
    small_lo = COL_GATE_A
    w_main = jnp.concatenate([w_in[:, :small_lo], w_in[:, small_lo + n_gate_cols:]], axis=1).astype(BF16)
    w_small = _pad_lanes(w_in[:, small_lo:small_lo + n_gate_cols])
    w_route = _pad_lanes(jnp.concatenate([w_group, w_router], axis=1))
    b_route = _pad_lanes(jnp.concatenate([b_group, b_router.reshape(-1)])[None, :])
    alog_row = _pad_lanes(a_log[None, :])
    dtb_row = _pad_lanes(dt_bias[None, :])
    f = w_gate.shape[-1]
    wg = w_gate.reshape(N_EXPERTS, d, f).astype(BF16)
    wu = w_up.reshape(N_EXPERTS, d, f).astype(BF16)
    wd = w_down.reshape(N_EXPERTS, f, d).astype(BF16)

    half = RET_DK // 2
    inv_freq = 1.0 / (ROPE_BASE ** (jnp.arange(half, dtype=F32) / half))
    ang = jnp.arange(s, dtype=F32)[:, None] * inv_freq[None, :]
    cos_t = jnp.concatenate([jnp.cos(ang), jnp.cos(ang)], axis=1)
    sin_t = jnp.concatenate([-jnp.sin(ang), jnp.sin(ang)], axis=1)
    log_gamma = jnp.log(1.0 - 2.0 ** (-5.0 - jnp.arange(RET_HEADS, dtype=F32)))
    log_gamma = jnp.broadcast_to(log_gamma[:, None], (RET_HEADS, RET_DV))

    mod4 = _mod(c, mod_w, mod_b).reshape(b, N_MOD, 1, d)
    proj, gates = _inproj(x2, mod4, norm_mix_w, w_main, w_small, s)
    proj3 = proj.reshape(b, s, N_MAIN)
    ya = _retention(proj3, cos_t, sin_t, log_gamma)
    yb = _gdn(proj3, gates.reshape(b, s, LANES), alog_row, dtb_row, conv_w, gdn_norm_w)
    h2, xf2, comb = _outproj(ya.reshape(t, d), yb.reshape(t, d), x2, w_out.astype(BF16), mod4, norm_ffn_w,
                             w_route, b_route, s)
    return _moe(xf2, comb, wg, wu, wd, h2, mod4, norm_out_w, s).reshape(b, s, d)


def kernel(x, c, mod_w, mod_b, norm_mix_w, w_in, gdn_conv_w, gdn_a_log, gdn_dt_bias, gdn_norm_w, w_out, norm_ffn_w,
           w_group, b_group, w_router, b_router, w_gate, w_up, w_down, norm_out_w):
    assert mod_w.shape[0] == 1, "one residual layer"
    return _layer(x, c, mod_w[0], mod_b[0], norm_mix_w[0], w_in[0], gdn_conv_w[0], gdn_a_log[0], gdn_dt_bias[0],
                  gdn_norm_w[0], w_out[0], norm_ffn_w[0], w_group[0], b_group[0], w_router[0], b_router[0],
                  w_gate[0], w_up[0], w_down[0], norm_out_w)
```

```python
import jax
import jax.numpy as jnp
from jax import lax
from jax.experimental import pallas as pl
from jax.experimental.pallas import tpu as pltpu

F32 = jnp.float32
BF16 = jnp.bfloat16
HIGHEST = lax.Precision.HIGHEST

RET_HEADS = 4
RET_DK = 128
RET_DV = 256
GDN_HEADS = 4
GDN_DK = 128
GDN_DV = 256
GDN_CHUNK = 64
GDN_BATCH = 2
CONV_K = 4
N_GROUPS = 4
EXPERTS_PER_GROUP = 4
N_EXPERTS = N_GROUPS * EXPERTS_PER_GROUP
D_FF_EXPERT = 512
ROPE_BASE = 10000.0
NORM_EPS = 1e-6
L2_EPS = 1e-6
N_MOD = 6
LANES = 128
SEQ_TILE = 256
VMEM_LIMIT = 48 * 1024 * 1024
MOE_TILE = 1024
MOE_ROW_ALIGN = 16
MOE_SORTED = MOE_TILE + N_GROUPS * MOE_ROW_ALIGN
MOE_PIECE = 512
MOE_BUCKET = 64
MOE_MIN_PIECE = 128
MOE_ROWS = MOE_SORTED + MOE_PIECE
MOE_EXPERTS_PER_STEP = 2
MOE_FF_SLAB = 256
MOE_VMEM_LIMIT = 56 * 1024 * 1024

COL_RQ, COL_RK, COL_RV, COL_RG = 0, 512, 1024, 2048
COL_GQ, COL_GK, COL_GV, COL_GZ = 3072, 3584, 4096, 5120
COL_GATE_A, COL_GATE_B = 6144, 7168
N_MAIN = 8192


def _sigmoid(x):
    return 0.5 * jnp.tanh(0.5 * x) + 0.5


def _silu(x):
    h = 0.5 * x
    return h + h * jnp.tanh(h)


def _dot(a, b, **kw):
    return jnp.dot(a, b, preferred_element_type=F32, **kw)


def _dot_nt(a, b, **kw):
    return lax.dot_general(a, b, (((1,), (1,)), ((), ())), preferred_element_type=F32, **kw)


def _dot_tn(a, b, **kw):
    return lax.dot_general(a, b, (((0,), (0,)), ((), ())), preferred_element_type=F32, **kw)


def _params(*sem):
    return pltpu.CompilerParams(dimension_semantics=sem, vmem_limit_bytes=VMEM_LIMIT)


def _mod_kernel(c_ref, w_ref, b_ref, o_ref):
    a = _silu(c_ref[...])
    o_ref[...] = _dot(a, w_ref[...], precision=HIGHEST) + b_ref[...]


def _mod(c, mod_w, mod_b):
    b, d = c.shape
    n = mod_w.shape[1]
    tn = d
    return pl.pallas_call(
        _mod_kernel,
        grid=(n // tn,),
        in_specs=[
            pl.BlockSpec((b, d), lambda j: (0, 0)),
            pl.BlockSpec((d, tn), lambda j: (0, j)),
            pl.BlockSpec((1, tn), lambda j: (0, j)),
        ],
        out_specs=pl.BlockSpec((b, tn), lambda j: (0, j)),
        out_shape=jax.ShapeDtypeStruct((b, n), F32),
        compiler_params=_params("arbitrary"),
        name="mod",
    )(c, mod_w, mod_b.reshape(1, n))


def _inproj_kernel(x0_ref, shift0_ref, scale0_ref, xn_ref, shiftn_ref, scalen_ref, nw_ref, w_ref, ws_ref,
                   o_ref, og_ref, xb_even_ref, xb_odd_ref):
    i = pl.program_id(0)
    j = pl.program_id(1)
    tm = x0_ref.shape[0]
    slab = tm // pl.num_programs(1)

    def prepare(x, shift, scale):
        y = x * lax.rsqrt(jnp.mean(x * x, axis=-1, keepdims=True) + NORM_EPS) * nw_ref[...]
        return (y * (1.0 + scale) + shift).astype(BF16)

    @pl.when(jnp.logical_and(i == 0, j == 0))
    def _():
        xb_even_ref[...] = prepare(x0_ref[...], shift0_ref[...], scale0_ref[...])

    def step(cur_ref, nxt_ref):
        rows = pl.ds(pl.multiple_of(j * slab, slab), slab)
        nxt_ref[rows, :] = prepare(xn_ref[rows, :], shiftn_ref[...], scalen_ref[...])
        xb = cur_ref[...]
        o_ref[...] = _dot(xb, w_ref[...]).astype(BF16)

        @pl.when(j == 0)
        def _():
            og_ref[...] = _dot(xb, ws_ref[...])

    @pl.when(i % 2 == 0)
    def _():
        step(xb_even_ref, xb_odd_ref)

    @pl.when(i % 2 == 1)
    def _():
        step(xb_odd_ref, xb_even_ref)


def _inproj(x2, mod4, norm_w, w_main, w_small, seq, tm=1024, tn=2048):
    t, d = x2.shape
    n = w_main.shape[1]
    per_b = seq // tm
    last = t // tm - 1

    def nxt(i):
        return jnp.minimum(i + 1, last)

    return pl.pallas_call(
        _inproj_kernel,
        grid=(t // tm, n // tn),
        in_specs=[
            pl.BlockSpec((tm, d), lambda i, j: (0, 0)),
            pl.BlockSpec((None, None, 1, d), lambda i, j: (0, 0, 0, 0)),
            pl.BlockSpec((None, None, 1, d), lambda i, j: (0, 1, 0, 0)),
            pl.BlockSpec((tm, d), lambda i, j: (nxt(i), 0)),
            pl.BlockSpec((None, None, 1, d), lambda i, j: (nxt(i) // per_b, 0, 0, 0)),
            pl.BlockSpec((None, None, 1, d), lambda i, j: (nxt(i) // per_b, 1, 0, 0)),
            pl.BlockSpec((1, d), lambda i, j: (0, 0)),
            pl.BlockSpec((d, tn), lambda i, j: (0, j)),
            pl.BlockSpec((d, LANES), lambda i, j: (0, 0)),
        ],
        out_specs=[
            pl.BlockSpec((tm, tn), lambda i, j: (i, j)),
            pl.BlockSpec((tm, LANES), lambda i, j: (i, 0)),
        ],
        out_shape=[
            jax.ShapeDtypeStruct((t, n), BF16),
            jax.ShapeDtypeStruct((t, LANES), F32),
        ],
        scratch_shapes=[pltpu.VMEM((tm, d), BF16), pltpu.VMEM((tm, d), BF16)],
        compiler_params=_params("arbitrary", "arbitrary"),
        name="inproj",
    )(x2, mod4, mod4, x2, mod4, mod4, norm_w.reshape(1, d), w_main, w_small.astype(BF16))


def _ret_kernel(lg_ref, q_ref, k_ref, v_ref, rg_ref, ga_ref, cos_ref, sin_ref, o_ref,
                state_ref, intra_ref, qd_ref, kd_ref):
    c = SEQ_TILE
    first = jnp.logical_and(pl.program_id(0) == 0, pl.program_id(1) == 0)

    @pl.when(first)
    def _():
        row = lax.broadcasted_iota(jnp.int32, (c, c), 0)
        col = lax.broadcasted_iota(jnp.int32, (c, c), 1)
        rel = (row - col).astype(F32)
        causal = row >= col
        pos = lax.broadcasted_iota(jnp.int32, (c, RET_DK), 0).astype(F32)
        for h in range(RET_HEADS):
            lg = lg_ref[h:h + 1, :]
            intra_ref[h] = jnp.where(causal, jnp.exp(jnp.where(causal, rel, 0.0) * lg), 0.0)
            qd_ref[h] = jnp.exp((pos + 1.0) * lg[:, :RET_DK])
            kd_ref[h] = jnp.exp((c - 1.0 - pos) * lg[:, :RET_DK])

    @pl.when(pl.program_id(1) == 0)
    def _():
        state_ref[...] = jnp.zeros_like(state_ref)

    cos = cos_ref[...]
    sin = sin_ref[...]
    for h in range(RET_HEADS):
        qs = slice(h * RET_DK, (h + 1) * RET_DK)
        vs = slice(h * RET_DV, (h + 1) * RET_DV)
        qr = q_ref[:, qs].astype(F32)
        kr = k_ref[:, qs].astype(F32)
        q = qr * cos + pltpu.roll(qr, RET_DK // 2, 1) * sin
        k = (kr * cos + pltpu.roll(kr, RET_DK // 2, 1) * sin) * (RET_DK ** -0.5)
        v = v_ref[:, vs]
        state = state_ref[h]
        chunk_decay = jnp.exp(float(c) * lg_ref[h:h + 1, :])
        scores = _dot_nt(q.astype(BF16), k.astype(BF16)) * intra_ref[h]
        o = _dot(scores.astype(BF16), v) + _dot((q * qd_ref[h]).astype(BF16), state.astype(BF16))
        state_ref[h] = state * chunk_decay + _dot_tn((k * kd_ref[h]).astype(BF16), v)
        o = o * lax.rsqrt(jnp.mean(o * o, axis=-1, keepdims=True) + NORM_EPS)
        y = _silu(rg_ref[:, vs].astype(F32)) * o
        o_ref[:, vs] = (_sigmoid(ga_ref[:, vs].astype(F32)) * y).astype(BF16)


def _retention(proj3, cos_t, sin_t, log_gamma):
    b, s, _ = proj3.shape
    ts = SEQ_TILE
    qk_w = RET_HEADS * RET_DK
    v_w = RET_HEADS * RET_DV
    return pl.pallas_call(
        _ret_kernel,
        grid=(b, s // ts),
        in_specs=[
            pl.BlockSpec((RET_HEADS, RET_DV), lambda i, j: (0, 0)),
            pl.BlockSpec((None, ts, qk_w), lambda i, j: (i, j, COL_RQ // qk_w)),
            pl.BlockSpec((None, ts, qk_w), lambda i, j: (i, j, COL_RK // qk_w)),
            pl.BlockSpec((None, ts, v_w), lambda i, j: (i, j, COL_RV // v_w)),
            pl.BlockSpec((None, ts, v_w), lambda i, j: (i, j, COL_RG // v_w)),
            pl.BlockSpec((None, ts, v_w), lambda i, j: (i, j, COL_GATE_A // v_w)),
            pl.BlockSpec((ts, RET_DK), lambda i, j: (j, 0)),
            pl.BlockSpec((ts, RET_DK), lambda i, j: (j, 0)),
        ],
        out_specs=pl.BlockSpec((None, ts, v_w), lambda i, j: (i, j, 0)),
        out_shape=jax.ShapeDtypeStruct((b, s, v_w), BF16),
        scratch_shapes=[
            pltpu.VMEM((RET_HEADS, RET_DK, RET_DV), F32),
            pltpu.VMEM((RET_HEADS, ts, ts), F32),
            pltpu.VMEM((RET_HEADS, ts, RET_DK), F32),
            pltpu.VMEM((RET_HEADS, ts, RET_DK), F32),
        ],
        compiler_params=_params("arbitrary", "arbitrary"),
        name="retention",
    )(log_gamma, proj3, proj3, proj3, proj3, proj3, cos_t, sin_t)


def _gdn_kernel(q_ref, k_ref, v_ref, z_ref, gb_ref, gates_ref, alog_ref, dtb_ref, cwq_ref, cwk_ref, cwv_ref,
                nw_ref, o_ref, state_ref, tail_ref):
    ts = SEQ_TILE
    cc = GDN_CHUNK
    qk_w = GDN_HEADS * GDN_DK

    @pl.when(pl.program_id(1) == 0)
    def _():
        state_ref[...] = jnp.zeros_like(state_ref)
        tail_ref[...] = jnp.zeros_like(tail_ref)

    def conv_silu(cur, tail, cw_ref):
        assert CONV_K == 4
        ext = jnp.concatenate([tail, cur], axis=0)
        ext1 = pltpu.roll(ext, 1, 0)
        near = ext * cw_ref[3:4, :] + ext1 * cw_ref[2:3, :]
        far = ext * cw_ref[1:2, :] + ext1 * cw_ref[0:1, :]
        return _silu((near + pltpu.roll(far, 2, 0))[8:])

    row = lax.broadcasted_iota(jnp.int32, (ts, ts), 0)
    col = lax.broadcasted_iota(jnp.int32, (ts, ts), 1)
    same_chunk = (row // cc) == (col // cc)
    causal = jnp.logical_and(same_chunk, row >= col)
    strict = jnp.logical_and(same_chunk, row > col)
    eye = jnp.where(row == col, 1.0, 0.0)
    tri_lower = jnp.where(causal, 1.0, 0.0)
    tri_upper = jnp.where(jnp.logical_and(same_chunk, row <= col), 1.0, 0.0)

    units = [(s, h) for s in range(GDN_BATCH) for h in range(GDN_HEADS)]
    q_all, k_all, v_all, beta_all, gc_col_all, gc_row_all = [], [], [], [], [], []
    for s in range(GDN_BATCH):
        q_raw = q_ref[s].astype(F32)
        k_raw = k_ref[s].astype(F32)
        v_raw = v_ref[s].astype(F32)
        q_all.append(conv_silu(q_raw, tail_ref[s, :, 0:qk_w], cwq_ref))
        k_all.append(conv_silu(k_raw, tail_ref[s, :, qk_w:2 * qk_w], cwk_ref))
        v_all.append(conv_silu(v_raw, tail_ref[s, :, 2 * qk_w:], cwv_ref))
        tail_ref[s, :, 0:qk_w] = q_raw[ts - 8:]
        tail_ref[s, :, qk_w:2 * qk_w] = k_raw[ts - 8:]
        tail_ref[s, :, 2 * qk_w:] = v_raw[ts - 8:]

        gates = gates_ref[s]
        x = gates + dtb_ref[...]
        softplus = jnp.maximum(x, 0.0) + jnp.log1p(jnp.exp(-jnp.abs(x)))
        g_all = -jnp.exp(alog_ref[...]) * softplus
        beta_all.append(_sigmoid(gates))
        gc_col_all.append(_dot(tri_lower, g_all, precision=HIGHEST))
        gc_row_all.append(_dot_tn(g_all, tri_upper, precision=HIGHEST))

    qn, kn, k16, k_beta, gc, decay, beta = {}, {}, {}, {}, {}, {}, {}
    for u in units:
        s, h = u
        qs = slice(h * GDN_DK, (h + 1) * GDN_DK)
        qh = q_all[s][:, qs]
        kh = k_all[s][:, qs]
        qn[u] = qh * lax.rsqrt(jnp.sum(qh * qh, axis=-1, keepdims=True) + L2_EPS) * (GDN_DK ** -0.5)
        kn[u] = kh * lax.rsqrt(jnp.sum(kh * kh, axis=-1, keepdims=True) + L2_EPS)
        gc[u] = gc_col_all[s][:, h:h + 1]
        gcr = gc_row_all[s][h:h + 1, :]
        decay[u] = jnp.exp(jnp.where(causal, gc[u] - gcr, -jnp.inf))
        beta[u] = beta_all[s][:, GDN_HEADS + h:GDN_HEADS + h + 1]
        k_beta[u] = kn[u] * beta[u]
        k16[u] = kn[u].astype(BF16)
    a_mat = {u: jnp.where(strict, _dot_nt(k_beta[u].astype(BF16), k16[u]) * decay[u], 0.0) for u in units}
    p = {u: eye - a_mat[u] for u in units}
    m16 = {u: a_mat[u].astype(BF16) for u in units}
    for _ in range((cc - 1).bit_length() - 1):
        m16 = {u: _dot(m16[u], m16[u]).astype(BF16) for u in units}
        p = {u: p[u] + _dot(p[u].astype(BF16), m16[u]) for u in units}
    u_all, lhs_state, lhs_vnew, state_decay = {}, {}, {}, {}
    for u in units:
        s, h = u
        vs = slice(h * GDN_DV, (h + 1) * GDN_DV)
        t16 = p[u].astype(BF16)
        egc = jnp.exp(gc[u])
        u_all[u] = _dot(t16, (v_all[s][:, vs] * beta[u]).astype(BF16))
        w_all = _dot(t16, (k_beta[u] * egc).astype(BF16))
        qk = _dot_nt(qn[u].astype(BF16), k16[u]) * decay[u]
        qg = qn[u] * egc
        kn_t = kn[u].T
        gcr = gc_row_all[s][h:h + 1, :]
        for n in range(ts // cc):
            rs = slice(n * cc, (n + 1) * cc)
            g_last = gc[u][(n + 1) * cc - 1:(n + 1) * cc, :]
            kg_t = kn_t[:, rs] * jnp.exp(g_last - gcr[:, rs])
            lhs_state[u, n] = jnp.concatenate([w_all[rs], qg[rs]], axis=0).astype(BF16)
            lhs_vnew[u, n] = jnp.concatenate([qk[rs, rs], kg_t], axis=0).astype(BF16)
            state_decay[u, n] = jnp.exp(g_last)
    state = {u: state_ref[u[0], u[1]] for u in units}
    outs = {u: [] for u in units}
    for n in range(ts // cc):
        rs = slice(n * cc, (n + 1) * cc)
        for u in units:
            from_state = _dot(lhs_state[u, n], state[u].astype(BF16))
            v_new = (u_all[u][rs] - from_state[:cc]).astype(BF16)
            from_vnew = _dot(lhs_vnew[u, n], v_new)
            outs[u].append(from_state[cc:] + from_vnew[:cc])
            state[u] = state[u] * state_decay[u, n] + from_vnew[cc:]
    for u in units:
        s, h = u
        vs = slice(h * GDN_DV, (h + 1) * GDN_DV)
        state_ref[s, h] = state[u]
        o = jnp.concatenate(outs[u], axis=0)
        o = o * lax.rsqrt(jnp.mean(o * o, axis=-1, keepdims=True) + NORM_EPS) * nw_ref[...]
        o = o * _silu(z_ref[s, :, vs].astype(F32))
        o_ref[s, :, vs] = (_sigmoid(gb_ref[s, :, vs].astype(F32)) * o).astype(BF16)


def _gdn(proj3, gates3, alog_row, dtb_row, conv_w, norm_w):
    b, s, _ = proj3.shape
    ts = SEQ_TILE
    nb = GDN_BATCH
    assert b % nb == 0
    qk_w = GDN_HEADS * GDN_DK
    v_w = GDN_HEADS * GDN_DV
    return pl.pallas_call(
        _gdn_kernel,
        grid=(b // nb, s // ts),
        in_specs=[
            pl.BlockSpec((nb, ts, qk_w), lambda i, j: (i, j, COL_GQ // qk_w)),
            pl.BlockSpec((nb, ts, qk_w), lambda i, j: (i, j, COL_GK // qk_w)),
            pl.BlockSpec((nb, ts, v_w), lambda i, j: (i, j, COL_GV // v_w)),
            pl.BlockSpec((nb, ts, v_w), lambda i, j: (i, j, COL_GZ // v_w)),
            pl.BlockSpec((nb, ts, v_w), lambda i, j: (i, j, COL_GATE_B // v_w)),
            pl.BlockSpec((nb, ts, LANES), lambda i, j: (i, j, 0)),
            pl.BlockSpec((1, LANES), lambda i, j: (0, 0)),
            pl.BlockSpec((1, LANES), lambda i, j: (0, 0)),
            pl.BlockSpec((CONV_K, qk_w), lambda i, j: (0, 0)),
            pl.BlockSpec((CONV_K, qk_w), lambda i, j: (0, 1)),
            pl.BlockSpec((CONV_K, v_w), lambda i, j: (0, 1)),
            pl.BlockSpec((1, GDN_DV), lambda i, j: (0, 0)),
        ],
        out_specs=pl.BlockSpec((nb, ts, v_w), lambda i, j: (i, j, 0)),
        out_shape=jax.ShapeDtypeStruct((b, s, v_w), BF16),
        scratch_shapes=[
            pltpu.VMEM((nb, GDN_HEADS, GDN_DK, GDN_DV), F32),
            pltpu.VMEM((nb, 8, 2 * qk_w + v_w), F32),
        ],
        compiler_params=_params("arbitrary", "arbitrary"),
        name="gdn",
    )(proj3, proj3, proj3, proj3, proj3, gates3, alog_row, dtb_row, conv_w, conv_w, conv_w,
      norm_w.reshape(1, GDN_DV))


def _route(logits):
    lane = lax.broadcasted_iota(jnp.int32, logits.shape, 1)
    neg = jnp.float32(-jnp.inf)

    def first_max(v):
        top = jnp.max(v, axis=-1, keepdims=True)
        return top, jnp.min(jnp.where(v == top, lane, LANES), axis=-1, keepdims=True)

    gl = jnp.where(lane < N_GROUPS, logits, neg)
    g_max, g_idx = first_max(gl)
    g_top = 1.0 / jnp.sum(jnp.exp(gl - g_max), axis=-1, keepdims=True)
    lo = N_GROUPS + EXPERTS_PER_GROUP * g_idx
    el = jnp.where(jnp.logical_and(lane >= lo, lane < lo + EXPERTS_PER_GROUP), logits, neg)
    top1, i1 = first_max(el)
    top2, i2 = first_max(jnp.where(lane == i1, neg, el))
    gap = jnp.exp(top2 - top1)
    w1 = 1.0 / (1.0 + gap)
    comb = jnp.where(lane == i1, g_top * w1, jnp.where(lane == i2, g_top * (gap * w1), 0.0))
    return jnp.where(lane == 0, g_idx.astype(F32), comb)


def _outproj_kernel(ya_ref, yb_ref, x_ref, w_ref, gate_ref, nw_ref, shift_ref, scale_ref, wr_ref, wrh_ref, br_ref,
                    h_ref, xf_ref, comb_ref):
    tm = x_ref.shape[0]
    halves = [slice(k * (tm // 2), (k + 1) * (tm // 2)) for k in range(2)]
    hs, xfs, logits = [], [], []
    for rs in halves:
        merged = (ya_ref[rs, :].astype(F32) + yb_ref[rs, :].astype(F32)).astype(BF16)
        hs.append(x_ref[rs, :] + gate_ref[...] * _dot(merged, w_ref[...]))
    for rs, h in zip(halves, hs):
        h_ref[rs, :] = h
        y = h * lax.rsqrt(jnp.mean(h * h, axis=-1, keepdims=True) + NORM_EPS) * nw_ref[...]
        xfs.append(y * (1.0 + scale_ref[...]) + shift_ref[...])
    for rs, xf in zip(halves, xfs):
        xf_hi = xf.astype(BF16)
        xf_ref[rs, :] = xf_hi
        xf_lo = (xf - xf_hi.astype(F32)).astype(BF16)
        both = _dot(xf_hi, wr_ref[...])
        logits.append(both[:, :LANES] + both[:, LANES:] + _dot(xf_lo, wrh_ref[...]) + br_ref[...])
    for rs, lg in zip(halves, logits):
        comb_ref[rs, :] = _route(lg)


def _outproj(ya2, yb2, x2, w_out_b, mod4, norm_w, w_route, b_route, seq, tm=512):
    t, d = x2.shape
    per_b = seq // tm
    row_spec = pl.BlockSpec((tm, d), lambda i: (i, 0))
    wr_hi = w_route.astype(BF16)
    wr_lo = (w_route - wr_hi.astype(F32)).astype(BF16)

    def mod_spec(k):
        return pl.BlockSpec((None, None, 1, d), lambda i: (i // per_b, k, 0, 0))

    return pl.pallas_call(
        _outproj_kernel,
        grid=(t // tm,),
        in_specs=[
            row_spec, row_spec, row_spec,
            pl.BlockSpec((d, d), lambda i: (0, 0)),
            mod_spec(2),
            pl.BlockSpec((1, d), lambda i: (0, 0)),
            mod_spec(3),
            mod_spec(4),
            pl.BlockSpec((d, 2 * LANES), lambda i: (0, 0)),
            pl.BlockSpec((d, LANES), lambda i: (0, 0)),
            pl.BlockSpec((1, LANES), lambda i: (0, 0)),
        ],
        out_specs=[row_spec, row_spec, pl.BlockSpec((tm, LANES), lambda i: (i, 0))],
        out_shape=[
            jax.ShapeDtypeStruct((t, d), F32),
            jax.ShapeDtypeStruct((t, d), BF16),
            jax.ShapeDtypeStruct((t, LANES), F32),
        ],
        compiler_params=_params("arbitrary"),
        name="outproj",
    )(ya2, yb2, x2, w_out_b, mod4, norm_w.reshape(1, d), mod4, mod4,
      jnp.concatenate([wr_hi, wr_lo], axis=1), wr_hi, b_route)


def _moe_kernel(xf_ref, comb_ref, wg_ref, wu_ref, wd_ref, h_ref, gate_ref, nw_ref, o_ref,
                xs_ref, cs_ref, acc_ref, pos_ref, ltri_ref, meta_ref):
    i = pl.program_id(0)
    p = pl.program_id(1)
    tm = MOE_TILE
    steps_per_group = EXPERTS_PER_GROUP // MOE_EXPERTS_PER_STEP
    g = p // steps_per_group

    @pl.when(jnp.logical_and(i == 0, p == 0))
    def _():
        row = lax.broadcasted_iota(jnp.int32, (tm, tm), 0)
        col = lax.broadcasted_iota(jnp.int32, (tm, tm), 1)
        ltri_ref[...] = jnp.where(row >= col, 1.0, 0.0).astype(BF16)
        xs_ref[MOE_SORTED:, :] = jnp.zeros((MOE_ROWS - MOE_SORTED, xs_ref.shape[1]), BF16)
        cs_ref[MOE_SORTED:, :] = jnp.zeros((MOE_ROWS - MOE_SORTED, LANES), F32)
        acc_ref[MOE_SORTED:, :] = jnp.zeros((MOE_ROWS - MOE_SORTED, acc_ref.shape[1]), F32)

    @pl.when(p == 0)
    def _():
        comb = comb_ref[...]
        lane = lax.broadcasted_iota(jnp.int32, (tm, LANES), 1)
        gidx = comb[:, 0:1]
        mine = jnp.logical_and(lane.astype(F32) == gidx, lane < N_GROUPS)
        csum = _dot(ltri_ref[...], jnp.where(mine, 1.0, 0.0).astype(BF16))
        counts = jnp.broadcast_to(csum[tm - 1:tm, :], (8, LANES))
        aligned = jnp.floor((counts + (MOE_ROW_ALIGN - 0.5)) * (1.0 / MOE_ROW_ALIGN)) * MOE_ROW_ALIGN
        lr = lax.broadcasted_iota(jnp.int32, (LANES, LANES), 0)
        lc = lax.broadcasted_iota(jnp.int32, (LANES, LANES), 1)
        seg_start = _dot(aligned, jnp.where(lr < lc, 1.0, 0.0), precision=HIGHEST)
        pos = jnp.sum(jnp.where(mine, seg_start[0:1, :] + csum - 1.0, 0.0), axis=-1, keepdims=True)
        pos_b = jnp.broadcast_to(pos, (tm, LANES))
        pos_ref[...] = pos_b.astype(jnp.int32)
        pos_row = pos_b.T[0:1, :].astype(jnp.int32)
        lane1 = lax.broadcasted_iota(jnp.int32, (8, LANES), 1)
        for k in range(N_GROUPS):
            meta_ref[k] = jnp.sum(jnp.where(lane1 == k, counts, 0.0)[0:1, :]).astype(jnp.int32)
            meta_ref[N_GROUPS + k] = jnp.sum(jnp.where(lane1 == k, seg_start, 0.0)[0:1, :]).astype(jnp.int32)

        c_hi = comb.astype(BF16)
        c_lo = (comb - c_hi.astype(F32)).astype(BF16)
        sub = lax.broadcasted_iota(jnp.int32, (MOE_SORTED, tm), 0)
        perm = jnp.where(pos_row == sub, 1.0, 0.0).astype(BF16)
        xs_ref[0:MOE_SORTED, :] = _dot(perm, xf_ref[...]).astype(BF16)
        cs_ref[0:MOE_SORTED, :] = _dot(perm, c_hi) + _dot(perm, c_lo)
        acc_ref[0:MOE_SORTED, :] = jnp.zeros((MOE_SORTED, acc_ref.shape[1]), F32)

    def expert_piece(start, m):
        rows = pl.ds(pl.multiple_of(start, MOE_ROW_ALIGN), m)
        xs = xs_ref[rows, :]
        cs = cs_ref[rows, :]
        lane_m = lax.broadcasted_iota(jnp.int32, (m, LANES), 1)
        contrib = None
        for j in range(MOE_EXPERTS_PER_STEP):
            e = p * MOE_EXPERTS_PER_STEP + j
            wcol = jnp.sum(jnp.where(lane_m == N_GROUPS + e, cs, 0.0), axis=-1, keepdims=True)
            for f0 in range(0, D_FF_EXPERT, MOE_FF_SLAB):
                fs = slice(f0, f0 + MOE_FF_SLAB)
                act = _silu(_dot(xs, wg_ref[j, :, fs])) * _dot(xs, wu_ref[j, :, fs]) * wcol
                part = _dot(act.astype(BF16), wd_ref[j, fs, :])
                contrib = part if contrib is None else contrib + part
        acc_ref[rows, :] += contrib

    count = meta_ref[g]
    seg = meta_ref[N_GROUPS + g]
    n_full = jnp.maximum(count - 1, 0) // MOE_PIECE

    def full_piece(k, carry):
        expert_piece(seg + k * MOE_PIECE, MOE_PIECE)
        return carry

    lax.fori_loop(0, n_full, full_piece, 0)
    last_start = seg + n_full * MOE_PIECE
    last_rows = count - n_full * MOE_PIECE
    for m in range(MOE_MIN_PIECE, MOE_PIECE + 1, MOE_BUCKET):
        lo = 0 if m == MOE_MIN_PIECE else m - MOE_BUCKET

        @pl.when(jnp.logical_and(last_rows > lo, last_rows <= m))
        def _(m=m):
            expert_piece(last_start, m)

    @pl.when(p == pl.num_programs(1) - 1)
    def _():
        pos = pos_ref[:, 0:1]
        lane_s = lax.broadcasted_iota(jnp.int32, (tm, MOE_SORTED), 1)
        inv = jnp.where(pos == lane_s, 1.0, 0.0).astype(BF16)
        moe = _dot(inv, acc_ref[0:MOE_SORTED, :].astype(BF16))
        h = h_ref[...] + gate_ref[...] * moe
        o_ref[...] = h * lax.rsqrt(jnp.mean(h * h, axis=-1, keepdims=True) + NORM_EPS) * nw_ref[...]


def _moe(xf2, comb, wg, wu, wd, h2, mod4, norm_out_w, seq):
    t, d = xf2.shape
    f = wg.shape[-1]
    tm = MOE_TILE
    per_b = seq // tm
    eps = MOE_EXPERTS_PER_STEP
    row_spec = pl.BlockSpec((tm, d), lambda i, p: (i, 0))
    return pl.pallas_call(
        _moe_kernel,
        grid=(t // tm, N_EXPERTS // eps),
        in_specs=[
            row_spec,
            pl.BlockSpec((tm, LANES), lambda i, p: (i, 0)),
            pl.BlockSpec((eps, d, f), lambda i, p: (p, 0, 0)),
            pl.BlockSpec((eps, d, f), lambda i, p: (p, 0, 0)),
            pl.BlockSpec((eps, f, d), lambda i, p: (p, 0, 0)),
            row_spec,
            pl.BlockSpec((None, None, 1, d), lambda i, p: (i // per_b, 5, 0, 0)),
            pl.BlockSpec((1, d), lambda i, p: (0, 0)),
        ],
        out_specs=row_spec,
        out_shape=jax.ShapeDtypeStruct((t, d), F32),
        scratch_shapes=[
            pltpu.VMEM((MOE_ROWS, d), BF16),
            pltpu.VMEM((MOE_ROWS, LANES), F32),
            pltpu.VMEM((MOE_ROWS, d), F32),
            pltpu.VMEM((tm, LANES), jnp.int32),
            pltpu.VMEM((tm, tm), BF16),
            pltpu.SMEM((2 * N_GROUPS,), jnp.int32),
        ],
        compiler_params=pltpu.CompilerParams(dimension_semantics=("arbitrary", "arbitrary"),
                                             vmem_limit_bytes=MOE_VMEM_LIMIT),
        name="moe",
    )(xf2, comb, wg, wu, wd, h2, mod4, norm_out_w.reshape(1, d))


def _pad_lanes(a):
    return jnp.pad(a, ((0, 0), (0, LANES - a.shape[1])))


def _layer(h3, c, mod_w, mod_b, norm_mix_w, w_in, conv_w, a_log, dt_bias, gdn_norm_w, w_out, norm_ffn_w,
           w_group, b_group, w_router, b_router, w_gate, w_up, w_down, norm_out_w):
    b, s, d = h3.shape
    t = b * s
    x2 = h3.reshape(t, d)

    n_gate_cols = 2 * GDN_HEADS
    small_lo = COL_GATE_A
    w_main = jnp.concatenate([w_in[:, :small_lo], w_in[:, small_lo + n_gate_cols:]], axis=1).astype(BF16)
    w_small = _pad_lanes(w_in[:, small_lo:small_lo + n_gate_cols])
    w_route = _pad_lanes(jnp.concatenate([w_group, w_router], axis=1))
    b_route = _pad_lanes(jnp.concatenate([b_group, b_router.reshape(-1)])[None, :])
    alog_row = _pad_lanes(a_log[None, :])
    dtb_row = _pad_lanes(dt_bias[None, :])
    f = w_gate.shape[-1]
    wg = w_gate.reshape(N_EXPERTS, d, f).astype(BF16)
    wu = w_up.reshape(N_EXPERTS, d, f).astype(BF16)
    wd = w_down.reshape(N_EXPERTS, f, d).astype(BF16)

    half = RET_DK // 2
    inv_freq = 1.0 / (ROPE_BASE ** (jnp.arange(half, dtype=F32) / half))
    ang = jnp.arange(s, dtype=F32)[:, None] * inv_freq[None, :]
    cos_t = jnp.concatenate([jnp.cos(ang), jnp.cos(ang)], axis=1)
    sin_t = jnp.concatenate([-jnp.sin(ang), jnp.sin(ang)], axis=1)
    log_gamma = jnp.log(1.0 - 2.0 ** (-5.0 - jnp.arange(RET_HEADS, dtype=F32)))
    log_gamma = jnp.broadcast_to(log_gamma[:, None], (RET_HEADS, RET_DV))

    mod4 = _mod(c, mod_w, mod_b).reshape(b, N_MOD, 1, d)
    proj, gates = _inproj(x2, mod4, norm_mix_w, w_main, w_small, s)
    proj3 = proj.reshape(b, s, N_MAIN)
    ya = _retention(proj3, cos_t, sin_t, log_gamma)
    yb = _gdn(proj3, gates.reshape(b, s, LANES), alog_row, dtb_row, conv_w, gdn_norm_w)
    h2, xf2, comb = _outproj(ya.reshape(t, d), yb.reshape(t, d), x2, w_out.astype(BF16), mod4, norm_ffn_w,
                             w_route, b_route, s)
    return _moe(xf2, comb, wg, wu, wd, h2, mod4, norm_out_w, s).reshape(b, s, d)


def kernel(x, c, mod_w, mod_b, norm_mix_w, w_in, gdn_conv_w, gdn_a_log, gdn_dt_bias, gdn_norm_w, w_out, norm_ffn_w,
           w_group, b_group, w_router, b_router, w_gate, w_up, w_down, norm_out_w):
    assert mod_w.shape[0] == 1, "one residual layer"
    return _layer(x, c, mod_w[0], mod_b[0], norm_mix_w[0], w_in[0], gdn_conv_w[0], gdn_a_log[0], gdn_dt_bias[0],
                  gdn_norm_w[0], w_out[0], norm_ffn_w[0], w_group[0], b_group[0], w_router[0], b_router[0],
                  w_gate[0], w_up[0], w_down[0], norm_out_w)
```

```python
import jax
import jax.numpy as jnp
from jax import lax
from jax.experimental import pallas as pl
from jax.experimental.pallas import tpu as pltpu

F32 = jnp.float32
BF16 = jnp.bfloat16
HIGHEST = lax.Precision.HIGHEST

RET_HEADS = 4
RET_DK = 128
RET_DV = 256
GDN_HEADS = 4
GDN_DK = 128
GDN_DV = 256
GDN_CHUNK = 64
GDN_BATCH = 2
CONV_K = 4
N_GROUPS = 4
EXPERTS_PER_GROUP = 4
N_EXPERTS = N_GROUPS * EXPERTS_PER_GROUP
D_FF_EXPERT = 512
ROPE_BASE = 10000.0
NORM_EPS = 1e-6
L2_EPS = 1e-6
N_MOD = 6
LANES = 128
SUBLANES = 8
SEQ_TILE = 256
VMEM_LIMIT = 48 * 1024 * 1024
MOE_TILE = 1024
MOE_ROW_ALIGN = 16
MOE_SORTED = MOE_TILE + N_GROUPS * MOE_ROW_ALIGN
MOE_PIECE = 512
MOE_BUCKET = 64
MOE_MIN_PIECE = 128
MOE_ROWS = MOE_SORTED + MOE_PIECE
MOE_EXPERTS_PER_STEP = 2
MOE_FF_SLAB = 256
MOE_VMEM_LIMIT = 56 * 1024 * 1024
INPROJ_VMEM_LIMIT = 56 * 1024 * 1024

COL_RQ, COL_RK, COL_RV, COL_RG = 0, 512, 1024, 2048
COL_GQ, COL_GK, COL_GV, COL_GZ = 3072, 3584, 4096, 5120
COL_GATE_A, COL_GATE_B = 6144, 7168
N_MAIN = 8192


def _sigmoid(x):
    return 0.5 * jnp.tanh(0.5 * x) + 0.5


def _silu(x):
    h = 0.5 * x
    return h + h * jnp.tanh(h)


def _dot(a, b, **kw):
    return jnp.dot(a, b, preferred_element_type=F32, **kw)


def _dot_nt(a, b, **kw):
    return lax.dot_general(a, b, (((1,), (1,)), ((), ())), preferred_element_type=F32, **kw)


def _dot_tn(a, b, **kw):
    return lax.dot_general(a, b, (((0,), (0,)), ((), ())), preferred_element_type=F32, **kw)


def _params(*sem):
    return pltpu.CompilerParams(dimension_semantics=sem, vmem_limit_bytes=VMEM_LIMIT)


def _mod_kernel(c_ref, w_ref, b_ref, o_ref):
    a = _silu(c_ref[...])
    o_ref[...] = _dot(a, w_ref[...], precision=HIGHEST) + b_ref[...]


def _mod(c, mod_w, mod_b):
    b, d = c.shape
    n = mod_w.shape[1]
    tn = d
    return pl.pallas_call(
        _mod_kernel,
        grid=(n // tn,),
        in_specs=[
            pl.BlockSpec((b, d), lambda j: (0, 0)),
            pl.BlockSpec((d, tn), lambda j: (0, j)),
            pl.BlockSpec((1, tn), lambda j: (0, j)),
        ],
        out_specs=pl.BlockSpec((b, tn), lambda j: (0, j)),
        out_shape=jax.ShapeDtypeStruct((b, n), F32),
        compiler_params=_params("arbitrary"),
        name="mod",
    )(c, mod_w, mod_b.reshape(1, n))


def _inproj_kernel(x0_ref, shift0_ref, scale0_ref, xn_ref, shiftn_ref, scalen_ref, nw_ref, w_ref, ws_ref,
                   o_ref, og_ref, xb_even_ref, xb_odd_ref):
    i = pl.program_id(0)
    j = pl.program_id(1)
    tm = x0_ref.shape[0]
    slab = tm // pl.num_programs(1)

    def prepare(x, shift, scale):
        y = x * lax.rsqrt(jnp.mean(x * x, axis=-1, keepdims=True) + NORM_EPS) * nw_ref[...]
        return (y * (1.0 + scale) + shift).astype(BF16)

    @pl.when(jnp.logical_and(i == 0, j == 0))
    def _():
        xb_even_ref[...] = prepare(x0_ref[...], shift0_ref[...], scale0_ref[...])

    def step(cur_ref, nxt_ref):
        rows = pl.ds(pl.multiple_of(j * slab, slab), slab)
        nxt_ref[rows, :] = prepare(xn_ref[rows, :], shiftn_ref[...], scalen_ref[...])
        xb = cur_ref[...]
        o_ref[...] = _dot(xb, w_ref[...]).astype(BF16)

        @pl.when(j == 0)
        def _():
            og_ref[...] = _dot(xb, ws_ref[...])

    @pl.when(i % 2 == 0)
    def _():
        step(xb_even_ref, xb_odd_ref)

    @pl.when(i % 2 == 1)
    def _():
        step(xb_odd_ref, xb_even_ref)


def _inproj(x2, mod4, norm_w, w_main, w_small, seq, tm=1024, tn=4096):
    t, d = x2.shape
    n = w_main.shape[1]
    per_b = seq // tm
    last = t // tm - 1

    def nxt(i):
        return jnp.minimum(i + 1, last)

    return pl.pallas_call(
        _inproj_kernel,
        grid=(t // tm, n // tn),
        in_specs=[
            pl.BlockSpec((tm, d), lambda i, j: (0, 0), pipeline_mode=pl.Buffered(1)),
            pl.BlockSpec((None, None, 1, d), lambda i, j: (0, 0, 0, 0)),
            pl.BlockSpec((None, None, 1, d), lambda i, j: (0, 1, 0, 0)),
            pl.BlockSpec((tm, d), lambda i, j: (nxt(i), 0)),
            pl.BlockSpec((None, None, 1, d), lambda i, j: (nxt(i) // per_b, 0, 0, 0)),
            pl.BlockSpec((None, None, 1, d), lambda i, j: (nxt(i) // per_b, 1, 0, 0)),
            pl.BlockSpec((1, d), lambda i, j: (0, 0)),
            pl.BlockSpec((d, tn), lambda i, j: (0, j)),
            pl.BlockSpec((d, LANES), lambda i, j: (0, 0)),
        ],
        out_specs=[
            pl.BlockSpec((tm, tn), lambda i, j: (i, j)),
            pl.BlockSpec((tm, LANES), lambda i, j: (i, 0)),
        ],
        out_shape=[
            jax.ShapeDtypeStruct((t, n), BF16),
            jax.ShapeDtypeStruct((t, LANES), F32),
        ],
        scratch_shapes=[pltpu.VMEM((tm, d), BF16), pltpu.VMEM((tm, d), BF16)],
        compiler_params=pltpu.CompilerParams(dimension_semantics=("arbitrary", "arbitrary"),
                                             vmem_limit_bytes=INPROJ_VMEM_LIMIT),
        name="inproj",
    )(x2, mod4, mod4, x2, mod4, mod4, norm_w.reshape(1, d), w_main, w_small.astype(BF16))


def _ret_kernel(lg_ref, q_ref, k_ref, v_ref, rg_ref, ga_ref, cos_ref, sin_ref, o_ref,
                state_ref, intra_ref, qd_ref, kd_ref):
    c = SEQ_TILE
    first = jnp.logical_and(pl.program_id(0) == 0, pl.program_id(1) == 0)

    @pl.when(first)
    def _():
        row = lax.broadcasted_iota(jnp.int32, (c, c), 0)
        col = lax.broadcasted_iota(jnp.int32, (c, c), 1)
        rel = (row - col).astype(F32)
        causal = row >= col
        pos = lax.broadcasted_iota(jnp.int32, (c, RET_DK), 0).astype(F32)
        for h in range(RET_HEADS):
            lg = lg_ref[h:h + 1, :]
            intra_ref[h] = jnp.where(causal, jnp.exp(jnp.where(causal, rel, 0.0) * lg), 0.0)
            qd_ref[h] = jnp.exp((pos + 1.0) * lg[:, :RET_DK])
            kd_ref[h] = jnp.exp((c - 1.0 - pos) * lg[:, :RET_DK])

    @pl.when(pl.program_id(1) == 0)
    def _():
        state_ref[...] = jnp.zeros_like(state_ref)

    cos = cos_ref[...]
    sin = sin_ref[...]
    for h in range(RET_HEADS):
        qs = slice(h * RET_DK, (h + 1) * RET_DK)
        vs = slice(h * RET_DV, (h + 1) * RET_DV)
        qr = q_ref[:, qs].astype(F32)
        kr = k_ref[:, qs].astype(F32)
        q = qr * cos + pltpu.roll(qr, RET_DK // 2, 1) * sin
        k = (kr * cos + pltpu.roll(kr, RET_DK // 2, 1) * sin) * (RET_DK ** -0.5)
        v = v_ref[:, vs]
        state = state_ref[h]
        chunk_decay = jnp.exp(float(c) * lg_ref[h:h + 1, :])
        scores = _dot_nt(q.astype(BF16), k.astype(BF16)) * intra_ref[h]
        o = _dot(scores.astype(BF16), v) + _dot((q * qd_ref[h]).astype(BF16), state.astype(BF16))
        state_ref[h] = state * chunk_decay + _dot_tn((k * kd_ref[h]).astype(BF16), v)
        o = o * lax.rsqrt(jnp.mean(o * o, axis=-1, keepdims=True) + NORM_EPS)
        y = _silu(rg_ref[:, vs].astype(F32)) * o
        o_ref[:, vs] = (_sigmoid(ga_ref[:, vs].astype(F32)) * y).astype(BF16)


def _retention(proj3, cos_t, sin_t, log_gamma):
    b, s, _ = proj3.shape
    ts = SEQ_TILE
    qk_w = RET_HEADS * RET_DK
    v_w = RET_HEADS * RET_DV
    return pl.pallas_call(
        _ret_kernel,
        grid=(b, s // ts),
        in_specs=[
            pl.BlockSpec((RET_HEADS, RET_DV), lambda i, j: (0, 0)),
            pl.BlockSpec((None, ts, qk_w), lambda i, j: (i, j, COL_RQ // qk_w)),
            pl.BlockSpec((None, ts, qk_w), lambda i, j: (i, j, COL_RK // qk_w)),
            pl.BlockSpec((None, ts, v_w), lambda i, j: (i, j, COL_RV // v_w)),
            pl.BlockSpec((None, ts, v_w), lambda i, j: (i, j, COL_RG // v_w)),
            pl.BlockSpec((None, ts, v_w), lambda i, j: (i, j, COL_GATE_A // v_w)),
            pl.BlockSpec((ts, RET_DK), lambda i, j: (j, 0)),
            pl.BlockSpec((ts, RET_DK), lambda i, j: (j, 0)),
        ],
        out_specs=pl.BlockSpec((None, ts, v_w), lambda i, j: (i, j, 0)),
        out_shape=jax.ShapeDtypeStruct((b, s, v_w), BF16),
        scratch_shapes=[
            pltpu.VMEM((RET_HEADS, RET_DK, RET_DV), F32),
            pltpu.VMEM((RET_HEADS, ts, ts), F32),
            pltpu.VMEM((RET_HEADS, ts, RET_DK), F32),
            pltpu.VMEM((RET_HEADS, ts, RET_DK), F32),
        ],
        compiler_params=_params("arbitrary", "arbitrary"),
        name="retention",
    )(log_gamma, proj3, proj3, proj3, proj3, proj3, cos_t, sin_t)


def _gdn_kernel(q_ref, k_ref, v_ref, z_ref, gb_ref, gates_ref, alog_ref, dtb_ref, cwq_ref, cwk_ref, cwv_ref,
                nw_ref, o_ref, state_ref, tail_ref):
    ts = SEQ_TILE
    cc = GDN_CHUNK
    qk_w = GDN_HEADS * GDN_DK

    @pl.when(pl.program_id(1) == 0)
    def _():
        state_ref[...] = jnp.zeros_like(state_ref)
        tail_ref[...] = jnp.zeros_like(tail_ref)

    def conv_silu(cur, tail, cw_ref):
        assert CONV_K == 4
        ext = jnp.concatenate([tail, cur], axis=0)
        ext1 = pltpu.roll(ext, 1, 0)
        near = ext * cw_ref[3:4, :] + ext1 * cw_ref[2:3, :]
        far = ext * cw_ref[1:2, :] + ext1 * cw_ref[0:1, :]
        return _silu((near + pltpu.roll(far, 2, 0))[SUBLANES:])

    row = lax.broadcasted_iota(jnp.int32, (ts, ts), 0)
    col = lax.broadcasted_iota(jnp.int32, (ts, ts), 1)
    same_chunk = (row // cc) == (col // cc)
    causal = jnp.logical_and(same_chunk, row >= col)
    strict = jnp.logical_and(same_chunk, row > col)
    eye = jnp.where(row == col, 1.0, 0.0)
    tri_lower = jnp.where(causal, 1.0, 0.0)
    tri_upper = jnp.where(jnp.logical_and(same_chunk, row <= col), 1.0, 0.0)

    units = [(s, h) for s in range(GDN_BATCH) for h in range(GDN_HEADS)]
    q_all, k_all, v_all, beta_all, gc_col_all, gc_row_all = [], [], [], [], [], []
    for s in range(GDN_BATCH):
        q_raw = q_ref[s].astype(F32)
        k_raw = k_ref[s].astype(F32)
        v_raw = v_ref[s].astype(F32)
        q_all.append(conv_silu(q_raw, tail_ref[s, :, 0:qk_w], cwq_ref))
        k_all.append(conv_silu(k_raw, tail_ref[s, :, qk_w:2 * qk_w], cwk_ref))
        v_all.append(conv_silu(v_raw, tail_ref[s, :, 2 * qk_w:], cwv_ref))
        tail_ref[s, :, 0:qk_w] = q_raw[ts - SUBLANES:]
        tail_ref[s, :, qk_w:2 * qk_w] = k_raw[ts - SUBLANES:]
        tail_ref[s, :, 2 * qk_w:] = v_raw[ts - SUBLANES:]

        gates = gates_ref[s]
        x = gates + dtb_ref[...]
        softplus = jnp.maximum(x, 0.0) + jnp.log1p(jnp.exp(-jnp.abs(x)))
        g_all = -jnp.exp(alog_ref[...]) * softplus
        beta_all.append(_sigmoid(gates))
        gc_col_all.append(_dot(tri_lower, g_all, precision=HIGHEST))
        gc_row_all.append(_dot_tn(g_all, tri_upper, precision=HIGHEST))

    qn, kn, k16, k_beta, gc, decay, beta = {}, {}, {}, {}, {}, {}, {}
    for u in units:
        s, h = u
        qs = slice(h * GDN_DK, (h + 1) * GDN_DK)
        qh = q_all[s][:, qs]
        kh = k_all[s][:, qs]
        qn[u] = qh * lax.rsqrt(jnp.sum(qh * qh, axis=-1, keepdims=True) + L2_EPS) * (GDN_DK ** -0.5)
        kn[u] = kh * lax.rsqrt(jnp.sum(kh * kh, axis=-1, keepdims=True) + L2_EPS)
        gc[u] = gc_col_all[s][:, h:h + 1]
        gcr = gc_row_all[s][h:h + 1, :]
        decay[u] = jnp.exp(jnp.where(causal, gc[u] - gcr, -jnp.inf))
        beta[u] = beta_all[s][:, GDN_HEADS + h:GDN_HEADS + h + 1]
        k_beta[u] = kn[u] * beta[u]
        k16[u] = kn[u].astype(BF16)
    a_mat = {u: jnp.where(strict, _dot_nt(k_beta[u].astype(BF16), k16[u]) * decay[u], 0.0) for u in units}
    p = {u: eye - a_mat[u] for u in units}
    m16 = {u: a_mat[u].astype(BF16) for u in units}
    for _ in range((cc - 1).bit_length() - 1):
        m16 = {u: _dot(m16[u], m16[u]).astype(BF16) for u in units}
        p = {u: p[u] + _dot(p[u].astype(BF16), m16[u]) for u in units}
    u_all, lhs_state, lhs_vnew, state_decay = {}, {}, {}, {}
    for u in units:
        s, h = u
        vs = slice(h * GDN_DV, (h + 1) * GDN_DV)
        t16 = p[u].astype(BF16)
        egc = jnp.exp(gc[u])
        u_all[u] = _dot(t16, (v_all[s][:, vs] * beta[u]).astype(BF16))
        w_all = _dot(t16, (k_beta[u] * egc).astype(BF16))
        qk = _dot_nt(qn[u].astype(BF16), k16[u]) * decay[u]
        qg = qn[u] * egc
        kn_t = kn[u].T
        gcr = gc_row_all[s][h:h + 1, :]
        for n in range(ts // cc):
            rs = slice(n * cc, (n + 1) * cc)
            g_last = gc[u][(n + 1) * cc - 1:(n + 1) * cc, :]
            kg_t = kn_t[:, rs] * jnp.exp(g_last - gcr[:, rs])
            lhs_state[u, n] = jnp.concatenate([w_all[rs], qg[rs]], axis=0).astype(BF16)
            lhs_vnew[u, n] = jnp.concatenate([qk[rs, rs], kg_t], axis=0).astype(BF16)
            state_decay[u, n] = jnp.exp(g_last)
    state = {u: state_ref[u[0], u[1]] for u in units}
    outs = {u: [] for u in units}
    for n in range(ts // cc):
        rs = slice(n * cc, (n + 1) * cc)
        for u in units:
            from_state = _dot(lhs_state[u, n], state[u].astype(BF16))
            v_new = (u_all[u][rs] - from_state[:cc]).astype(BF16)
            from_vnew = _dot(lhs_vnew[u, n], v_new)
            outs[u].append(from_state[cc:] + from_vnew[:cc])
            state[u] = state[u] * state_decay[u, n] + from_vnew[cc:]
    for u in units:
        s, h = u
        vs = slice(h * GDN_DV, (h + 1) * GDN_DV)
        state_ref[s, h] = state[u]
        o = jnp.concatenate(outs[u], axis=0)
        o = o * lax.rsqrt(jnp.mean(o * o, axis=-1, keepdims=True) + NORM_EPS) * nw_ref[...]
        o = o * _silu(z_ref[s, :, vs].astype(F32))
        o_ref[s, :, vs] = (_sigmoid(gb_ref[s, :, vs].astype(F32)) * o).astype(BF16)


def _gdn(proj3, gates3, alog_row, dtb_row, conv_w, norm_w):
    b, s, _ = proj3.shape
    ts = SEQ_TILE
    nb = GDN_BATCH
    assert b % nb == 0
    qk_w = GDN_HEADS * GDN_DK
    v_w = GDN_HEADS * GDN_DV
    return pl.pallas_call(
        _gdn_kernel,
        grid=(b // nb, s // ts),
        in_specs=[
            pl.BlockSpec((nb, ts, qk_w), lambda i, j: (i, j, COL_GQ // qk_w)),
            pl.BlockSpec((nb, ts, qk_w), lambda i, j: (i, j, COL_GK // qk_w)),
            pl.BlockSpec((nb, ts, v_w), lambda i, j: (i, j, COL_GV // v_w)),
            pl.BlockSpec((nb, ts, v_w), lambda i, j: (i, j, COL_GZ // v_w)),
            pl.BlockSpec((nb, ts, v_w), lambda i, j: (i, j, COL_GATE_B // v_w)),
            pl.BlockSpec((nb, ts, LANES), lambda i, j: (i, j, 0)),
            pl.BlockSpec((1, LANES), lambda i, j: (0, 0)),
            pl.BlockSpec((1, LANES), lambda i, j: (0, 0)),
            pl.BlockSpec((CONV_K, qk_w), lambda i, j: (0, 0)),
            pl.BlockSpec((CONV_K, qk_w), lambda i, j: (0, 1)),
            pl.BlockSpec((CONV_K, v_w), lambda i, j: (0, 1)),
            pl.BlockSpec((1, GDN_DV), lambda i, j: (0, 0)),
        ],
        out_specs=pl.BlockSpec((nb, ts, v_w), lambda i, j: (i, j, 0)),
        out_shape=jax.ShapeDtypeStruct((b, s, v_w), BF16),
        scratch_shapes=[
            pltpu.VMEM((nb, GDN_HEADS, GDN_DK, GDN_DV), F32),
            pltpu.VMEM((nb, SUBLANES, 2 * qk_w + v_w), F32),
        ],
        compiler_params=_params("arbitrary", "arbitrary"),
        name="gdn",
    )(proj3, proj3, proj3, proj3, proj3, gates3, alog_row, dtb_row, conv_w, conv_w, conv_w,
      norm_w.reshape(1, GDN_DV))


def _route(logits):
    lane = lax.broadcasted_iota(jnp.int32, logits.shape, 1)
    neg = jnp.float32(-jnp.inf)

    def first_max(v):
        top = jnp.max(v, axis=-1, keepdims=True)
        return top, jnp.min(jnp.where(v == top, lane, LANES), axis=-1, keepdims=True)

    gl = jnp.where(lane < N_GROUPS, logits, neg)
    g_max, g_idx = first_max(gl)
    g_top = 1.0 / jnp.sum(jnp.exp(gl - g_max), axis=-1, keepdims=True)
    lo = N_GROUPS + EXPERTS_PER_GROUP * g_idx
    el = jnp.where(jnp.logical_and(lane >= lo, lane < lo + EXPERTS_PER_GROUP), logits, neg)
    top1, i1 = first_max(el)
    top2, i2 = first_max(jnp.where(lane == i1, neg, el))
    gap = jnp.exp(top2 - top1)
    w1 = 1.0 / (1.0 + gap)
    comb = jnp.where(lane == i1, g_top * w1, jnp.where(lane == i2, g_top * (gap * w1), 0.0))
    return jnp.where(lane == 0, g_idx.astype(F32), comb)


def _outproj_kernel(ya_ref, yb_ref, x_ref, w_ref, gate_ref, nw_ref, shift_ref, scale_ref, wr_ref, wrh_ref, br_ref,
                    h_ref, xf_ref, comb_ref):
    tm = x_ref.shape[0]
    halves = [slice(k * (tm // 2), (k + 1) * (tm // 2)) for k in range(2)]
    hs, xfs, logits = [], [], []
    for rs in halves:
        merged = (ya_ref[rs, :].astype(F32) + yb_ref[rs, :].astype(F32)).astype(BF16)
        hs.append(x_ref[rs, :] + gate_ref[...] * _dot(merged, w_ref[...]))
    for rs, h in zip(halves, hs):
        h_ref[rs, :] = h
        y = h * lax.rsqrt(jnp.mean(h * h, axis=-1, keepdims=True) + NORM_EPS) * nw_ref[...]
        xfs.append(y * (1.0 + scale_ref[...]) + shift_ref[...])
    for rs, xf in zip(halves, xfs):
        xf_hi = xf.astype(BF16)
        xf_ref[rs, :] = xf_hi
        xf_lo = (xf - xf_hi.astype(F32)).astype(BF16)
        both = _dot(xf_hi, wr_ref[...])
        logits.append(both[:, :LANES] + both[:, LANES:] + _dot(xf_lo, wrh_ref[...]) + br_ref[...])
    for rs, lg in zip(halves, logits):
        comb_ref[rs, :] = _route(lg)


def _outproj(ya2, yb2, x2, w_out_b, mod4, norm_w, w_route, b_route, seq, tm=512):
    t, d = x2.shape
    per_b = seq // tm
    row_spec = pl.BlockSpec((tm, d), lambda i: (i, 0))
    wr_hi = w_route.astype(BF16)
    wr_lo = (w_route - wr_hi.astype(F32)).astype(BF16)

    def mod_spec(k):
        return pl.BlockSpec((None, None, 1, d), lambda i: (i // per_b, k, 0, 0))

    return pl.pallas_call(
        _outproj_kernel,
        grid=(t // tm,),
        in_specs=[
            row_spec, row_spec, row_spec,
            pl.BlockSpec((d, d), lambda i: (0, 0)),
            mod_spec(2),
            pl.BlockSpec((1, d), lambda i: (0, 0)),
            mod_spec(3),
            mod_spec(4),
            pl.BlockSpec((d, 2 * LANES), lambda i: (0, 0)),
            pl.BlockSpec((d, LANES), lambda i: (0, 0)),
            pl.BlockSpec((1, LANES), lambda i: (0, 0)),
        ],
        out_specs=[row_spec, row_spec, pl.BlockSpec((tm, LANES), lambda i: (i, 0))],
        out_shape=[
            jax.ShapeDtypeStruct((t, d), F32),
            jax.ShapeDtypeStruct((t, d), BF16),
            jax.ShapeDtypeStruct((t, LANES), F32),
        ],
        compiler_params=_params("arbitrary"),
        name="outproj",
    )(ya2, yb2, x2, w_out_b, mod4, norm_w.reshape(1, d), mod4, mod4,
      jnp.concatenate([wr_hi, wr_lo], axis=1), wr_hi, b_route)


def _moe_kernel(xf_ref, comb_ref, wg_ref, wu_ref, wd_ref, h_ref, gate_ref, nw_ref, o_ref,
                xs_ref, cs_ref, acc_ref, pos_ref, ltri_ref, meta_ref):
    i = pl.program_id(0)
    p = pl.program_id(1)
    tm = MOE_TILE
    steps_per_group = EXPERTS_PER_GROUP // MOE_EXPERTS_PER_STEP
    g = p // steps_per_group

    @pl.when(jnp.logical_and(i == 0, p == 0))
    def _():
        row = lax.broadcasted_iota(jnp.int32, (tm, tm), 0)
        col = lax.broadcasted_iota(jnp.int32, (tm, tm), 1)
        ltri_ref[...] = jnp.where(row >= col, 1.0, 0.0).astype(BF16)
        xs_ref[MOE_SORTED:, :] = jnp.zeros((MOE_ROWS - MOE_SORTED, xs_ref.shape[1]), BF16)
        cs_ref[MOE_SORTED:, :] = jnp.zeros((MOE_ROWS - MOE_SORTED, LANES), F32)
        acc_ref[MOE_SORTED:, :] = jnp.zeros((MOE_ROWS - MOE_SORTED, acc_ref.shape[1]), F32)

    @pl.when(p == 0)
    def _():
        comb = comb_ref[...]
        lane = lax.broadcasted_iota(jnp.int32, (tm, LANES), 1)
        gidx = comb[:, 0:1]
        mine = jnp.logical_and(lane.astype(F32) == gidx, lane < N_GROUPS)
        csum = _dot(ltri_ref[...], jnp.where(mine, 1.0, 0.0).astype(BF16))
        counts = jnp.broadcast_to(csum[tm - 1:tm, :], (SUBLANES, LANES))
        aligned = jnp.floor((counts + (MOE_ROW_ALIGN - 0.5)) * (1.0 / MOE_ROW_ALIGN)) * MOE_ROW_ALIGN
        lr = lax.broadcasted_iota(jnp.int32, (LANES, LANES), 0)
        lc = lax.broadcasted_iota(jnp.int32, (LANES, LANES), 1)
        seg_start = _dot(aligned, jnp.where(lr < lc, 1.0, 0.0), precision=HIGHEST)
        pos = jnp.sum(jnp.where(mine, seg_start[0:1, :] + csum - 1.0, 0.0), axis=-1, keepdims=True)
        pos_b = jnp.broadcast_to(pos, (tm, LANES))
        pos_ref[...] = pos_b.astype(jnp.int32)
        pos_row = pos_b.T[0:1, :].astype(jnp.int32)
        lane1 = lax.broadcasted_iota(jnp.int32, (SUBLANES, LANES), 1)
        for k in range(N_GROUPS):
            meta_ref[k] = jnp.sum(jnp.where(lane1 == k, counts, 0.0)[0:1, :]).astype(jnp.int32)
            meta_ref[N_GROUPS + k] = jnp.sum(jnp.where(lane1 == k, seg_start, 0.0)[0:1, :]).astype(jnp.int32)

        c_hi = comb.astype(BF16)
        c_lo = (comb - c_hi.astype(F32)).astype(BF16)
        sub = lax.broadcasted_iota(jnp.int32, (MOE_SORTED, tm), 0)
        perm = jnp.where(pos_row == sub, 1.0, 0.0).astype(BF16)
        xs_ref[0:MOE_SORTED, :] = _dot(perm, xf_ref[...]).astype(BF16)
        cs_ref[0:MOE_SORTED, :] = _dot(perm, c_hi) + _dot(perm, c_lo)
        acc_ref[0:MOE_SORTED, :] = jnp.zeros((MOE_SORTED, acc_ref.shape[1]), F32)

    def expert_piece(start, m):
        rows = pl.ds(pl.multiple_of(start, MOE_ROW_ALIGN), m)
        xs = xs_ref[rows, :]
        cs = cs_ref[rows, :]
        lane_m = lax.broadcasted_iota(jnp.int32, (m, LANES), 1)
        contrib = None
        for j in range(MOE_EXPERTS_PER_STEP):
            e = p * MOE_EXPERTS_PER_STEP + j
            wcol = jnp.sum(jnp.where(lane_m == N_GROUPS + e, cs, 0.0), axis=-1, keepdims=True)
            for f0 in range(0, D_FF_EXPERT, MOE_FF_SLAB):
                fs = slice(f0, f0 + MOE_FF_SLAB)
                act = _silu(_dot(xs, wg_ref[j, :, fs])) * _dot(xs, wu_ref[j, :, fs]) * wcol
                part = _dot(act.astype(BF16), wd_ref[j, fs, :])
                contrib = part if contrib is None else contrib + part
        acc_ref[rows, :] += contrib

    count = meta_ref[g]
    seg = meta_ref[N_GROUPS + g]
    n_full = jnp.maximum(count - 1, 0) // MOE_PIECE

    def full_piece(k, carry):
        expert_piece(seg + k * MOE_PIECE, MOE_PIECE)
        return carry

    lax.fori_loop(0, n_full, full_piece, 0)
    last_start = seg + n_full * MOE_PIECE
    last_rows = count - n_full * MOE_PIECE
    for m in range(MOE_MIN_PIECE, MOE_PIECE + 1, MOE_BUCKET):
        lo = 0 if m == MOE_MIN_PIECE else m - MOE_BUCKET

        @pl.when(jnp.logical_and(last_rows > lo, last_rows <= m))
        def _(m=m):
            expert_piece(last_start, m)

    @pl.when(p == pl.num_programs(1) - 1)
    def _():
        pos = pos_ref[:, 0:1]
        lane_s = lax.broadcasted_iota(jnp.int32, (tm, MOE_SORTED), 1)
        inv = jnp.where(pos == lane_s, 1.0, 0.0).astype(BF16)
        moe = _dot(inv, acc_ref[0:MOE_SORTED, :].astype(BF16))
        h = h_ref[...] + gate_ref[...] * moe
        o_ref[...] = h * lax.rsqrt(jnp.mean(h * h, axis=-1, keepdims=True) + NORM_EPS) * nw_ref[...]


def _moe(xf2, comb, wg, wu, wd, h2, mod4, norm_out_w, seq):
    t, d = xf2.shape
    f = wg.shape[-1]
    tm = MOE_TILE
    per_b = seq // tm
    eps = MOE_EXPERTS_PER_STEP
    row_spec = pl.BlockSpec((tm, d), lambda i, p: (i, 0))
    return pl.pallas_call(
        _moe_kernel,
        grid=(t // tm, N_EXPERTS // eps),
        in_specs=[
            row_spec,
            pl.BlockSpec((tm, LANES), lambda i, p: (i, 0)),
            pl.BlockSpec((eps, d, f), lambda i, p: (p, 0, 0)),
            pl.BlockSpec((eps, d, f), lambda i, p: (p, 0, 0)),
            pl.BlockSpec((eps, f, d), lambda i, p: (p, 0, 0)),
            row_spec,
            pl.BlockSpec((None, None, 1, d), lambda i, p: (i // per_b, 5, 0, 0)),
            pl.BlockSpec((1, d), lambda i, p: (0, 0)),
        ],
        out_specs=row_spec,
        out_shape=jax.ShapeDtypeStruct((t, d), F32),
        scratch_shapes=[
            pltpu.VMEM((MOE_ROWS, d), BF16),
            pltpu.VMEM((MOE_ROWS, LANES), F32),
            pltpu.VMEM((MOE_ROWS, d), F32),
            pltpu.VMEM((tm, LANES), jnp.int32),
            pltpu.VMEM((tm, tm), BF16),
            pltpu.SMEM((2 * N_GROUPS,), jnp.int32),
        ],
        compiler_params=pltpu.CompilerParams(dimension_semantics=("arbitrary", "arbitrary"),
                                             vmem_limit_bytes=MOE_VMEM_LIMIT),
        name="moe",
    )(xf2, comb, wg, wu, wd, h2, mod4, norm_out_w.reshape(1, d))


def _pad_lanes(a):
    return jnp.pad(a, ((0, 0), (0, LANES - a.shape[1])))


def _layer(h3, c, mod_w, mod_b, norm_mix_w, w_in, conv_w, a_log, dt_bias, gdn_norm_w, w_out, norm_ffn_w,
           w_group, b_group, w_router, b_router, w_gate, w_up, w_down, norm_out_w):
    b, s, d = h3.shape
    t = b * s
    x2 = h3.reshape(t, d)

    n_gate_cols = 2 * GDN_HEADS
    small_lo = COL_GATE_A
    w_main = jnp.concatenate([w_in[:, :small_lo], w_in[:, small_lo + n_gate_cols:]], axis=1).astype(BF16)
    w_small = _pad_lanes(w_in[:, small_lo:small_lo + n_gate_cols])
    w_route = _pad_lanes(jnp.concatenate([w_group, w_router], axis=1))
    b_route = _pad_lanes(jnp.concatenate([b_group, b_router.reshape(-1)])[None, :])
    alog_row = _pad_lanes(a_log[None, :])
    dtb_row = _pad_lanes(dt_bias[None, :])
    f = w_gate.shape[-1]
    wg = w_gate.reshape(N_EXPERTS, d, f).astype(BF16)
    wu = w_up.reshape(N_EXPERTS, d, f).astype(BF16)
    wd = w_down.reshape(N_EXPERTS, f, d).astype(BF16)

    half = RET_DK // 2
    inv_freq = 1.0 / (ROPE_BASE ** (jnp.arange(half, dtype=F32) / half))
    ang = jnp.arange(s, dtype=F32)[:, None] * inv_freq[None, :]
    cos_t = jnp.concatenate([jnp.cos(ang), jnp.cos(ang)], axis=1)
    sin_t = jnp.concatenate([-jnp.sin(ang), jnp.sin(ang)], axis=1)
    log_gamma = jnp.log(1.0 - 2.0 ** (-5.0 - jnp.arange(RET_HEADS, dtype=F32)))
    log_gamma = jnp.broadcast_to(log_gamma[:, None], (RET_HEADS, RET_DV))

    mod4 = _mod(c, mod_w, mod_b).reshape(b, N_MOD, 1, d)
    proj, gates = _inproj(x2, mod4, norm_mix_w, w_main, w_small, s)
    proj3 = proj.reshape(b, s, N_MAIN)
    ya = _retention(proj3, cos_t, sin_t, log_gamma)
    yb = _gdn(proj3, gates.reshape(b, s, LANES), alog_row, dtb_row, conv_w, gdn_norm_w)
    h2, xf2, comb = _outproj(ya.reshape(t, d), yb.reshape(t, d), x2, w_out.astype(BF16), mod4, norm_ffn_w,
                             w_route, b_route, s)
    return _moe(xf2, comb, wg, wu, wd, h2, mod4, norm_out_w, s).reshape(b, s, d)


def kernel(x, c, mod_w, mod_b, norm_mix_w, w_in, gdn_conv_w, gdn_a_log, gdn_dt_bias, gdn_norm_w, w_out, norm_ffn_w,
           w_group, b_group, w_router, b_router, w_gate, w_up, w_down, norm_out_w):
    assert mod_w.shape[0] == 1, "one residual layer"
    return _layer(x, c, mod_w[0], mod_b[0], norm_mix_w[0], w_in[0], gdn_conv_w[0], gdn_a_log[0], gdn_dt_bias[0],
                  gdn_norm_w[0], w_out[0], norm_ffn_w[0], w_group[0], b_group[0], w_router[0], b_router[0],
                  w_gate[0], w_up[0], w_down[0], norm_out_w)
```

```python
import jax
import jax.numpy as jnp
from jax import lax
from jax.experimental import pallas as pl
from jax.experimental.pallas import tpu as pltpu

F32 = jnp.float32
BF16 = jnp.bfloat16
HIGHEST = lax.Precision.HIGHEST

RET_HEADS = 4
RET_DK = 128
RET_DV = 256
RET_BATCH = 2
GDN_HEADS = 4
GDN_DK = 128
GDN_DV = 256
GDN_CHUNK = 64
GDN_BATCH = 2
CONV_K = 4
N_GROUPS = 4
EXPERTS_PER_GROUP = 4
N_EXPERTS = N_GROUPS * EXPERTS_PER_GROUP
D_FF_EXPERT = 512
ROPE_BASE = 10000.0
NORM_EPS = 1e-6
L2_EPS = 1e-6
N_MOD = 6
LANES = 128
SUBLANES = 8
SEQ_TILE = 256
VMEM_LIMIT = 48 * 1024 * 1024
MOE_TILE = 1024
MOE_ROW_ALIGN = 16
MOE_SORTED = MOE_TILE + N_GROUPS * MOE_ROW_ALIGN
MOE_PIECE = 512
MOE_BUCKET = 64
MOE_MIN_PIECE = 128
MOE_ROWS = MOE_SORTED + MOE_PIECE
MOE_EXPERTS_PER_STEP = 2
MOE_FF_SLAB = 256
MOE_VMEM_LIMIT = 56 * 1024 * 1024
INPROJ_VMEM_LIMIT = 56 * 1024 * 1024

COL_RQ, COL_RK, COL_RV, COL_RG = 0, 512, 1024, 2048
COL_GQ, COL_GK, COL_GV, COL_GZ = 3072, 3584, 4096, 5120
COL_GATE_A, COL_GATE_B = 6144, 7168
N_MAIN = 8192


def _sigmoid(x):
    return 0.5 * jnp.tanh(0.5 * x) + 0.5


def _silu(x):
    h = 0.5 * x
    return h + h * jnp.tanh(h)


def _dot(a, b, **kw):
    return jnp.dot(a, b, preferred_element_type=F32, **kw)


def _dot_nt(a, b, **kw):
    return lax.dot_general(a, b, (((1,), (1,)), ((), ())), preferred_element_type=F32, **kw)


def _dot_tn(a, b, **kw):
    return lax.dot_general(a, b, (((0,), (0,)), ((), ())), preferred_element_type=F32, **kw)


def _params(*sem):
    return pltpu.CompilerParams(dimension_semantics=sem, vmem_limit_bytes=VMEM_LIMIT)


def _mod_kernel(c_ref, w_ref, b_ref, o_ref):
    a = _silu(c_ref[...])
    o_ref[...] = _dot(a, w_ref[...], precision=HIGHEST) + b_ref[...]


def _mod(c, mod_w, mod_b):
    b, d = c.shape
    n = mod_w.shape[1]
    tn = d
    return pl.pallas_call(
        _mod_kernel,
        grid=(n // tn,),
        in_specs=[
            pl.BlockSpec((b, d), lambda j: (0, 0)),
            pl.BlockSpec((d, tn), lambda j: (0, j)),
            pl.BlockSpec((1, tn), lambda j: (0, j)),
        ],
        out_specs=pl.BlockSpec((b, tn), lambda j: (0, j)),
        out_shape=jax.ShapeDtypeStruct((b, n), F32),
        compiler_params=_params("arbitrary"),
        name="mod",
    )(c, mod_w, mod_b.reshape(1, n))


def _inproj_kernel(x0_ref, shift0_ref, scale0_ref, xn_ref, shiftn_ref, scalen_ref, nw_ref, w_ref, ws_ref,
                   o_ref, og_ref, xb_even_ref, xb_odd_ref):
    i = pl.program_id(0)
    j = pl.program_id(1)
    tm = x0_ref.shape[0]
    slab = tm // pl.num_programs(1)

    def prepare(x, shift, scale):
        y = x * lax.rsqrt(jnp.mean(x * x, axis=-1, keepdims=True) + NORM_EPS) * nw_ref[...]
        return (y * (1.0 + scale) + shift).astype(BF16)

    @pl.when(jnp.logical_and(i == 0, j == 0))
    def _():
        xb_even_ref[...] = prepare(x0_ref[...], shift0_ref[...], scale0_ref[...])

    def step(cur_ref, nxt_ref):
        rows = pl.ds(pl.multiple_of(j * slab, slab), slab)
        nxt_ref[rows, :] = prepare(xn_ref[rows, :], shiftn_ref[...], scalen_ref[...])
        xb = cur_ref[...]
        o_ref[...] = _dot(xb, w_ref[...]).astype(BF16)

        @pl.when(j == 0)
        def _():
            og_ref[...] = _dot(xb, ws_ref[...])

    @pl.when(i % 2 == 0)
    def _():
        step(xb_even_ref, xb_odd_ref)

    @pl.when(i % 2 == 1)
    def _():
        step(xb_odd_ref, xb_even_ref)


def _inproj(x2, mod4, norm_w, w_main, w_small, seq, tm=1024, tn=4096):
    t, d = x2.shape
    n = w_main.shape[1]
    per_b = seq // tm
    last = t // tm - 1

    def nxt(i):
        return jnp.minimum(i + 1, last)

    return pl.pallas_call(
        _inproj_kernel,
        grid=(t // tm, n // tn),
        in_specs=[
            pl.BlockSpec((tm, d), lambda i, j: (0, 0), pipeline_mode=pl.Buffered(1)),
            pl.BlockSpec((None, None, 1, d), lambda i, j: (0, 0, 0, 0)),
            pl.BlockSpec((None, None, 1, d), lambda i, j: (0, 1, 0, 0)),
            pl.BlockSpec((tm, d), lambda i, j: (nxt(i), 0)),
            pl.BlockSpec((None, None, 1, d), lambda i, j: (nxt(i) // per_b, 0, 0, 0)),
            pl.BlockSpec((None, None, 1, d), lambda i, j: (nxt(i) // per_b, 1, 0, 0)),
            pl.BlockSpec((1, d), lambda i, j: (0, 0)),
            pl.BlockSpec((d, tn), lambda i, j: (0, j)),
            pl.BlockSpec((d, LANES), lambda i, j: (0, 0)),
        ],
        out_specs=[
            pl.BlockSpec((tm, tn), lambda i, j: (i, j)),
            pl.BlockSpec((tm, LANES), lambda i, j: (i, 0)),
        ],
        out_shape=[
            jax.ShapeDtypeStruct((t, n), BF16),
            jax.ShapeDtypeStruct((t, LANES), F32),
        ],
        scratch_shapes=[pltpu.VMEM((tm, d), BF16), pltpu.VMEM((tm, d), BF16)],
        compiler_params=pltpu.CompilerParams(dimension_semantics=("arbitrary", "arbitrary"),
                                             vmem_limit_bytes=INPROJ_VMEM_LIMIT),
        name="inproj",
    )(x2, mod4, mod4, x2, mod4, mod4, norm_w.reshape(1, d), w_main, w_small.astype(BF16))


def _ret_kernel(lg_ref, q_ref, k_ref, v_ref, rg_ref, ga_ref, cos_ref, sin_ref, o_ref,
                state_ref, intra_ref, qd_ref, kd_ref):
    c = SEQ_TILE
    first = jnp.logical_and(pl.program_id(0) == 0, pl.program_id(1) == 0)

    @pl.when(first)
    def _():
        row = lax.broadcasted_iota(jnp.int32, (c, c), 0)
        col = lax.broadcasted_iota(jnp.int32, (c, c), 1)
        rel = (row - col).astype(F32)
        causal = row >= col
        pos = lax.broadcasted_iota(jnp.int32, (c, RET_DK), 0).astype(F32)
        for h in range(RET_HEADS):
            lg = lg_ref[h:h + 1, :]
            intra_ref[h] = jnp.where(causal, jnp.exp(jnp.where(causal, rel, 0.0) * lg), 0.0)
            qd_ref[h] = jnp.exp((pos + 1.0) * lg[:, :RET_DK])
            kd_ref[h] = jnp.exp((c - 1.0 - pos) * lg[:, :RET_DK])

    @pl.when(pl.program_id(1) == 0)
    def _():
        state_ref[...] = jnp.zeros_like(state_ref)

    cos = cos_ref[...]
    sin = sin_ref[...]
    for s in range(RET_BATCH):
        for h in range(RET_HEADS):
            qs = slice(h * RET_DK, (h + 1) * RET_DK)
            vs = slice(h * RET_DV, (h + 1) * RET_DV)
            qr = q_ref[s, :, qs].astype(F32)
            kr = k_ref[s, :, qs].astype(F32)
            q = qr * cos + pltpu.roll(qr, RET_DK // 2, 1) * sin
            k = (kr * cos + pltpu.roll(kr, RET_DK // 2, 1) * sin) * (RET_DK ** -0.5)
            v = v_ref[s, :, vs]
            state = state_ref[s, h]
            chunk_decay = jnp.exp(float(c) * lg_ref[h:h + 1, :])
            scores = _dot_nt(q.astype(BF16), k.astype(BF16)) * intra_ref[h]
            o = _dot(scores.astype(BF16), v) + _dot((q * qd_ref[h]).astype(BF16), state.astype(BF16))
            state_ref[s, h] = state * chunk_decay + _dot_tn((k * kd_ref[h]).astype(BF16), v)
            o = o * lax.rsqrt(jnp.mean(o * o, axis=-1, keepdims=True) + NORM_EPS)
            y = _silu(rg_ref[s, :, vs].astype(F32)) * o
            o_ref[s, :, vs] = (_sigmoid(ga_ref[s, :, vs].astype(F32)) * y).astype(BF16)


def _retention(proj3, cos_t, sin_t, log_gamma):
    b, s, _ = proj3.shape
    ts = SEQ_TILE
    nb = RET_BATCH
    assert b % nb == 0
    qk_w = RET_HEADS * RET_DK
    v_w = RET_HEADS * RET_DV
    return pl.pallas_call(
        _ret_kernel,
        grid=(b // nb, s // ts),
        in_specs=[
            pl.BlockSpec((RET_HEADS, RET_DV), lambda i, j: (0, 0)),
            pl.BlockSpec((nb, ts, qk_w), lambda i, j: (i, j, COL_RQ // qk_w)),
            pl.BlockSpec((nb, ts, qk_w), lambda i, j: (i, j, COL_RK // qk_w)),
            pl.BlockSpec((nb, ts, v_w), lambda i, j: (i, j, COL_RV // v_w)),
            pl.BlockSpec((nb, ts, v_w), lambda i, j: (i, j, COL_RG // v_w)),
            pl.BlockSpec((nb, ts, v_w), lambda i, j: (i, j, COL_GATE_A // v_w)),
            pl.BlockSpec((ts, RET_DK), lambda i, j: (j, 0)),
            pl.BlockSpec((ts, RET_DK), lambda i, j: (j, 0)),
        ],
        out_specs=pl.BlockSpec((nb, ts, v_w), lambda i, j: (i, j, 0)),
        out_shape=jax.ShapeDtypeStruct((b, s, v_w), BF16),
        scratch_shapes=[
            pltpu.VMEM((nb, RET_HEADS, RET_DK, RET_DV), F32),
            pltpu.VMEM((RET_HEADS, ts, ts), F32),
            pltpu.VMEM((RET_HEADS, ts, RET_DK), F32),
            pltpu.VMEM((RET_HEADS, ts, RET_DK), F32),
        ],
        compiler_params=_params("arbitrary", "arbitrary"),
        name="retention",
    )(log_gamma, proj3, proj3, proj3, proj3, proj3, cos_t, sin_t)


def _gdn_kernel(q_ref, k_ref, v_ref, z_ref, gb_ref, gates_ref, alog_ref, dtb_ref, cwq_ref, cwk_ref, cwv_ref,
                nw_ref, o_ref, state_ref, tail_ref):
    ts = SEQ_TILE
    cc = GDN_CHUNK
    qk_w = GDN_HEADS * GDN_DK

    @pl.when(pl.program_id(1) == 0)
    def _():
        state_ref[...] = jnp.zeros_like(state_ref)
        tail_ref[...] = jnp.zeros_like(tail_ref)

    def conv_silu(cur, tail, cw_ref):
        assert CONV_K == 4
        ext = jnp.concatenate([tail, cur], axis=0)
        ext1 = pltpu.roll(ext, 1, 0)
        near = ext * cw_ref[3:4, :] + ext1 * cw_ref[2:3, :]
        far = ext * cw_ref[1:2, :] + ext1 * cw_ref[0:1, :]
        return _silu((near + pltpu.roll(far, 2, 0))[SUBLANES:])

    row = lax.broadcasted_iota(jnp.int32, (ts, ts), 0)
    col = lax.broadcasted_iota(jnp.int32, (ts, ts), 1)
    same_chunk = (row // cc) == (col // cc)
    causal = jnp.logical_and(same_chunk, row >= col)
    strict = jnp.logical_and(same_chunk, row > col)
    eye = jnp.where(row == col, 1.0, 0.0)
    tri_lower = jnp.where(causal, 1.0, 0.0)
    tri_upper = jnp.where(jnp.logical_and(same_chunk, row <= col), 1.0, 0.0)

    units = [(s, h) for s in range(GDN_BATCH) for h in range(GDN_HEADS)]
    q_all, k_all, v_all, beta_all, gc_col_all, gc_row_all = [], [], [], [], [], []
    for s in range(GDN_BATCH):
        q_raw = q_ref[s].astype(F32)
        k_raw = k_ref[s].astype(F32)
        v_raw = v_ref[s].astype(F32)
        q_all.append(conv_silu(q_raw, tail_ref[s, :, 0:qk_w], cwq_ref))
        k_all.append(conv_silu(k_raw, tail_ref[s, :, qk_w:2 * qk_w], cwk_ref))
        v_all.append(conv_silu(v_raw, tail_ref[s, :, 2 * qk_w:], cwv_ref))
        tail_ref[s, :, 0:qk_w] = q_raw[ts - SUBLANES:]
        tail_ref[s, :, qk_w:2 * qk_w] = k_raw[ts - SUBLANES:]
        tail_ref[s, :, 2 * qk_w:] = v_raw[ts - SUBLANES:]

        gates = gates_ref[s]
        x = gates + dtb_ref[...]
        softplus = jnp.maximum(x, 0.0) + jnp.log1p(jnp.exp(-jnp.abs(x)))
        g_all = -jnp.exp(alog_ref[...]) * softplus
        beta_all.append(_sigmoid(gates))
        gc_col_all.append(_dot(tri_lower, g_all, precision=HIGHEST))
        gc_row_all.append(_dot_tn(g_all, tri_upper, precision=HIGHEST))

    qn, kn, k16, k_beta, gc, decay, beta = {}, {}, {}, {}, {}, {}, {}
    for u in units:
        s, h = u
        qs = slice(h * GDN_DK, (h + 1) * GDN_DK)
        qh = q_all[s][:, qs]
        kh = k_all[s][:, qs]
        qn[u] = qh * lax.rsqrt(jnp.sum(qh * qh, axis=-1, keepdims=True) + L2_EPS) * (GDN_DK ** -0.5)
        kn[u] = kh * lax.rsqrt(jnp.sum(kh * kh, axis=-1, keepdims=True) + L2_EPS)
        gc[u] = gc_col_all[s][:, h:h + 1]
        gcr = gc_row_all[s][h:h + 1, :]
        decay[u] = jnp.exp(jnp.where(causal, gc[u] - gcr, -jnp.inf))
        beta[u] = beta_all[s][:, GDN_HEADS + h:GDN_HEADS + h + 1]
        k_beta[u] = kn[u] * beta[u]
        k16[u] = kn[u].astype(BF16)
    a_mat = {u: jnp.where(strict, _dot_nt(k_beta[u].astype(BF16), k16[u]) * decay[u], 0.0) for u in units}
    p = {u: eye - a_mat[u] for u in units}
    m16 = {u: a_mat[u].astype(BF16) for u in units}
    for _ in range((cc - 1).bit_length() - 1):
        m16 = {u: _dot(m16[u], m16[u]).astype(BF16) for u in units}
        p = {u: p[u] + _dot(p[u].astype(BF16), m16[u]) for u in units}
    u_all, lhs_state, lhs_vnew, state_decay = {}, {}, {}, {}
    for u in units:
        s, h = u
        vs = slice(h * GDN_DV, (h + 1) * GDN_DV)
        t16 = p[u].astype(BF16)
        egc = jnp.exp(gc[u])
        u_all[u] = _dot(t16, (v_all[s][:, vs] * beta[u]).astype(BF16))
        w_all = _dot(t16, (k_beta[u] * egc).astype(BF16))
        qk = _dot_nt(qn[u].astype(BF16), k16[u]) * decay[u]
        qg = qn[u] * egc
        kn_t = kn[u].T
        gcr = gc_row_all[s][h:h + 1, :]
        for n in range(ts // cc):
            rs = slice(n * cc, (n + 1) * cc)
            g_last = gc[u][(n + 1) * cc - 1:(n + 1) * cc, :]
            kg_t = kn_t[:, rs] * jnp.exp(g_last - gcr[:, rs])
            lhs_state[u, n] = jnp.concatenate([w_all[rs], qg[rs]], axis=0).astype(BF16)
            lhs_vnew[u, n] = jnp.concatenate([qk[rs, rs], kg_t], axis=0).astype(BF16)
            state_decay[u, n] = jnp.exp(g_last)
    state = {u: state_ref[u[0], u[1]] for u in units}
    outs = {u: [] for u in units}
    for n in range(ts // cc):
        rs = slice(n * cc, (n + 1) * cc)
        for u in units:
            from_state = _dot(lhs_state[u, n], state[u].astype(BF16))
            v_new = (u_all[u][rs] - from_state[:cc]).astype(BF16)
            from_vnew = _dot(lhs_vnew[u, n], v_new)
            outs[u].append(from_state[cc:] + from_vnew[:cc])
            state[u] = state[u] * state_decay[u, n] + from_vnew[cc:]
    for u in units:
        s, h = u
        vs = slice(h * GDN_DV, (h + 1) * GDN_DV)
        state_ref[s, h] = state[u]
        o = jnp.concatenate(outs[u], axis=0)
        o = o * lax.rsqrt(jnp.mean(o * o, axis=-1, keepdims=True) + NORM_EPS) * nw_ref[...]
        o = o * _silu(z_ref[s, :, vs].astype(F32))
        o_ref[s, :, vs] = (_sigmoid(gb_ref[s, :, vs].astype(F32)) * o).astype(BF16)


def _gdn(proj3, gates3, alog_row, dtb_row, conv_w, norm_w):
    b, s, _ = proj3.shape
    ts = SEQ_TILE
    nb = GDN_BATCH
    assert b % nb == 0
    qk_w = GDN_HEADS * GDN_DK
    v_w = GDN_HEADS * GDN_DV
    return pl.pallas_call(
        _gdn_kernel,
        grid=(b // nb, s // ts),
        in_specs=[
            pl.BlockSpec((nb, ts, qk_w), lambda i, j: (i, j, COL_GQ // qk_w)),
            pl.BlockSpec((nb, ts, qk_w), lambda i, j: (i, j, COL_GK // qk_w)),
            pl.BlockSpec((nb, ts, v_w), lambda i, j: (i, j, COL_GV // v_w)),
            pl.BlockSpec((nb, ts, v_w), lambda i, j: (i, j, COL_GZ // v_w)),
            pl.BlockSpec((nb, ts, v_w), lambda i, j: (i, j, COL_GATE_B // v_w)),
            pl.BlockSpec((nb, ts, LANES), lambda i, j: (i, j, 0)),
            pl.BlockSpec((1, LANES), lambda i, j: (0, 0)),
            pl.BlockSpec((1, LANES), lambda i, j: (0, 0)),
            pl.BlockSpec((CONV_K, qk_w), lambda i, j: (0, 0)),
            pl.BlockSpec((CONV_K, qk_w), lambda i, j: (0, 1)),
            pl.BlockSpec((CONV_K, v_w), lambda i, j: (0, 1)),
            pl.BlockSpec((1, GDN_DV), lambda i, j: (0, 0)),
        ],
        out_specs=pl.BlockSpec((nb, ts, v_w), lambda i, j: (i, j, 0)),
        out_shape=jax.ShapeDtypeStruct((b, s, v_w), BF16),
        scratch_shapes=[
            pltpu.VMEM((nb, GDN_HEADS, GDN_DK, GDN_DV), F32),
            pltpu.VMEM((nb, SUBLANES, 2 * qk_w + v_w), F32),
        ],
        compiler_params=_params("arbitrary", "arbitrary"),
        name="gdn",
    )(proj3, proj3, proj3, proj3, proj3, gates3, alog_row, dtb_row, conv_w, conv_w, conv_w,
      norm_w.reshape(1, GDN_DV))


def _route(logits):
    lane = lax.broadcasted_iota(jnp.int32, logits.shape, 1)
    neg = jnp.float32(-jnp.inf)

    def first_max(v):
        top = jnp.max(v, axis=-1, keepdims=True)
        return top, jnp.min(jnp.where(v == top, lane, LANES), axis=-1, keepdims=True)

    gl = jnp.where(lane < N_GROUPS, logits, neg)
    g_max, g_idx = first_max(gl)
    g_top = 1.0 / jnp.sum(jnp.exp(gl - g_max), axis=-1, keepdims=True)
    lo = N_GROUPS + EXPERTS_PER_GROUP * g_idx
    el = jnp.where(jnp.logical_and(lane >= lo, lane < lo + EXPERTS_PER_GROUP), logits, neg)
    top1, i1 = first_max(el)
    top2, i2 = first_max(jnp.where(lane == i1, neg, el))
    gap = jnp.exp(top2 - top1)
    w1 = 1.0 / (1.0 + gap)
    comb = jnp.where(lane == i1, g_top * w1, jnp.where(lane == i2, g_top * (gap * w1), 0.0))
    return jnp.where(lane == 0, g_idx.astype(F32), comb)


def _outproj_kernel(ya_ref, yb_ref, x_ref, w_ref, gate_ref, nw_ref, shift_ref, scale_ref, wr_ref, wrh_ref, br_ref,
                    h_ref, xf_ref, comb_ref):
    tm = x_ref.shape[0]
    halves = [slice(k * (tm // 2), (k + 1) * (tm // 2)) for k in range(2)]
    hs, xfs, logits = [], [], []
    for rs in halves:
        merged = (ya_ref[rs, :].astype(F32) + yb_ref[rs, :].astype(F32)).astype(BF16)
        hs.append(x_ref[rs, :] + gate_ref[...] * _dot(merged, w_ref[...]))
    for rs, h in zip(halves, hs):
        h_ref[rs, :] = h
        y = h * lax.rsqrt(jnp.mean(h * h, axis=-1, keepdims=True) + NORM_EPS) * nw_ref[...]
        xfs.append(y * (1.0 + scale_ref[...]) + shift_ref[...])
    for rs, xf in zip(halves, xfs):
        xf_hi = xf.astype(BF16)
        xf_ref[rs, :] = xf_hi
        xf_lo = (xf - xf_hi.astype(F32)).astype(BF16)
        both = _dot(xf_hi, wr_ref[...])
        logits.append(both[:, :LANES] + both[:, LANES:] + _dot(xf_lo, wrh_ref[...]) + br_ref[...])
    for rs, lg in zip(halves, logits):
        comb_ref[rs, :] = _route(lg)


def _outproj(ya2, yb2, x2, w_out_b, mod4, norm_w, w_route, b_route, seq, tm=1024):
    t, d = x2.shape
    per_b = seq // tm
    row_spec = pl.BlockSpec((tm, d), lambda i: (i, 0))
    wr_hi = w_route.astype(BF16)
    wr_lo = (w_route - wr_hi.astype(F32)).astype(BF16)

    def mod_spec(k):
        return pl.BlockSpec((None, None, 1, d), lambda i: (i // per_b, k, 0, 0))

    return pl.pallas_call(
        _outproj_kernel,
        grid=(t // tm,),
        in_specs=[
            row_spec, row_spec, row_spec,
            pl.BlockSpec((d, d), lambda i: (0, 0)),
            mod_spec(2),
            pl.BlockSpec((1, d), lambda i: (0, 0)),
            mod_spec(3),
            mod_spec(4),
            pl.BlockSpec((d, 2 * LANES), lambda i: (0, 0)),
            pl.BlockSpec((d, LANES), lambda i: (0, 0)),
            pl.BlockSpec((1, LANES), lambda i: (0, 0)),
        ],
        out_specs=[row_spec, row_spec, pl.BlockSpec((tm, LANES), lambda i: (i, 0))],
        out_shape=[
            jax.ShapeDtypeStruct((t, d), F32),
            jax.ShapeDtypeStruct((t, d), BF16),
            jax.ShapeDtypeStruct((t, LANES), F32),
        ],
        compiler_params=_params("arbitrary"),
        name="outproj",
    )(ya2, yb2, x2, w_out_b, mod4, norm_w.reshape(1, d), mod4, mod4,
      jnp.concatenate([wr_hi, wr_lo], axis=1), wr_hi, b_route)


def _moe_kernel(xf_ref, comb_ref, wg_ref, wu_ref, wd_ref, h_ref, gate_ref, nw_ref, o_ref,
                xs_ref, cs_ref, acc_ref, pos_ref, ltri_ref, meta_ref):
    i = pl.program_id(0)
    p = pl.program_id(1)
    tm = MOE_TILE
    steps_per_group = EXPERTS_PER_GROUP // MOE_EXPERTS_PER_STEP
    g = p // steps_per_group

    @pl.when(jnp.logical_and(i == 0, p == 0))
    def _():
        row = lax.broadcasted_iota(jnp.int32, (tm, tm), 0)
        col = lax.broadcasted_iota(jnp.int32, (tm, tm), 1)
        ltri_ref[...] = jnp.where(row >= col, 1.0, 0.0).astype(BF16)
        xs_ref[MOE_SORTED:, :] = jnp.zeros((MOE_ROWS - MOE_SORTED, xs_ref.shape[1]), BF16)
        cs_ref[MOE_SORTED:, :] = jnp.zeros((MOE_ROWS - MOE_SORTED, LANES), F32)
        acc_ref[MOE_SORTED:, :] = jnp.zeros((MOE_ROWS - MOE_SORTED, acc_ref.shape[1]), F32)

    @pl.when(p == 0)
    def _():
        comb = comb_ref[...]
        lane = lax.broadcasted_iota(jnp.int32, (tm, LANES), 1)
        gidx = comb[:, 0:1]
        mine = jnp.logical_and(lane.astype(F32) == gidx, lane < N_GROUPS)
        csum = _dot(ltri_ref[...], jnp.where(mine, 1.0, 0.0).astype(BF16))
        counts = jnp.broadcast_to(csum[tm - 1:tm, :], (SUBLANES, LANES))
        aligned = jnp.floor((counts + (MOE_ROW_ALIGN - 0.5)) * (1.0 / MOE_ROW_ALIGN)) * MOE_ROW_ALIGN
        lr = lax.broadcasted_iota(jnp.int32, (LANES, LANES), 0)
        lc = lax.broadcasted_iota(jnp.int32, (LANES, LANES), 1)
        seg_start = _dot(aligned, jnp.where(lr < lc, 1.0, 0.0), precision=HIGHEST)
        pos = jnp.sum(jnp.where(mine, seg_start[0:1, :] + csum - 1.0, 0.0), axis=-1, keepdims=True)
        pos_b = jnp.broadcast_to(pos, (tm, LANES))
        pos_ref[...] = pos_b.astype(jnp.int32)
        pos_row = pos_b.T[0:1, :].astype(jnp.int32)
        lane1 = lax.broadcasted_iota(jnp.int32, (SUBLANES, LANES), 1)
        for k in range(N_GROUPS):
            meta_ref[k] = jnp.sum(jnp.where(lane1 == k, counts, 0.0)[0:1, :]).astype(jnp.int32)
            meta_ref[N_GROUPS + k] = jnp.sum(jnp.where(lane1 == k, seg_start, 0.0)[0:1, :]).astype(jnp.int32)

        c_hi = comb.astype(BF16)
        c_lo = (comb - c_hi.astype(F32)).astype(BF16)
        sub = lax.broadcasted_iota(jnp.int32, (MOE_SORTED, tm), 0)
        perm = jnp.where(pos_row == sub, 1.0, 0.0).astype(BF16)
        xs_ref[0:MOE_SORTED, :] = _dot(perm, xf_ref[...]).astype(BF16)
        cs_ref[0:MOE_SORTED, :] = _dot(perm, c_hi) + _dot(perm, c_lo)
        acc_ref[0:MOE_SORTED, :] = jnp.zeros((MOE_SORTED, acc_ref.shape[1]), F32)

    def expert_piece(start, m):
        rows = pl.ds(pl.multiple_of(start, MOE_ROW_ALIGN), m)
        xs = xs_ref[rows, :]
        cs = cs_ref[rows, :]
        lane_m = lax.broadcasted_iota(jnp.int32, (m, LANES), 1)
        contrib = None
        for j in range(MOE_EXPERTS_PER_STEP):
            e = p * MOE_EXPERTS_PER_STEP + j
            wcol = jnp.sum(jnp.where(lane_m == N_GROUPS + e, cs, 0.0), axis=-1, keepdims=True)
            for f0 in range(0, D_FF_EXPERT, MOE_FF_SLAB):
                fs = slice(f0, f0 + MOE_FF_SLAB)
                act = _silu(_dot(xs, wg_ref[j, :, fs])) * _dot(xs, wu_ref[j, :, fs]) * wcol
                part = _dot(act.astype(BF16), wd_ref[j, fs, :])
                contrib = part if contrib is None else contrib + part
        acc_ref[rows, :] += contrib

    count = meta_ref[g]
    seg = meta_ref[N_GROUPS + g]
    n_full = jnp.maximum(count - 1, 0) // MOE_PIECE

    def full_piece(k, carry):
        expert_piece(seg + k * MOE_PIECE, MOE_PIECE)
        return carry

    lax.fori_loop(0, n_full, full_piece, 0)
    last_start = seg + n_full * MOE_PIECE
    last_rows = count - n_full * MOE_PIECE
    for m in range(MOE_MIN_PIECE, MOE_PIECE + 1, MOE_BUCKET):
        lo = 0 if m == MOE_MIN_PIECE else m - MOE_BUCKET

        @pl.when(jnp.logical_and(last_rows > lo, last_rows <= m))
        def _(m=m):
            expert_piece(last_start, m)

    @pl.when(p == pl.num_programs(1) - 1)
    def _():
        pos = pos_ref[:, 0:1]
        lane_s = lax.broadcasted_iota(jnp.int32, (tm, MOE_SORTED), 1)
        inv = jnp.where(pos == lane_s, 1.0, 0.0).astype(BF16)
        moe = _dot(inv, acc_ref[0:MOE_SORTED, :].astype(BF16))
        h = h_ref[...] + gate_ref[...] * moe
        o_ref[...] = h * lax.rsqrt(jnp.mean(h * h, axis=-1, keepdims=True) + NORM_EPS) * nw_ref[...]


def _moe(xf2, comb, wg, wu, wd, h2, mod4, norm_out_w, seq):
    t, d = xf2.shape
    f = wg.shape[-1]
    tm = MOE_TILE
    per_b = seq // tm
    eps = MOE_EXPERTS_PER_STEP
    row_spec = pl.BlockSpec((tm, d), lambda i, p: (i, 0))
    return pl.pallas_call(
        _moe_kernel,
        grid=(t // tm, N_EXPERTS // eps),
        in_specs=[
            row_spec,
            pl.BlockSpec((tm, LANES), lambda i, p: (i, 0)),
            pl.BlockSpec((eps, d, f), lambda i, p: (p, 0, 0)),
            pl.BlockSpec((eps, d, f), lambda i, p: (p, 0, 0)),
            pl.BlockSpec((eps, f, d), lambda i, p: (p, 0, 0)),
            row_spec,
            pl.BlockSpec((None, None, 1, d), lambda i, p: (i // per_b, 5, 0, 0)),
            pl.BlockSpec((1, d), lambda i, p: (0, 0)),
        ],
        out_specs=row_spec,
        out_shape=jax.ShapeDtypeStruct((t, d), F32),
        scratch_shapes=[
            pltpu.VMEM((MOE_ROWS, d), BF16),
            pltpu.VMEM((MOE_ROWS, LANES), F32),
            pltpu.VMEM((MOE_ROWS, d), F32),
            pltpu.VMEM((tm, LANES), jnp.int32),
            pltpu.VMEM((tm, tm), BF16),
            pltpu.SMEM((2 * N_GROUPS,), jnp.int32),
        ],
        compiler_params=pltpu.CompilerParams(dimension_semantics=("arbitrary", "arbitrary"),
                                             vmem_limit_bytes=MOE_VMEM_LIMIT),
        name="moe",
    )(xf2, comb, wg, wu, wd, h2, mod4, norm_out_w.reshape(1, d))


def _pad_lanes(a):
    return jnp.pad(a, ((0, 0), (0, LANES - a.shape[1])))


def _layer(h3, c, mod_w, mod_b, norm_mix_w, w_in, conv_w, a_log, dt_bias, gdn_norm_w, w_out, norm_ffn_w,
           w_group, b_group, w_router, b_router, w_gate, w_up, w_down, norm_out_w):
    b, s, d = h3.shape
    t = b * s
    x2 = h3.reshape(t, d)

    n_gate_cols = 2 * GDN_HEADS
    small_lo = COL_GATE_A
    w_main = jnp.concatenate([w_in[:, :small_lo], w_in[:, small_lo + n_gate_cols:]], axis=1).astype(BF16)
    w_small = _pad_lanes(w_in[:, small_lo:small_lo + n_gate_cols])
    w_route = _pad_lanes(jnp.concatenate([w_group, w_router], axis=1))
    b_route = _pad_lanes(jnp.concatenate([b_group, b_router.reshape(-1)])[None, :])
    alog_row = _pad_lanes(a_log[None, :])
    dtb_row = _pad_lanes(dt_bias[None, :])
    f = w_gate.shape[-1]
    wg = w_gate.reshape(N_EXPERTS, d, f).astype(BF16)
    wu = w_up.reshape(N_EXPERTS, d, f).astype(BF16)
    wd = w_down.reshape(N_EXPERTS, f, d).astype(BF16)

    half = RET_DK // 2
    inv_freq = 1.0 / (ROPE_BASE ** (jnp.arange(half, dtype=F32) / half))
    ang = jnp.arange(s, dtype=F32)[:, None] * inv_freq[None, :]
    cos_t = jnp.concatenate([jnp.cos(ang), jnp.cos(ang)], axis=1)
    sin_t = jnp.concatenate([-jnp.sin(ang), jnp.sin(ang)], axis=1)
    log_gamma = jnp.log(1.0 - 2.0 ** (-5.0 - jnp.arange(RET_HEADS, dtype=F32)))
    log_gamma = jnp.broadcast_to(log_gamma[:, None], (RET_HEADS, RET_DV))

    mod4 = _mod(c, mod_w, mod_b).reshape(b, N_MOD, 1, d)
    proj, gates = _inproj(x2, mod4, norm_mix_w, w_main, w_small, s)
    proj3 = proj.reshape(b, s, N_MAIN)
    ya = _retention(proj3, cos_t, sin_t, log_gamma)
    yb = _gdn(proj3, gates.reshape(b, s, LANES), alog_row, dtb_row, conv_w, gdn_norm_w)
    h2, xf2, comb = _outproj(ya.reshape(t, d), yb.reshape(t, d), x2, w_out.astype(BF16), mod4, norm_ffn_w,
                             w_route, b_route, s)
    return _moe(xf2, comb, wg, wu, wd, h2, mod4, norm_out_w, s).reshape(b, s, d)


def kernel(x, c, mod_w, mod_b, norm_mix_w, w_in, gdn_conv_w, gdn_a_log, gdn_dt_bias, gdn_norm_w, w_out, norm_ffn_w,
           w_group, b_group, w_router, b_router, w_gate, w_up, w_down, norm_out_w):
    assert mod_w.shape[0] == 1, "one residual layer"
    return _layer(x, c, mod_w[0], mod_b[0], norm_mix_w[0], w_in[0], gdn_conv_w[0], gdn_a_log[0], gdn_dt_bias[0],
                  gdn_norm_w[0], w_out[0], norm_ffn_w[0], w_group[0], b_group[0], w_router[0], b_router[0],
                  w_gate[0], w_up[0], w_down[0], norm_out_w)
```

```python
import jax
import jax.numpy as jnp
from jax import lax
from jax.experimental import pallas as pl
from jax.experimental.pallas import tpu as pltpu

F32 = jnp.float32
BF16 = jnp.bfloat16
HIGHEST = lax.Precision.HIGHEST

RET_HEADS = 4
RET_DK = 128
RET_DV = 256
RET_BATCH = 2
GDN_HEADS = 4
GDN_DK = 128
GDN_DV = 256
GDN_CHUNK = 64
GDN_BATCH = 2
CONV_K = 4
N_GROUPS = 4
EXPERTS_PER_GROUP = 4
N_EXPERTS = N_GROUPS * EXPERTS_PER_GROUP
D_FF_EXPERT = 512
ROPE_BASE = 10000.0
NORM_EPS = 1e-6
L2_EPS = 1e-6
N_MOD = 6
LANES = 128
SUBLANES = 8
SEQ_TILE = 256
VMEM_LIMIT = 48 * 1024 * 1024
MOE_TILE = 1024
MOE_ROW_ALIGN = 16
MOE_SORTED = MOE_TILE + N_GROUPS * MOE_ROW_ALIGN
MOE_PIECE = 512
MOE_BUCKET = 64
MOE_MIN_PIECE = 128
MOE_ROWS = MOE_SORTED + MOE_PIECE
MOE_EXPERTS_PER_STEP = 2
MOE_FF_SLAB = 256
MOE_VMEM_LIMIT = 56 * 1024 * 1024
INPROJ_VMEM_LIMIT = 56 * 1024 * 1024

COL_RQ, COL_RK, COL_RV, COL_RG = 0, 512, 1024, 2048
COL_GQ, COL_GK, COL_GV, COL_GZ = 3072, 3584, 4096, 5120
COL_GATE_A, COL_GATE_B = 6144, 7168
N_MAIN = 8192


def _sigmoid(x):
    return 0.5 * jnp.tanh(0.5 * x) + 0.5


def _silu(x):
    h = 0.5 * x
    return h + h * jnp.tanh(h)


def _dot(a, b, **kw):
    return jnp.dot(a, b, preferred_element_type=F32, **kw)


def _dot_nt(a, b, **kw):
    return lax.dot_general(a, b, (((1,), (1,)), ((), ())), preferred_element_type=F32, **kw)


def _dot_tn(a, b, **kw):
    return lax.dot_general(a, b, (((0,), (0,)), ((), ())), preferred_element_type=F32, **kw)


def _params(*sem):
    return pltpu.CompilerParams(dimension_semantics=sem, vmem_limit_bytes=VMEM_LIMIT)


def _mod_kernel(c_ref, w_ref, b_ref, o_ref):
    a = _silu(c_ref[...])
    o_ref[...] = _dot(a, w_ref[...], precision=HIGHEST) + b_ref[...]


def _mod(c, mod_w, mod_b):
    b, d = c.shape
    n = mod_w.shape[1]
    tn = d
    return pl.pallas_call(
        _mod_kernel,
        grid=(n // tn,),
        in_specs=[
            pl.BlockSpec((b, d), lambda j: (0, 0)),
            pl.BlockSpec((d, tn), lambda j: (0, j)),
            pl.BlockSpec((1, tn), lambda j: (0, j)),
        ],
        out_specs=pl.BlockSpec((b, tn), lambda j: (0, j)),
        out_shape=jax.ShapeDtypeStruct((b, n), F32),
        compiler_params=_params("arbitrary"),
        name="mod",
    )(c, mod_w, mod_b.reshape(1, n))


def _inproj_kernel(x0_ref, shift0_ref, scale0_ref, xn_ref, shiftn_ref, scalen_ref, nw_ref, w_ref, ws_ref,
                   o_ref, og_ref, xb_even_ref, xb_odd_ref):
    i = pl.program_id(0)
    j = pl.program_id(1)
    tm = x0_ref.shape[0]
    slab = tm // pl.num_programs(1)

    def prepare(x, shift, scale):
        y = x * lax.rsqrt(jnp.mean(x * x, axis=-1, keepdims=True) + NORM_EPS) * nw_ref[...]
        return (y * (1.0 + scale) + shift).astype(BF16)

    @pl.when(jnp.logical_and(i == 0, j == 0))
    def _():
        xb_even_ref[...] = prepare(x0_ref[...], shift0_ref[...], scale0_ref[...])

    def step(cur_ref, nxt_ref):
        rows = pl.ds(pl.multiple_of(j * slab, slab), slab)
        nxt_ref[rows, :] = prepare(xn_ref[rows, :], shiftn_ref[...], scalen_ref[...])
        xb = cur_ref[...]
        o_ref[...] = _dot(xb, w_ref[...]).astype(BF16)

        @pl.when(j == 0)
        def _():
            og_ref[...] = _dot(xb, ws_ref[...])

    @pl.when(i % 2 == 0)
    def _():
        step(xb_even_ref, xb_odd_ref)

    @pl.when(i % 2 == 1)
    def _():
        step(xb_odd_ref, xb_even_ref)


def _inproj(x2, mod4, norm_w, w_main, w_small, seq, tm=1024, tn=4096):
    t, d = x2.shape
    n = w_main.shape[1]
    per_b = seq // tm
    last = t // tm - 1

    def nxt(i):
        return jnp.minimum(i + 1, last)

    return pl.pallas_call(
        _inproj_kernel,
        grid=(t // tm, n // tn),
        in_specs=[
            pl.BlockSpec((tm, d), lambda i, j: (0, 0), pipeline_mode=pl.Buffered(1)),
            pl.BlockSpec((None, None, 1, d), lambda i, j: (0, 0, 0, 0)),
            pl.BlockSpec((None, None, 1, d), lambda i, j: (0, 1, 0, 0)),
            pl.BlockSpec((tm, d), lambda i, j: (nxt(i), 0)),
            pl.BlockSpec((None, None, 1, d), lambda i, j: (nxt(i) // per_b, 0, 0, 0)),
            pl.BlockSpec((None, None, 1, d), lambda i, j: (nxt(i) // per_b, 1, 0, 0)),
            pl.BlockSpec((1, d), lambda i, j: (0, 0)),
            pl.BlockSpec((d, tn), lambda i, j: (0, j)),
            pl.BlockSpec((d, LANES), lambda i, j: (0, 0)),
        ],
        out_specs=[
            pl.BlockSpec((tm, tn), lambda i, j: (i, j)),
            pl.BlockSpec((tm, LANES), lambda i, j: (i, 0)),
        ],
        out_shape=[
            jax.ShapeDtypeStruct((t, n), BF16),
            jax.ShapeDtypeStruct((t, LANES), F32),
        ],
        scratch_shapes=[pltpu.VMEM((tm, d), BF16), pltpu.VMEM((tm, d), BF16)],
        compiler_params=pltpu.CompilerParams(dimension_semantics=("arbitrary", "arbitrary"),
                                             vmem_limit_bytes=INPROJ_VMEM_LIMIT),
        name="inproj",
    )(x2, mod4, mod4, x2, mod4, mod4, norm_w.reshape(1, d), w_main, w_small.astype(BF16))


def _ret_kernel(lg_ref, q_ref, k_ref, v_ref, rg_ref, ga_ref, cos_ref, sin_ref, o_ref,
                state_ref, intra_ref, qd_ref, kd_ref):
    c = SEQ_TILE
    first = jnp.logical_and(pl.program_id(0) == 0, pl.program_id(1) == 0)

    @pl.when(first)
    def _():
        row = lax.broadcasted_iota(jnp.int32, (c, c), 0)
        col = lax.broadcasted_iota(jnp.int32, (c, c), 1)
        rel = (row - col).astype(F32)
        causal = row >= col
        pos = lax.broadcasted_iota(jnp.int32, (c, RET_DK), 0).astype(F32)
        for h in range(RET_HEADS):
            lg = lg_ref[h:h + 1, :]
            intra_ref[h] = jnp.where(causal, jnp.exp(jnp.where(causal, rel, 0.0) * lg), 0.0)
            qd_ref[h] = jnp.exp((pos + 1.0) * lg[:, :RET_DK])
            kd_ref[h] = jnp.exp((c - 1.0 - pos) * lg[:, :RET_DK])

    @pl.when(pl.program_id(1) == 0)
    def _():
        state_ref[...] = jnp.zeros_like(state_ref)

    cos = cos_ref[...]
    sin = sin_ref[...]
    for s in range(RET_BATCH):
        for h in range(RET_HEADS):
            qs = slice(h * RET_DK, (h + 1) * RET_DK)
            vs = slice(h * RET_DV, (h + 1) * RET_DV)
            qr = q_ref[s, :, qs].astype(F32)
            kr = k_ref[s, :, qs].astype(F32)
            q = qr * cos + pltpu.roll(qr, RET_DK // 2, 1) * sin
            k = (kr * cos + pltpu.roll(kr, RET_DK // 2, 1) * sin) * (RET_DK ** -0.5)
            v = v_ref[s, :, vs]
            state = state_ref[s, h]
            chunk_decay = jnp.exp(float(c) * lg_ref[h:h + 1, :])
            scores = _dot_nt(q.astype(BF16), k.astype(BF16)) * intra_ref[h]
            o = _dot(scores.astype(BF16), v) + _dot((q * qd_ref[h]).astype(BF16), state.astype(BF16))
            state_ref[s, h] = state * chunk_decay + _dot_tn((k * kd_ref[h]).astype(BF16), v)
            o = o * lax.rsqrt(jnp.mean(o * o, axis=-1, keepdims=True) + NORM_EPS)
            y = _silu(rg_ref[s, :, vs].astype(F32)) * o
            o_ref[s, :, vs] = (_sigmoid(ga_ref[s, :, vs].astype(F32)) * y).astype(BF16)


def _retention(proj3, cos_t, sin_t, log_gamma):
    b, s, _ = proj3.shape
    ts = SEQ_TILE
    nb = RET_BATCH
    assert b % nb == 0
    qk_w = RET_HEADS * RET_DK
    v_w = RET_HEADS * RET_DV
    return pl.pallas_call(
        _ret_kernel,
        grid=(b // nb, s // ts),
        in_specs=[
            pl.BlockSpec((RET_HEADS, RET_DV), lambda i, j: (0, 0)),
            pl.BlockSpec((nb, ts, qk_w), lambda i, j: (i, j, COL_RQ // qk_w)),
            pl.BlockSpec((nb, ts, qk_w), lambda i, j: (i, j, COL_RK // qk_w)),
            pl.BlockSpec((nb, ts, v_w), lambda i, j: (i, j, COL_RV // v_w)),
            pl.BlockSpec((nb, ts, v_w), lambda i, j: (i, j, COL_RG // v_w)),
            pl.BlockSpec((nb, ts, v_w), lambda i, j: (i, j, COL_GATE_A // v_w)),
            pl.BlockSpec((ts, RET_DK), lambda i, j: (j, 0)),
            pl.BlockSpec((ts, RET_DK), lambda i, j: (j, 0)),
        ],
        out_specs=pl.BlockSpec((nb, ts, v_w), lambda i, j: (i, j, 0)),
        out_shape=jax.ShapeDtypeStruct((b, s, v_w), BF16),
        scratch_shapes=[
            pltpu.VMEM((nb, RET_HEADS, RET_DK, RET_DV), F32),
            pltpu.VMEM((RET_HEADS, ts, ts), F32),
            pltpu.VMEM((RET_HEADS, ts, RET_DK), F32),
            pltpu.VMEM((RET_HEADS, ts, RET_DK), F32),
        ],
        compiler_params=_params("arbitrary", "arbitrary"),
        name="retention",
    )(log_gamma, proj3, proj3, proj3, proj3, proj3, cos_t, sin_t)


def _gdn_kernel(q_ref, k_ref, v_ref, z_ref, gb_ref, gates_ref, alog_ref, dtb_ref, cwq_ref, cwk_ref, cwv_ref,
                nw_ref, o_ref, state_ref, tail_ref):
    ts = SEQ_TILE
    cc = GDN_CHUNK
    qk_w = GDN_HEADS * GDN_DK

    @pl.when(pl.program_id(1) == 0)
    def _():
        state_ref[...] = jnp.zeros_like(state_ref)
        tail_ref[...] = jnp.zeros_like(tail_ref)

    def conv_silu(cur, tail, cw_ref):
        assert CONV_K == 4
        ext = jnp.concatenate([tail, cur], axis=0)
        ext1 = pltpu.roll(ext, 1, 0)
        near = ext * cw_ref[3:4, :] + ext1 * cw_ref[2:3, :]
        far = ext * cw_ref[1:2, :] + ext1 * cw_ref[0:1, :]
        return _silu((near + pltpu.roll(far, 2, 0))[SUBLANES:])

    row = lax.broadcasted_iota(jnp.int32, (ts, ts), 0)
    col = lax.broadcasted_iota(jnp.int32, (ts, ts), 1)
    same_chunk = (row // cc) == (col // cc)
    causal = jnp.logical_and(same_chunk, row >= col)
    strict = jnp.logical_and(same_chunk, row > col)
    eye = jnp.where(row == col, 1.0, 0.0)
    tri_lower = jnp.where(causal, 1.0, 0.0).astype(BF16)
    tri_upper = jnp.where(jnp.logical_and(same_chunk, row <= col), 1.0, 0.0).astype(BF16)

    units = [(s, h) for s in range(GDN_BATCH) for h in range(GDN_HEADS)]
    q_all, k_all, v_all, beta_all, gc_col_all, gc_row_all = [], [], [], [], [], []
    for s in range(GDN_BATCH):
        q_raw = q_ref[s].astype(F32)
        k_raw = k_ref[s].astype(F32)
        v_raw = v_ref[s].astype(F32)
        q_all.append(conv_silu(q_raw, tail_ref[s, :, 0:qk_w], cwq_ref))
        k_all.append(conv_silu(k_raw, tail_ref[s, :, qk_w:2 * qk_w], cwk_ref))
        v_all.append(conv_silu(v_raw, tail_ref[s, :, 2 * qk_w:], cwv_ref))
        tail_ref[s, :, 0:qk_w] = q_raw[ts - SUBLANES:]
        tail_ref[s, :, qk_w:2 * qk_w] = k_raw[ts - SUBLANES:]
        tail_ref[s, :, 2 * qk_w:] = v_raw[ts - SUBLANES:]

        gates = gates_ref[s]
        x = gates + dtb_ref[...]
        softplus = jnp.maximum(x, 0.0) + jnp.log1p(jnp.exp(-jnp.abs(x)))
        g_all = -jnp.exp(alog_ref[...]) * softplus
        beta_all.append(_sigmoid(gates))
        g_hi = g_all.astype(BF16)
        g_rest = g_all - g_hi.astype(F32)
        g_mid = g_rest.astype(BF16)
        g_lo = (g_rest - g_mid.astype(F32)).astype(BF16)
        gc_col_all.append(sum(_dot(tri_lower, part) for part in (g_hi, g_mid, g_lo)))
        gc_row_all.append(sum(_dot_tn(part, tri_upper) for part in (g_hi, g_mid, g_lo)))

    qn, kn, k16, k_beta, gc, decay, beta = {}, {}, {}, {}, {}, {}, {}
    for u in units:
        s, h = u
        qs = slice(h * GDN_DK, (h + 1) * GDN_DK)
        qh = q_all[s][:, qs]
        kh = k_all[s][:, qs]
        qn[u] = qh * lax.rsqrt(jnp.sum(qh * qh, axis=-1, keepdims=True) + L2_EPS) * (GDN_DK ** -0.5)
        kn[u] = kh * lax.rsqrt(jnp.sum(kh * kh, axis=-1, keepdims=True) + L2_EPS)
        gc[u] = gc_col_all[s][:, h:h + 1]
        gcr = gc_row_all[s][h:h + 1, :]
        decay[u] = jnp.exp(jnp.where(causal, gc[u] - gcr, -jnp.inf))
        beta[u] = beta_all[s][:, GDN_HEADS + h:GDN_HEADS + h + 1]
        k_beta[u] = kn[u] * beta[u]
        k16[u] = kn[u].astype(BF16)
    a_mat = {u: jnp.where(strict, _dot_nt(k_beta[u].astype(BF16), k16[u]) * decay[u], 0.0) for u in units}
    p = {u: eye - a_mat[u] for u in units}
    m16 = {u: a_mat[u].astype(BF16) for u in units}
    for _ in range((cc - 1).bit_length() - 1):
        m16 = {u: _dot(m16[u], m16[u]).astype(BF16) for u in units}
        p = {u: p[u] + _dot(p[u].astype(BF16), m16[u]) for u in units}
    u_all, lhs_state, lhs_vnew, state_decay = {}, {}, {}, {}
    for u in units:
        s, h = u
        vs = slice(h * GDN_DV, (h + 1) * GDN_DV)
        t16 = p[u].astype(BF16)
        egc = jnp.exp(gc[u])
        u_all[u] = _dot(t16, (v_all[s][:, vs] * beta[u]).astype(BF16))
        w_all = _dot(t16, (k_beta[u] * egc).astype(BF16))
        qk = _dot_nt(qn[u].astype(BF16), k16[u]) * decay[u]
        qg = qn[u] * egc
        kn_t = kn[u].T
        gcr = gc_row_all[s][h:h + 1, :]
        for n in range(ts // cc):
            rs = slice(n * cc, (n + 1) * cc)
            g_last = gc[u][(n + 1) * cc - 1:(n + 1) * cc, :]
            kg_t = kn_t[:, rs] * jnp.exp(g_last - gcr[:, rs])
            lhs_state[u, n] = jnp.concatenate([w_all[rs], qg[rs]], axis=0).astype(BF16)
            lhs_vnew[u, n] = jnp.concatenate([qk[rs, rs], kg_t], axis=0).astype(BF16)
            state_decay[u, n] = jnp.exp(g_last)
    state = {u: state_ref[u[0], u[1]] for u in units}
    outs = {u: [] for u in units}
    for n in range(ts // cc):
        rs = slice(n * cc, (n + 1) * cc)
        for u in units:
            from_state = _dot(lhs_state[u, n], state[u].astype(BF16))
            v_new = (u_all[u][rs] - from_state[:cc]).astype(BF16)
            from_vnew = _dot(lhs_vnew[u, n], v_new)
            outs[u].append(from_state[cc:] + from_vnew[:cc])
            state[u] = state[u] * state_decay[u, n] + from_vnew[cc:]
    for u in units:
        s, h = u
        vs = slice(h * GDN_DV, (h + 1) * GDN_DV)
        state_ref[s, h] = state[u]
        o = jnp.concatenate(outs[u], axis=0)
        o = o * lax.rsqrt(jnp.mean(o * o, axis=-1, keepdims=True) + NORM_EPS) * nw_ref[...]
        o = o * _silu(z_ref[s, :, vs].astype(F32))
        o_ref[s, :, vs] = (_sigmoid(gb_ref[s, :, vs].astype(F32)) * o).astype(BF16)


def _gdn(proj3, gates3, alog_row, dtb_row, conv_w, norm_w):
    b, s, _ = proj3.shape
    ts = SEQ_TILE
    nb = GDN_BATCH
    assert b % nb == 0
    qk_w = GDN_HEADS * GDN_DK
    v_w = GDN_HEADS * GDN_DV
    return pl.pallas_call(
        _gdn_kernel,
        grid=(b // nb, s // ts),
        in_specs=[
            pl.BlockSpec((nb, ts, qk_w), lambda i, j: (i, j, COL_GQ // qk_w)),
            pl.BlockSpec((nb, ts, qk_w), lambda i, j: (i, j, COL_GK // qk_w)),
            pl.BlockSpec((nb, ts, v_w), lambda i, j: (i, j, COL_GV // v_w)),
            pl.BlockSpec((nb, ts, v_w), lambda i, j: (i, j, COL_GZ // v_w)),
            pl.BlockSpec((nb, ts, v_w), lambda i, j: (i, j, COL_GATE_B // v_w)),
            pl.BlockSpec((nb, ts, LANES), lambda i, j: (i, j, 0)),
            pl.BlockSpec((1, LANES), lambda i, j: (0, 0)),
            pl.BlockSpec((1, LANES), lambda i, j: (0, 0)),
            pl.BlockSpec((CONV_K, qk_w), lambda i, j: (0, 0)),
            pl.BlockSpec((CONV_K, qk_w), lambda i, j: (0, 1)),
            pl.BlockSpec((CONV_K, v_w), lambda i, j: (0, 1)),
            pl.BlockSpec((1, GDN_DV), lambda i, j: (0, 0)),
        ],
        out_specs=pl.BlockSpec((nb, ts, v_w), lambda i, j: (i, j, 0)),
        out_shape=jax.ShapeDtypeStruct((b, s, v_w), BF16),
        scratch_shapes=[
            pltpu.VMEM((nb, GDN_HEADS, GDN_DK, GDN_DV), F32),
            pltpu.VMEM((nb, SUBLANES, 2 * qk_w + v_w), F32),
        ],
        compiler_params=_params("arbitrary", "arbitrary"),
        name="gdn",
    )(proj3, proj3, proj3, proj3, proj3, gates3, alog_row, dtb_row, conv_w, conv_w, conv_w,
      norm_w.reshape(1, GDN_DV))


def _route(logits):
    lane = lax.broadcasted_iota(jnp.int32, logits.shape, 1)
    neg = jnp.float32(-jnp.inf)

    def first_max(v):
        top = jnp.max(v, axis=-1, keepdims=True)
        return top, jnp.min(jnp.where(v == top, lane, LANES), axis=-1, keepdims=True)

    gl = jnp.where(lane < N_GROUPS, logits, neg)
    g_max, g_idx = first_max(gl)
    g_top = 1.0 / jnp.sum(jnp.exp(gl - g_max), axis=-1, keepdims=True)
    lo = N_GROUPS + EXPERTS_PER_GROUP * g_idx
    el = jnp.where(jnp.logical_and(lane >= lo, lane < lo + EXPERTS_PER_GROUP), logits, neg)
    top1, i1 = first_max(el)
    top2, i2 = first_max(jnp.where(lane == i1, neg, el))
    gap = jnp.exp(top2 - top1)
    w1 = 1.0 / (1.0 + gap)
    comb = jnp.where(lane == i1, g_top * w1, jnp.where(lane == i2, g_top * (gap * w1), 0.0))
    return jnp.where(lane == 0, g_idx.astype(F32), comb)


def _outproj_kernel(ya_ref, yb_ref, x_ref, w_ref, gate_ref, nw_ref, shift_ref, scale_ref, wr_ref, wrh_ref, br_ref,
                    h_ref, xf_ref, comb_ref):
    tm = x_ref.shape[0]
    halves = [slice(k * (tm // 2), (k + 1) * (tm // 2)) for k in range(2)]
    hs, xfs, logits = [], [], []
    for rs in halves:
        merged = ya_ref[rs, :] + yb_ref[rs, :]
        hs.append(x_ref[rs, :] + gate_ref[...] * _dot(merged, w_ref[...]))
    for rs, h in zip(halves, hs):
        h_ref[rs, :] = h
        y = h * lax.rsqrt(jnp.mean(h * h, axis=-1, keepdims=True) + NORM_EPS) * nw_ref[...]
        xfs.append(y * (1.0 + scale_ref[...]) + shift_ref[...])
    for rs, xf in zip(halves, xfs):
        xf_hi = xf.astype(BF16)
        xf_ref[rs, :] = xf_hi
        xf_lo = (xf - xf_hi.astype(F32)).astype(BF16)
        both = _dot(xf_hi, wr_ref[...])
        logits.append(both[:, :LANES] + both[:, LANES:] + _dot(xf_lo, wrh_ref[...]) + br_ref[...])
    for rs, lg in zip(halves, logits):
        comb_ref[rs, :] = _route(lg)


def _outproj(ya2, yb2, x2, w_out_b, mod4, norm_w, w_route, b_route, seq, tm=1024):
    t, d = x2.shape
    per_b = seq // tm
    row_spec = pl.BlockSpec((tm, d), lambda i: (i, 0))
    wr_hi = w_route.astype(BF16)
    wr_lo = (w_route - wr_hi.astype(F32)).astype(BF16)

    def mod_spec(k):
        return pl.BlockSpec((None, None, 1, d), lambda i: (i // per_b, k, 0, 0))

    return pl.pallas_call(
        _outproj_kernel,
        grid=(t // tm,),
        in_specs=[
            row_spec, row_spec, row_spec,
            pl.BlockSpec((d, d), lambda i: (0, 0)),
            mod_spec(2),
            pl.BlockSpec((1, d), lambda i: (0, 0)),
            mod_spec(3),
            mod_spec(4),
            pl.BlockSpec((d, 2 * LANES), lambda i: (0, 0)),
            pl.BlockSpec((d, LANES), lambda i: (0, 0)),
            pl.BlockSpec((1, LANES), lambda i: (0, 0)),
        ],
        out_specs=[row_spec, row_spec, pl.BlockSpec((tm, LANES), lambda i: (i, 0))],
        out_shape=[
            jax.ShapeDtypeStruct((t, d), F32),
            jax.ShapeDtypeStruct((t, d), BF16),
            jax.ShapeDtypeStruct((t, LANES), F32),
        ],
        compiler_params=_params("arbitrary"),
        name="outproj",
    )(ya2, yb2, x2, w_out_b, mod4, norm_w.reshape(1, d), mod4, mod4,
      jnp.concatenate([wr_hi, wr_lo], axis=1), wr_hi, b_route)


def _moe_kernel(xf_ref, comb_ref, wg_ref, wu_ref, wd_ref, h_ref, gate_ref, nw_ref, o_ref,
                xs_ref, cs_ref, acc_ref, pos_ref, ltri_ref, meta_ref):
    i = pl.program_id(0)
    p = pl.program_id(1)
    tm = MOE_TILE
    steps_per_group = EXPERTS_PER_GROUP // MOE_EXPERTS_PER_STEP
    g = p // steps_per_group

    @pl.when(jnp.logical_and(i == 0, p == 0))
    def _():
        row = lax.broadcasted_iota(jnp.int32, (tm, tm), 0)
        col = lax.broadcasted_iota(jnp.int32, (tm, tm), 1)
        ltri_ref[...] = jnp.where(row >= col, 1.0, 0.0).astype(BF16)
        xs_ref[MOE_SORTED:, :] = jnp.zeros((MOE_ROWS - MOE_SORTED, xs_ref.shape[1]), BF16)
        cs_ref[MOE_SORTED:, :] = jnp.zeros((MOE_ROWS - MOE_SORTED, LANES), F32)
        acc_ref[MOE_SORTED:, :] = jnp.zeros((MOE_ROWS - MOE_SORTED, acc_ref.shape[1]), F32)

    @pl.when(p == 0)
    def _():
        comb = comb_ref[...]
        lane = lax.broadcasted_iota(jnp.int32, (tm, LANES), 1)
        gidx = comb[:, 0:1]
        mine = jnp.logical_and(lane.astype(F32) == gidx, lane < N_GROUPS)
        csum = _dot(ltri_ref[...], jnp.where(mine, 1.0, 0.0).astype(BF16))
        counts = jnp.broadcast_to(csum[tm - 1:tm, :], (SUBLANES, LANES))
        aligned = jnp.floor((counts + (MOE_ROW_ALIGN - 0.5)) * (1.0 / MOE_ROW_ALIGN)) * MOE_ROW_ALIGN
        lr = lax.broadcasted_iota(jnp.int32, (LANES, LANES), 0)
        lc = lax.broadcasted_iota(jnp.int32, (LANES, LANES), 1)
        seg_start = _dot(aligned, jnp.where(lr < lc, 1.0, 0.0), precision=HIGHEST)
        pos = jnp.sum(jnp.where(mine, seg_start[0:1, :] + csum - 1.0, 0.0), axis=-1, keepdims=True)
        pos_b = jnp.broadcast_to(pos, (tm, LANES))
        pos_ref[...] = pos_b.astype(jnp.int32)
        pos_row = pos_b.T[0:1, :].astype(jnp.int32)
        lane1 = lax.broadcasted_iota(jnp.int32, (SUBLANES, LANES), 1)
        for k in range(N_GROUPS):
            meta_ref[k] = jnp.sum(jnp.where(lane1 == k, counts, 0.0)[0:1, :]).astype(jnp.int32)
            meta_ref[N_GROUPS + k] = jnp.sum(jnp.where(lane1 == k, seg_start, 0.0)[0:1, :]).astype(jnp.int32)

        c_hi = comb.astype(BF16)
        c_lo = (comb - c_hi.astype(F32)).astype(BF16)
        sub = lax.broadcasted_iota(jnp.int32, (MOE_SORTED, tm), 0)
        perm = jnp.where(pos_row == sub, 1.0, 0.0).astype(BF16)
        xs_ref[0:MOE_SORTED, :] = _dot(perm, xf_ref[...]).astype(BF16)
        cs_ref[0:MOE_SORTED, :] = _dot(perm, c_hi) + _dot(perm, c_lo)
        acc_ref[0:MOE_SORTED, :] = jnp.zeros((MOE_SORTED, acc_ref.shape[1]), F32)

    def expert_piece(start, m):
        rows = pl.ds(pl.multiple_of(start, MOE_ROW_ALIGN), m)
        xs = xs_ref[rows, :]
        cs = cs_ref[rows, :]
        lane_m = lax.broadcasted_iota(jnp.int32, (m, LANES), 1)
        contrib = None
        for j in range(MOE_EXPERTS_PER_STEP):
            e = p * MOE_EXPERTS_PER_STEP + j
            wcol = jnp.sum(jnp.where(lane_m == N_GROUPS + e, cs, 0.0), axis=-1, keepdims=True)
            for f0 in range(0, D_FF_EXPERT, MOE_FF_SLAB):
                fs = slice(f0, f0 + MOE_FF_SLAB)
                act = _silu(_dot(xs, wg_ref[j, :, fs])) * _dot(xs, wu_ref[j, :, fs]) * wcol
                part = _dot(act.astype(BF16), wd_ref[j, fs, :])
                contrib = part if contrib is None else contrib + part
        acc_ref[rows, :] += contrib

    count = meta_ref[g]
    seg = meta_ref[N_GROUPS + g]
    n_full = jnp.maximum(count - 1, 0) // MOE_PIECE

    def full_piece(k, carry):
        expert_piece(seg + k * MOE_PIECE, MOE_PIECE)
        return carry

    lax.fori_loop(0, n_full, full_piece, 0)
    last_start = seg + n_full * MOE_PIECE
    last_rows = count - n_full * MOE_PIECE
    for m in range(MOE_MIN_PIECE, MOE_PIECE + 1, MOE_BUCKET):
        lo = 0 if m == MOE_MIN_PIECE else m - MOE_BUCKET

        @pl.when(jnp.logical_and(last_rows > lo, last_rows <= m))
        def _(m=m):
            expert_piece(last_start, m)

    @pl.when(p == pl.num_programs(1) - 1)
    def _():
        pos = pos_ref[:, 0:1]
        lane_s = lax.broadcasted_iota(jnp.int32, (tm, MOE_SORTED), 1)
        inv = jnp.where(pos == lane_s, 1.0, 0.0).astype(BF16)
        moe = _dot(inv, acc_ref[0:MOE_SORTED, :].astype(BF16))
        h = h_ref[...] + gate_ref[...] * moe
        o_ref[...] = h * lax.rsqrt(jnp.mean(h * h, axis=-1, keepdims=True) + NORM_EPS) * nw_ref[...]


def _moe(xf2, comb, wg, wu, wd, h2, mod4, norm_out_w, seq):
    t, d = xf2.shape
    f = wg.shape[-1]
    tm = MOE_TILE
    per_b = seq // tm
    eps = MOE_EXPERTS_PER_STEP
    row_spec = pl.BlockSpec((tm, d), lambda i, p: (i, 0))
    return pl.pallas_call(
        _moe_kernel,
        grid=(t // tm, N_EXPERTS // eps),
        in_specs=[
            row_spec,
            pl.BlockSpec((tm, LANES), lambda i, p: (i, 0)),
            pl.BlockSpec((eps, d, f), lambda i, p: (p, 0, 0)),
            pl.BlockSpec((eps, d, f), lambda i, p: (p, 0, 0)),
            pl.BlockSpec((eps, f, d), lambda i, p: (p, 0, 0)),
            row_spec,
            pl.BlockSpec((None, None, 1, d), lambda i, p: (i // per_b, 5, 0, 0)),
            pl.BlockSpec((1, d), lambda i, p: (0, 0)),
        ],
        out_specs=row_spec,
        out_shape=jax.ShapeDtypeStruct((t, d), F32),
        scratch_shapes=[
            pltpu.VMEM((MOE_ROWS, d), BF16),
            pltpu.VMEM((MOE_ROWS, LANES), F32),
            pltpu.VMEM((MOE_ROWS, d), F32),
            pltpu.VMEM((tm, LANES), jnp.int32),
            pltpu.VMEM((tm, tm), BF16),
            pltpu.SMEM((2 * N_GROUPS,), jnp.int32),
        ],
        compiler_params=pltpu.CompilerParams(dimension_semantics=("arbitrary", "arbitrary"),
                                             vmem_limit_bytes=MOE_VMEM_LIMIT),
        name="moe",
    )(xf2, comb, wg, wu, wd, h2, mod4, norm_out_w.reshape(1, d))


def _pad_lanes(a):
    return jnp.pad(a, ((0, 0), (0, LANES - a.shape[1])))


def _layer(h3, c, mod_w, mod_b, norm_mix_w, w_in, conv_w, a_log, dt_bias, gdn_norm_w, w_out, norm_ffn_w,
           w_group, b_group, w_router, b_router, w_gate, w_up, w_down, norm_out_w):
    b, s, d = h3.shape
    t = b * s
    x2 = h3.reshape(t, d)

    n_gate_cols = 2 * GDN_HEADS
    small_lo = COL_GATE_A
    w_main = jnp.concatenate([w_in[:, :small_lo], w_in[:, small_lo + n_gate_cols:]], axis=1).astype(BF16)
    w_small = _pad_lanes(w_in[:, small_lo:small_lo + n_gate_cols])
    w_route = _pad_lanes(jnp.concatenate([w_group, w_router], axis=1))
    b_route = _pad_lanes(jnp.concatenate([b_group, b_router.reshape(-1)])[None, :])
    alog_row = _pad_lanes(a_log[None, :])
    dtb_row = _pad_lanes(dt_bias[None, :])
    f = w_gate.shape[-1]
    wg = w_gate.reshape(N_EXPERTS, d, f).astype(BF16)
    wu = w_up.reshape(N_EXPERTS, d, f).astype(BF16)
    wd = w_down.reshape(N_EXPERTS, f, d).astype(BF16)

    half = RET_DK // 2
    inv_freq = 1.0 / (ROPE_BASE ** (jnp.arange(half, dtype=F32) / half))
    ang = jnp.arange(s, dtype=F32)[:, None] * inv_freq[None, :]
    cos_t = jnp.concatenate([jnp.cos(ang), jnp.cos(ang)], axis=1)
    sin_t = jnp.concatenate([-jnp.sin(ang), jnp.sin(ang)], axis=1)
    log_gamma = jnp.log(1.0 - 2.0 ** (-5.0 - jnp.arange(RET_HEADS, dtype=F32)))
    log_gamma = jnp.broadcast_to(log_gamma[:, None], (RET_HEADS, RET_DV))

    mod4 = _mod(c, mod_w, mod_b).reshape(b, N_MOD, 1, d)
    proj, gates = _inproj(x2, mod4, norm_mix_w, w_main, w_small, s)
    proj3 = proj.reshape(b, s, N_MAIN)
    ya = _retention(proj3, cos_t, sin_t, log_gamma)
    yb = _gdn(proj3, gates.reshape(b, s, LANES), alog_row, dtb_row, conv_w, gdn_norm_w)
    h2, xf2, comb = _outproj(ya.reshape(t, d), yb.reshape(t, d), x2, w_out.astype(BF16), mod4, norm_ffn_w,
                             w_route, b_route, s)
    return _moe(xf2, comb, wg, wu, wd, h2, mod4, norm_out_w, s).reshape(b, s, d)


def kernel(x, c, mod_w, mod_b, norm_mix_w, w_in, gdn_conv_w, gdn_a_log, gdn_dt_bias, gdn_norm_w, w_out, norm_ffn_w,
           w_group, b_group, w_router, b_router, w_gate, w_up, w_down, norm_out_w):
    assert mod_w.shape[0] == 1, "one residual layer"
    return _layer(x, c, mod_w[0], mod_b[0], norm_mix_w[0], w_in[0], gdn_conv_w[0], gdn_a_log[0], gdn_dt_bias[0],
                  gdn_norm_w[0], w_out[0], norm_ffn_w[0], w_group[0], b_group[0], w_router[0], b_router[0],
                  w_gate[0], w_up[0], w_down[0], norm_out_w)
```

```python
import jax
import jax.numpy as jnp
from jax import lax
from jax.experimental import pallas as pl
from jax.experimental.pallas import tpu as pltpu

F32 = jnp.float32
BF16 = jnp.bfloat16
HIGHEST = lax.Precision.HIGHEST

RET_HEADS = 4
RET_DK = 128
RET_DV = 256
RET_BATCH = 2
GDN_HEADS = 4
GDN_DK = 128
GDN_DV = 256
GDN_CHUNK = 64
GDN_BATCH = 2
CONV_K = 4
N_GROUPS = 4
EXPERTS_PER_GROUP = 4
N_EXPERTS = N_GROUPS * EXPERTS_PER_GROUP
ROPE_BASE = 10000.0
NORM_EPS = 1e-6
L2_EPS = 1e-6
N_MOD = 6
LANES = 128
SUBLANES = 8
SEQ_TILE = 256
VMEM_LIMIT = 48 * 1024 * 1024
MOE_TILE = 1024
MOE_ROW_ALIGN = 16
MOE_SORTED = MOE_TILE + N_GROUPS * MOE_ROW_ALIGN
MOE_PIECE = 512
MOE_BUCKET = 64
MOE_MIN_PIECE = 128
MOE_ROWS = MOE_SORTED + MOE_PIECE
MOE_EXPERTS_PER_STEP = 2
MOE_VMEM_LIMIT = 56 * 1024 * 1024
INPROJ_VMEM_LIMIT = 56 * 1024 * 1024

COL_RQ, COL_RK, COL_RV, COL_RG = 0, 512, 1024, 2048
COL_GQ, COL_GK, COL_GV, COL_GZ = 3072, 3584, 4096, 5120
COL_GATE_A, COL_GATE_B = 6144, 7168
N_MAIN = 8192


def _sigmoid(x):
    return 0.5 * jnp.tanh(0.5 * x) + 0.5


def _silu(x):
    h = 0.5 * x
    return h + h * jnp.tanh(h)


def _dot(a, b, **kw):
    return jnp.dot(a, b, preferred_element_type=F32, **kw)


def _dot_nt(a, b, **kw):
    return lax.dot_general(a, b, (((1,), (1,)), ((), ())), preferred_element_type=F32, **kw)


def _dot_tn(a, b, **kw):
    return lax.dot_general(a, b, (((0,), (0,)), ((), ())), preferred_element_type=F32, **kw)


def _params(*sem):
    return pltpu.CompilerParams(dimension_semantics=sem, vmem_limit_bytes=VMEM_LIMIT)


def _mod_kernel(c_ref, w_ref, b_ref, o_ref):
    a = _silu(c_ref[...])
    o_ref[...] = _dot(a, w_ref[...], precision=HIGHEST) + b_ref[...]


def _mod(c, mod_w, mod_b):
    b, d = c.shape
    n = mod_w.shape[1]
    tn = d
    return pl.pallas_call(
        _mod_kernel,
        grid=(n // tn,),
        in_specs=[
            pl.BlockSpec((b, d), lambda j: (0, 0)),
            pl.BlockSpec((d, tn), lambda j: (0, j)),
            pl.BlockSpec((1, tn), lambda j: (0, j)),
        ],
        out_specs=pl.BlockSpec((b, tn), lambda j: (0, j)),
        out_shape=jax.ShapeDtypeStruct((b, n), F32),
        compiler_params=_params("arbitrary"),
        name="mod",
    )(c, mod_w, mod_b.reshape(1, n))


def _inproj_kernel(x0_ref, shift0_ref, scale0_ref, xn_ref, shiftn_ref, scalen_ref, nw_ref, w_ref, ws_ref,
                   o_ref, og_ref, xb_even_ref, xb_odd_ref):
    i = pl.program_id(0)
    j = pl.program_id(1)
    tm = x0_ref.shape[0]
    slab = tm // pl.num_programs(1)

    def prepare(x, shift, scale):
        y = x * lax.rsqrt(jnp.mean(x * x, axis=-1, keepdims=True) + NORM_EPS) * nw_ref[...]
        return (y * (1.0 + scale) + shift).astype(BF16)

    @pl.when(jnp.logical_and(i == 0, j == 0))
    def _():
        xb_even_ref[...] = prepare(x0_ref[...], shift0_ref[...], scale0_ref[...])

    def step(cur_ref, nxt_ref):
        rows = pl.ds(pl.multiple_of(j * slab, slab), slab)
        nxt_ref[rows, :] = prepare(xn_ref[rows, :], shiftn_ref[...], scalen_ref[...])
        xb = cur_ref[...]
        o_ref[...] = _dot(xb, w_ref[...]).astype(BF16)

        @pl.when(j == 0)
        def _():
            og_ref[...] = _dot(xb, ws_ref[...])

    @pl.when(i % 2 == 0)
    def _():
        step(xb_even_ref, xb_odd_ref)

    @pl.when(i % 2 == 1)
    def _():
        step(xb_odd_ref, xb_even_ref)


def _inproj(x2, mod4, norm_w, w_main, w_small, seq, tm=1024, tn=4096):
    t, d = x2.shape
    n = w_main.shape[1]
    per_b = seq // tm
    last = t // tm - 1

    def nxt(i):
        return jnp.minimum(i + 1, last)

    return pl.pallas_call(
        _inproj_kernel,
        grid=(t // tm, n // tn),
        in_specs=[
            pl.BlockSpec((tm, d), lambda i, j: (0, 0), pipeline_mode=pl.Buffered(1)),
            pl.BlockSpec((None, None, 1, d), lambda i, j: (0, 0, 0, 0)),
            pl.BlockSpec((None, None, 1, d), lambda i, j: (0, 1, 0, 0)),
            pl.BlockSpec((tm, d), lambda i, j: (nxt(i), 0)),
            pl.BlockSpec((None, None, 1, d), lambda i, j: (nxt(i) // per_b, 0, 0, 0)),
            pl.BlockSpec((None, None, 1, d), lambda i, j: (nxt(i) // per_b, 1, 0, 0)),
            pl.BlockSpec((1, d), lambda i, j: (0, 0)),
            pl.BlockSpec((d, tn), lambda i, j: (0, j)),
            pl.BlockSpec((d, LANES), lambda i, j: (0, 0)),
        ],
        out_specs=[
            pl.BlockSpec((tm, tn), lambda i, j: (i, j)),
            pl.BlockSpec((tm, LANES), lambda i, j: (i, 0)),
        ],
        out_shape=[
            jax.ShapeDtypeStruct((t, n), BF16),
            jax.ShapeDtypeStruct((t, LANES), F32),
        ],
        scratch_shapes=[pltpu.VMEM((tm, d), BF16), pltpu.VMEM((tm, d), BF16)],
        compiler_params=pltpu.CompilerParams(dimension_semantics=("arbitrary", "arbitrary"),
                                             vmem_limit_bytes=INPROJ_VMEM_LIMIT),
        name="inproj",
    )(x2, mod4, mod4, x2, mod4, mod4, norm_w.reshape(1, d), w_main, w_small.astype(BF16))


def _ret_kernel(lg_ref, q_ref, k_ref, v_ref, rg_ref, ga_ref, cos_ref, sin_ref, o_ref,
                state_ref, intra_ref, qd_ref, kd_ref):
    c = SEQ_TILE
    first = jnp.logical_and(pl.program_id(0) == 0, pl.program_id(1) == 0)

    @pl.when(first)
    def _():
        row = lax.broadcasted_iota(jnp.int32, (c, c), 0)
        col = lax.broadcasted_iota(jnp.int32, (c, c), 1)
        rel = (row - col).astype(F32)
        causal = row >= col
        pos = lax.broadcasted_iota(jnp.int32, (c, RET_DK), 0).astype(F32)
        for h in range(RET_HEADS):
            lg = lg_ref[h:h + 1, :]
            intra_ref[h] = jnp.where(causal, jnp.exp(jnp.where(causal, rel, 0.0) * lg), 0.0)
            qd_ref[h] = jnp.exp((pos + 1.0) * lg[:, :RET_DK])
            kd_ref[h] = jnp.exp((c - 1.0 - pos) * lg[:, :RET_DK])

    @pl.when(pl.program_id(1) == 0)
    def _():
        state_ref[...] = jnp.zeros_like(state_ref)

    cos = cos_ref[...]
    sin = sin_ref[...]
    for s in range(RET_BATCH):
        for h in range(RET_HEADS):
            qs = slice(h * RET_DK, (h + 1) * RET_DK)
            vs = slice(h * RET_DV, (h + 1) * RET_DV)
            qr = q_ref[s, :, qs].astype(F32)
            kr = k_ref[s, :, qs].astype(F32)
            q = qr * cos + pltpu.roll(qr, RET_DK // 2, 1) * sin
            k = (kr * cos + pltpu.roll(kr, RET_DK // 2, 1) * sin) * (RET_DK ** -0.5)
            v = v_ref[s, :, vs]
            state = state_ref[s, h]
            chunk_decay = jnp.exp(float(c) * lg_ref[h:h + 1, :])
            scores = _dot_nt(q.astype(BF16), k.astype(BF16)) * intra_ref[h]
            o = _dot(scores.astype(BF16), v) + _dot((q * qd_ref[h]).astype(BF16), state.astype(BF16))
            state_ref[s, h] = state * chunk_decay + _dot_tn((k * kd_ref[h]).astype(BF16), v)
            o = o * lax.rsqrt(jnp.mean(o * o, axis=-1, keepdims=True) + NORM_EPS)
            y = _silu(rg_ref[s, :, vs].astype(F32)) * o
            o_ref[s, :, vs] = (_sigmoid(ga_ref[s, :, vs].astype(F32)) * y).astype(BF16)


def _retention(proj3, cos_t, sin_t, log_gamma):
    b, s, _ = proj3.shape
    ts = SEQ_TILE
    nb = RET_BATCH
    assert b % nb == 0
    qk_w = RET_HEADS * RET_DK
    v_w = RET_HEADS * RET_DV
    return pl.pallas_call(
        _ret_kernel,
        grid=(b // nb, s // ts),
        in_specs=[
            pl.BlockSpec((RET_HEADS, RET_DV), lambda i, j: (0, 0)),
            pl.BlockSpec((nb, ts, qk_w), lambda i, j: (i, j, COL_RQ // qk_w)),
            pl.BlockSpec((nb, ts, qk_w), lambda i, j: (i, j, COL_RK // qk_w)),
            pl.BlockSpec((nb, ts, v_w), lambda i, j: (i, j, COL_RV // v_w)),
            pl.BlockSpec((nb, ts, v_w), lambda i, j: (i, j, COL_RG // v_w)),
            pl.BlockSpec((nb, ts, v_w), lambda i, j: (i, j, COL_GATE_A // v_w)),
            pl.BlockSpec((ts, RET_DK), lambda i, j: (j, 0)),
            pl.BlockSpec((ts, RET_DK), lambda i, j: (j, 0)),
        ],
        out_specs=pl.BlockSpec((nb, ts, v_w), lambda i, j: (i, j, 0)),
        out_shape=jax.ShapeDtypeStruct((b, s, v_w), BF16),
        scratch_shapes=[
            pltpu.VMEM((nb, RET_HEADS, RET_DK, RET_DV), F32),
            pltpu.VMEM((RET_HEADS, ts, ts), F32),
            pltpu.VMEM((RET_HEADS, ts, RET_DK), F32),
            pltpu.VMEM((RET_HEADS, ts, RET_DK), F32),
        ],
        compiler_params=_params("arbitrary", "arbitrary"),
        name="retention",
    )(log_gamma, proj3, proj3, proj3, proj3, proj3, cos_t, sin_t)


def _gdn_kernel(q_ref, k_ref, v_ref, z_ref, gb_ref, gates_ref, alog_ref, dtb_ref, cwq_ref, cwk_ref, cwv_ref,
                nw_ref, o_ref, state_ref, tail_ref):
    ts = SEQ_TILE
    cc = GDN_CHUNK
    qk_w = GDN_HEADS * GDN_DK

    @pl.when(pl.program_id(1) == 0)
    def _():
        state_ref[...] = jnp.zeros_like(state_ref)
        tail_ref[...] = jnp.zeros_like(tail_ref)

    def conv_silu(cur, tail, cw_ref):
        assert CONV_K == 4
        ext = jnp.concatenate([tail, cur], axis=0)
        ext1 = pltpu.roll(ext, 1, 0)
        near = ext * cw_ref[3:4, :] + ext1 * cw_ref[2:3, :]
        far = ext * cw_ref[1:2, :] + ext1 * cw_ref[0:1, :]
        return _silu((near + pltpu.roll(far, 2, 0))[SUBLANES:])

    row = lax.broadcasted_iota(jnp.int32, (ts, ts), 0)
    col = lax.broadcasted_iota(jnp.int32, (ts, ts), 1)
    same_chunk = (row // cc) == (col // cc)
    causal = jnp.logical_and(same_chunk, row >= col)
    strict = jnp.logical_and(same_chunk, row > col)
    eye = jnp.where(row == col, 1.0, 0.0)
    tri_lower = jnp.where(causal, 1.0, 0.0).astype(BF16)
    tri_upper = jnp.where(jnp.logical_and(same_chunk, row <= col), 1.0, 0.0).astype(BF16)

    units = [(s, h) for s in range(GDN_BATCH) for h in range(GDN_HEADS)]
    q_all, k_all, v_all, beta_all, gc_col_all, gc_row_all = [], [], [], [], [], []
    for s in range(GDN_BATCH):
        q_raw = q_ref[s].astype(F32)
        k_raw = k_ref[s].astype(F32)
        v_raw = v_ref[s].astype(F32)
        q_all.append(conv_silu(q_raw, tail_ref[s, :, 0:qk_w], cwq_ref))
        k_all.append(conv_silu(k_raw, tail_ref[s, :, qk_w:2 * qk_w], cwk_ref))
        v_all.append(conv_silu(v_raw, tail_ref[s, :, 2 * qk_w:], cwv_ref))
        tail_ref[s, :, 0:qk_w] = q_raw[ts - SUBLANES:]
        tail_ref[s, :, qk_w:2 * qk_w] = k_raw[ts - SUBLANES:]
        tail_ref[s, :, 2 * qk_w:] = v_raw[ts - SUBLANES:]

        gates = gates_ref[s]
        x = gates + dtb_ref[...]
        softplus = jnp.maximum(x, 0.0) + jnp.log1p(jnp.exp(-jnp.abs(x)))
        g_all = -jnp.exp(alog_ref[...]) * softplus
        beta_all.append(_sigmoid(gates))
        g_hi = g_all.astype(BF16)
        g_rest = g_all - g_hi.astype(F32)
        g_mid = g_rest.astype(BF16)
        g_lo = (g_rest - g_mid.astype(F32)).astype(BF16)
        gc_col_all.append(sum(_dot(tri_lower, part) for part in (g_hi, g_mid, g_lo)))
        gc_row_all.append(sum(_dot_tn(part, tri_upper) for part in (g_hi, g_mid, g_lo)))

    qn, kn, k16, k_beta, gc, decay, beta = {}, {}, {}, {}, {}, {}, {}
    for u in units:
        s, h = u
        qs = slice(h * GDN_DK, (h + 1) * GDN_DK)
        qh = q_all[s][:, qs]
        kh = k_all[s][:, qs]
        qn[u] = qh * lax.rsqrt(jnp.sum(qh * qh, axis=-1, keepdims=True) + L2_EPS) * (GDN_DK ** -0.5)
        kn[u] = kh * lax.rsqrt(jnp.sum(kh * kh, axis=-1, keepdims=True) + L2_EPS)
        gc[u] = gc_col_all[s][:, h:h + 1]
        gcr = gc_row_all[s][h:h + 1, :]
        decay[u] = jnp.exp(jnp.where(causal, gc[u] - gcr, -jnp.inf))
        beta[u] = beta_all[s][:, GDN_HEADS + h:GDN_HEADS + h + 1]
        k_beta[u] = kn[u] * beta[u]
        k16[u] = kn[u].astype(BF16)
    a_mat = {u: jnp.where(strict, _dot_nt(k_beta[u].astype(BF16), k16[u]) * decay[u], 0.0) for u in units}
    p = {u: eye - a_mat[u] for u in units}
    m16 = {u: a_mat[u].astype(BF16) for u in units}
    for _ in range((cc - 1).bit_length() - 1):
        m16 = {u: _dot(m16[u], m16[u]).astype(BF16) for u in units}
        p = {u: p[u] + _dot(p[u].astype(BF16), m16[u]) for u in units}
    u_all, lhs_state, lhs_vnew, state_decay = {}, {}, {}, {}
    for u in units:
        s, h = u
        vs = slice(h * GDN_DV, (h + 1) * GDN_DV)
        t16 = p[u].astype(BF16)
        egc = jnp.exp(gc[u])
        u_all[u] = _dot(t16, (v_all[s][:, vs] * beta[u]).astype(BF16))
        w_all = _dot(t16, (k_beta[u] * egc).astype(BF16))
        qk = _dot_nt(qn[u].astype(BF16), k16[u]) * decay[u]
        qg = qn[u] * egc
        kn_t = kn[u].T
        gcr = gc_row_all[s][h:h + 1, :]
        for n in range(ts // cc):
            rs = slice(n * cc, (n + 1) * cc)
            g_last = gc[u][(n + 1) * cc - 1:(n + 1) * cc, :]
            kg_t = kn_t[:, rs] * jnp.exp(g_last - gcr[:, rs])
            lhs_state[u, n] = jnp.concatenate([w_all[rs], qg[rs]], axis=0).astype(BF16)
            lhs_vnew[u, n] = jnp.concatenate([qk[rs, rs], kg_t], axis=0).astype(BF16)
            state_decay[u, n] = jnp.exp(g_last)
    state = {u: state_ref[u[0], u[1]] for u in units}
    outs = {u: [] for u in units}
    for n in range(ts // cc):
        rs = slice(n * cc, (n + 1) * cc)
        for u in units:
            from_state = _dot(lhs_state[u, n], state[u].astype(BF16))
            v_new = (u_all[u][rs] - from_state[:cc]).astype(BF16)
            from_vnew = _dot(lhs_vnew[u, n], v_new)
            outs[u].append(from_state[cc:] + from_vnew[:cc])
            state[u] = state[u] * state_decay[u, n] + from_vnew[cc:]
    for u in units:
        s, h = u
        vs = slice(h * GDN_DV, (h + 1) * GDN_DV)
        state_ref[s, h] = state[u]
        o = jnp.concatenate(outs[u], axis=0)
        o = o * lax.rsqrt(jnp.mean(o * o, axis=-1, keepdims=True) + NORM_EPS) * nw_ref[...]
        o = o * _silu(z_ref[s, :, vs].astype(F32))
        o_ref[s, :, vs] = (_sigmoid(gb_ref[s, :, vs].astype(F32)) * o).astype(BF16)


def _gdn(proj3, gates3, alog_row, dtb_row, conv_w, norm_w):
    b, s, _ = proj3.shape
    ts = SEQ_TILE
    nb = GDN_BATCH
    assert b % nb == 0
    qk_w = GDN_HEADS * GDN_DK
    v_w = GDN_HEADS * GDN_DV
    return pl.pallas_call(
        _gdn_kernel,
        grid=(b // nb, s // ts),
        in_specs=[
            pl.BlockSpec((nb, ts, qk_w), lambda i, j: (i, j, COL_GQ // qk_w)),
            pl.BlockSpec((nb, ts, qk_w), lambda i, j: (i, j, COL_GK // qk_w)),
            pl.BlockSpec((nb, ts, v_w), lambda i, j: (i, j, COL_GV // v_w)),
            pl.BlockSpec((nb, ts, v_w), lambda i, j: (i, j, COL_GZ // v_w)),
            pl.BlockSpec((nb, ts, v_w), lambda i, j: (i, j, COL_GATE_B // v_w)),
            pl.BlockSpec((nb, ts, LANES), lambda i, j: (i, j, 0)),
            pl.BlockSpec((1, LANES), lambda i, j: (0, 0)),
            pl.BlockSpec((1, LANES), lambda i, j: (0, 0)),
            pl.BlockSpec((CONV_K, qk_w), lambda i, j: (0, 0)),
            pl.BlockSpec((CONV_K, qk_w), lambda i, j: (0, 1)),
            pl.BlockSpec((CONV_K, v_w), lambda i, j: (0, 1)),
            pl.BlockSpec((1, GDN_DV), lambda i, j: (0, 0)),
        ],
        out_specs=pl.BlockSpec((nb, ts, v_w), lambda i, j: (i, j, 0)),
        out_shape=jax.ShapeDtypeStruct((b, s, v_w), BF16),
        scratch_shapes=[
            pltpu.VMEM((nb, GDN_HEADS, GDN_DK, GDN_DV), F32),
            pltpu.VMEM((nb, SUBLANES, 2 * qk_w + v_w), F32),
        ],
        compiler_params=_params("arbitrary", "arbitrary"),
        name="gdn",
    )(proj3, proj3, proj3, proj3, proj3, gates3, alog_row, dtb_row, conv_w, conv_w, conv_w,
      norm_w.reshape(1, GDN_DV))


def _route(logits):
    lane = lax.broadcasted_iota(jnp.int32, logits.shape, 1)
    neg = jnp.float32(-jnp.inf)

    def first_max(v):
        top = jnp.max(v, axis=-1, keepdims=True)
        return top, jnp.min(jnp.where(v == top, lane, LANES), axis=-1, keepdims=True)

    gl = jnp.where(lane < N_GROUPS, logits, neg)
    g_max, g_idx = first_max(gl)
    g_top = 1.0 / jnp.sum(jnp.exp(gl - g_max), axis=-1, keepdims=True)
    lo = N_GROUPS + EXPERTS_PER_GROUP * g_idx
    el = jnp.where(jnp.logical_and(lane >= lo, lane < lo + EXPERTS_PER_GROUP), logits, neg)
    top1, i1 = first_max(el)
    top2, i2 = first_max(jnp.where(lane == i1, neg, el))
    gap = jnp.exp(top2 - top1)
    w1 = 1.0 / (1.0 + gap)
    comb = jnp.where(lane == i1, g_top * w1, jnp.where(lane == i2, g_top * (gap * w1), 0.0))
    return jnp.where(lane == 0, g_idx.astype(F32), comb)


def _outproj_kernel(ya_ref, yb_ref, x_ref, w_ref, gate_ref, nw_ref, shift_ref, scale_ref, wr_ref, wrh_ref, br_ref,
                    h_ref, xf_ref, comb_ref):
    tm = x_ref.shape[0]
    halves = [slice(k * (tm // 2), (k + 1) * (tm // 2)) for k in range(2)]
    hs, xfs, logits = [], [], []
    for rs in halves:
        merged = ya_ref[rs, :] + yb_ref[rs, :]
        hs.append(x_ref[rs, :] + gate_ref[...] * _dot(merged, w_ref[...]))
    for rs, h in zip(halves, hs):
        h_ref[rs, :] = h
        y = h * lax.rsqrt(jnp.mean(h * h, axis=-1, keepdims=True) + NORM_EPS) * nw_ref[...]
        xfs.append(y * (1.0 + scale_ref[...]) + shift_ref[...])
    for rs, xf in zip(halves, xfs):
        xf_hi = xf.astype(BF16)
        xf_ref[rs, :] = xf_hi
        xf_lo = (xf - xf_hi.astype(F32)).astype(BF16)
        both = _dot(xf_hi, wr_ref[...])
        logits.append(both[:, :LANES] + both[:, LANES:] + _dot(xf_lo, wrh_ref[...]) + br_ref[...])
    for rs, lg in zip(halves, logits):
        comb_ref[rs, :] = _route(lg)


def _outproj(ya2, yb2, x2, w_out_b, mod4, norm_w, w_route, b_route, seq, tm=1024):
    t, d = x2.shape
    per_b = seq // tm
    row_spec = pl.BlockSpec((tm, d), lambda i: (i, 0))
    wr_hi = w_route.astype(BF16)
    wr_lo = (w_route - wr_hi.astype(F32)).astype(BF16)

    def mod_spec(k):
        return pl.BlockSpec((None, None, 1, d), lambda i: (i // per_b, k, 0, 0))

    return pl.pallas_call(
        _outproj_kernel,
        grid=(t // tm,),
        in_specs=[
            row_spec, row_spec, row_spec,
            pl.BlockSpec((d, d), lambda i: (0, 0)),
            mod_spec(2),
            pl.BlockSpec((1, d), lambda i: (0, 0)),
            mod_spec(3),
            mod_spec(4),
            pl.BlockSpec((d, 2 * LANES), lambda i: (0, 0)),
            pl.BlockSpec((d, LANES), lambda i: (0, 0)),
            pl.BlockSpec((1, LANES), lambda i: (0, 0)),
        ],
        out_specs=[row_spec, row_spec, pl.BlockSpec((tm, LANES), lambda i: (i, 0))],
        out_shape=[
            jax.ShapeDtypeStruct((t, d), F32),
            jax.ShapeDtypeStruct((t, d), BF16),
            jax.ShapeDtypeStruct((t, LANES), F32),
        ],
        compiler_params=_params("arbitrary"),
        name="outproj",
    )(ya2, yb2, x2, w_out_b, mod4, norm_w.reshape(1, d), mod4, mod4,
      jnp.concatenate([wr_hi, wr_lo], axis=1), wr_hi, b_route)


def _moe_kernel(xf_ref, comb_ref, wg_ref, wu_ref, wd_ref, h_ref, gate_ref, nw_ref, o_ref,
                xs_ref, cs_ref, acc_ref, pos_ref, ltri_ref, meta_ref):
    i = pl.program_id(0)
    p = pl.program_id(1)
    tm = MOE_TILE
    steps_per_group = EXPERTS_PER_GROUP // MOE_EXPERTS_PER_STEP
    g = p // steps_per_group

    @pl.when(jnp.logical_and(i == 0, p == 0))
    def _():
        row = lax.broadcasted_iota(jnp.int32, (tm, tm), 0)
        col = lax.broadcasted_iota(jnp.int32, (tm, tm), 1)
        ltri_ref[...] = jnp.where(row >= col, 1.0, 0.0).astype(BF16)
        xs_ref[MOE_SORTED:, :] = jnp.zeros((MOE_ROWS - MOE_SORTED, xs_ref.shape[1]), BF16)
        cs_ref[MOE_SORTED:, :] = jnp.zeros((MOE_ROWS - MOE_SORTED, LANES), F32)
        acc_ref[MOE_SORTED:, :] = jnp.zeros((MOE_ROWS - MOE_SORTED, acc_ref.shape[1]), F32)

    @pl.when(p == 0)
    def _():
        comb = comb_ref[...]
        lane = lax.broadcasted_iota(jnp.int32, (tm, LANES), 1)
        gidx = comb[:, 0:1]
        mine = jnp.logical_and(lane.astype(F32) == gidx, lane < N_GROUPS)
        csum = _dot(ltri_ref[...], jnp.where(mine, 1.0, 0.0).astype(BF16))
        counts = jnp.broadcast_to(csum[tm - 1:tm, :], (SUBLANES, LANES))
        aligned = jnp.floor((counts + (MOE_ROW_ALIGN - 0.5)) * (1.0 / MOE_ROW_ALIGN)) * MOE_ROW_ALIGN
        lr = lax.broadcasted_iota(jnp.int32, (LANES, LANES), 0)
        lc = lax.broadcasted_iota(jnp.int32, (LANES, LANES), 1)
        seg_start = _dot(aligned, jnp.where(lr < lc, 1.0, 0.0), precision=HIGHEST)
        pos = jnp.sum(jnp.where(mine, seg_start[0:1, :] + csum - 1.0, 0.0), axis=-1, keepdims=True)
        pos_b = jnp.broadcast_to(pos, (tm, LANES))
        pos_ref[...] = pos_b.astype(jnp.int32)
        pos_row = pos_b.T[0:1, :].astype(jnp.int32)
        lane1 = lax.broadcasted_iota(jnp.int32, (SUBLANES, LANES), 1)
        for k in range(N_GROUPS):
            meta_ref[k] = jnp.sum(jnp.where(lane1 == k, counts, 0.0)[0:1, :]).astype(jnp.int32)
            meta_ref[N_GROUPS + k] = jnp.sum(jnp.where(lane1 == k, seg_start, 0.0)[0:1, :]).astype(jnp.int32)

        c_hi = comb.astype(BF16)
        c_lo = (comb - c_hi.astype(F32)).astype(BF16)
        sub = lax.broadcasted_iota(jnp.int32, (MOE_SORTED, tm), 0)
        perm = jnp.where(pos_row == sub, 1.0, 0.0).astype(BF16)
        xs_ref[0:MOE_SORTED, :] = _dot(perm, xf_ref[...]).astype(BF16)
        cs_ref[0:MOE_SORTED, :] = _dot(perm, c_hi) + _dot(perm, c_lo)
        acc_ref[0:MOE_SORTED, :] = jnp.zeros((MOE_SORTED, acc_ref.shape[1]), F32)

    def expert_piece(start, m):
        rows = pl.ds(pl.multiple_of(start, MOE_ROW_ALIGN), m)
        xs = xs_ref[rows, :]
        cs = cs_ref[rows, :]
        lane_m = lax.broadcasted_iota(jnp.int32, (m, LANES), 1)
        contrib = None
        for j in range(MOE_EXPERTS_PER_STEP):
            e = p * MOE_EXPERTS_PER_STEP + j
            wcol = jnp.sum(jnp.where(lane_m == N_GROUPS + e, cs, 0.0), axis=-1, keepdims=True)
            act = _silu(_dot(xs, wg_ref[j])) * _dot(xs, wu_ref[j]) * wcol
            part = _dot(act.astype(BF16), wd_ref[j])
            contrib = part if contrib is None else contrib + part
        acc_ref[rows, :] += contrib

    count = meta_ref[g]
    seg = meta_ref[N_GROUPS + g]
    n_full = jnp.maximum(count - 1, 0) // MOE_PIECE

    def full_piece(k, carry):
        expert_piece(seg + k * MOE_PIECE, MOE_PIECE)
        return carry

    lax.fori_loop(0, n_full, full_piece, 0)
    last_start = seg + n_full * MOE_PIECE
    last_rows = count - n_full * MOE_PIECE
    for m in range(MOE_MIN_PIECE, MOE_PIECE + 1, MOE_BUCKET):
        lo = 0 if m == MOE_MIN_PIECE else m - MOE_BUCKET

        @pl.when(jnp.logical_and(last_rows > lo, last_rows <= m))
        def _(m=m):
            expert_piece(last_start, m)

    @pl.when(p == pl.num_programs(1) - 1)
    def _():
        pos = pos_ref[:, 0:1]
        lane_s = lax.broadcasted_iota(jnp.int32, (tm, MOE_SORTED), 1)
        inv = jnp.where(pos == lane_s, 1.0, 0.0).astype(BF16)
        moe = _dot(inv, acc_ref[0:MOE_SORTED, :].astype(BF16))
        h = h_ref[...] + gate_ref[...] * moe
        o_ref[...] = h * lax.rsqrt(jnp.mean(h * h, axis=-1, keepdims=True) + NORM_EPS) * nw_ref[...]


def _moe(xf2, comb, wg, wu, wd, h2, mod4, norm_out_w, seq):
    t, d = xf2.shape
    f = wg.shape[-1]
    tm = MOE_TILE
    per_b = seq // tm
    eps = MOE_EXPERTS_PER_STEP
    row_spec = pl.BlockSpec((tm, d), lambda i, p: (i, 0))
    return pl.pallas_call(
        _moe_kernel,
        grid=(t // tm, N_EXPERTS // eps),
        in_specs=[
            row_spec,
            pl.BlockSpec((tm, LANES), lambda i, p: (i, 0)),
            pl.BlockSpec((eps, d, f), lambda i, p: (p, 0, 0)),
            pl.BlockSpec((eps, d, f), lambda i, p: (p, 0, 0)),
            pl.BlockSpec((eps, f, d), lambda i, p: (p, 0, 0)),
            row_spec,
            pl.BlockSpec((None, None, 1, d), lambda i, p: (i // per_b, 5, 0, 0)),
            pl.BlockSpec((1, d), lambda i, p: (0, 0)),
        ],
        out_specs=row_spec,
        out_shape=jax.ShapeDtypeStruct((t, d), F32),
        scratch_shapes=[
            pltpu.VMEM((MOE_ROWS, d), BF16),
            pltpu.VMEM((MOE_ROWS, LANES), F32),
            pltpu.VMEM((MOE_ROWS, d), F32),
            pltpu.VMEM((tm, LANES), jnp.int32),
            pltpu.VMEM((tm, tm), BF16),
            pltpu.SMEM((2 * N_GROUPS,), jnp.int32),
        ],
        compiler_params=pltpu.CompilerParams(dimension_semantics=("arbitrary", "arbitrary"),
                                             vmem_limit_bytes=MOE_VMEM_LIMIT),
        name="moe",
    )(xf2, comb, wg, wu, wd, h2, mod4, norm_out_w.reshape(1, d))


def _pad_lanes(a):
    return jnp.pad(a, ((0, 0), (0, LANES - a.shape[1])))


def _layer(h3, c, mod_w, mod_b, norm_mix_w, w_in, conv_w, a_log, dt_bias, gdn_norm_w, w_out, norm_ffn_w,
           w_group, b_group, w_router, b_router, w_gate, w_up, w_down, norm_out_w):
    b, s, d = h3.shape
    t = b * s
    x2 = h3.reshape(t, d)

    n_gate_cols = 2 * GDN_HEADS
    small_lo = COL_GATE_A
    w_main = jnp.concatenate([w_in[:, :small_lo], w_in[:, small_lo + n_gate_cols:]], axis=1).astype(BF16)
    w_small = _pad_lanes(w_in[:, small_lo:small_lo + n_gate_cols])
    w_route = _pad_lanes(jnp.concatenate([w_group, w_router], axis=1))
    b_route = _pad_lanes(jnp.concatenate([b_group, b_router.reshape(-1)])[None, :])
    alog_row = _pad_lanes(a_log[None, :])
    dtb_row = _pad_lanes(dt_bias[None, :])
    f = w_gate.shape[-1]
    wg = w_gate.reshape(N_EXPERTS, d, f).astype(BF16)
    wu = w_up.reshape(N_EXPERTS, d, f).astype(BF16)
    wd = w_down.reshape(N_EXPERTS, f, d).astype(BF16)

    half = RET_DK // 2
    inv_freq = 1.0 / (ROPE_BASE ** (jnp.arange(half, dtype=F32) / half))
    ang = jnp.arange(s, dtype=F32)[:, None] * inv_freq[None, :]
    cos_t = jnp.concatenate([jnp.cos(ang), jnp.cos(ang)], axis=1)
    sin_t = jnp.concatenate([-jnp.sin(ang), jnp.sin(ang)], axis=1)
    log_gamma = jnp.log(1.0 - 2.0 ** (-5.0 - jnp.arange(RET_HEADS, dtype=F32)))
    log_gamma = jnp.broadcast_to(log_gamma[:, None], (RET_HEADS, RET_DV))

    mod4 = _mod(c, mod_w, mod_b).reshape(b, N_MOD, 1, d)
    proj, gates = _inproj(x2, mod4, norm_mix_w, w_main, w_small, s)
    proj3 = proj.reshape(b, s, N_MAIN)
    ya = _retention(proj3, cos_t, sin_t, log_gamma)
    yb = _gdn(proj3, gates.reshape(b, s, LANES), alog_row, dtb_row, conv_w, gdn_norm_w)
    h2, xf2, comb = _outproj(ya.reshape(t, d), yb.reshape(t, d), x2, w_out.astype(BF16), mod4, norm_ffn_w,
                             w_route, b_route, s)
    return _moe(xf2, comb, wg, wu, wd, h2, mod4, norm_out_w, s).reshape(b, s, d)


def kernel(x, c, mod_w, mod_b, norm_mix_w, w_in, gdn_conv_w, gdn_a_log, gdn_dt_bias, gdn_norm_w, w_out, norm_ffn_w,
           w_group, b_group, w_router, b_router, w_gate, w_up, w_down, norm_out_w):
    assert mod_w.shape[0] == 1, "one residual layer"
    return _layer(x, c, mod_w[0], mod_b[0], norm_mix_w[0], w_in[0], gdn_conv_w[0], gdn_a_log[0], gdn_dt_bias[0],
                  gdn_norm_w[0], w_out[0], norm_ffn_w[0], w_group[0], b_group[0], w_router[0], b_router[0],
                  w_gate[0], w_up[0], w_down[0], norm_out_w)
```

```python
import jax
import jax.numpy as jnp
from jax import lax
from jax.experimental import pallas as pl
from jax.experimental.pallas import tpu as pltpu

F32 = jnp.float32
BF16 = jnp.bfloat16
HIGHEST = lax.Precision.HIGHEST

RET_HEADS = 4
RET_DK = 128
RET_DV = 256
RET_BATCH = 2
GDN_HEADS = 4
GDN_DK = 128
GDN_DV = 256
GDN_CHUNK = 64
GDN_BATCH = 2
CONV_K = 4
N_GROUPS = 4
EXPERTS_PER_GROUP = 4
N_EXPERTS = N_GROUPS * EXPERTS_PER_GROUP
ROPE_BASE = 10000.0
NORM_EPS = 1e-6
L2_EPS = 1e-6
N_MOD = 6
LANES = 128
SUBLANES = 8
SEQ_TILE = 256
VMEM_LIMIT = 48 * 1024 * 1024
MOE_TILE = 1024
MOE_ROW_ALIGN = 16
MOE_SORTED = MOE_TILE + N_GROUPS * MOE_ROW_ALIGN
MOE_PIECE = 512
MOE_BUCKET = 32
MOE_MIN_PIECE = 128
MOE_ROWS = MOE_SORTED + MOE_PIECE
MOE_EXPERTS_PER_STEP = 2
MOE_VMEM_LIMIT = 56 * 1024 * 1024
INPROJ_VMEM_LIMIT = 56 * 1024 * 1024

COL_RQ, COL_RK, COL_RV, COL_RG = 0, 512, 1024, 2048
COL_GQ, COL_GK, COL_GV, COL_GZ = 3072, 3584, 4096, 5120
COL_GATE_A, COL_GATE_B = 6144, 7168
N_MAIN = 8192


def _sigmoid(x):
    return 0.5 * jnp.tanh(0.5 * x) + 0.5


def _silu(x):
    h = 0.5 * x
    return h + h * jnp.tanh(h)


def _dot(a, b, **kw):
    return jnp.dot(a, b, preferred_element_type=F32, **kw)


def _dot_nt(a, b, **kw):
    return lax.dot_general(a, b, (((1,), (1,)), ((), ())), preferred_element_type=F32, **kw)


def _dot_tn(a, b, **kw):
    return lax.dot_general(a, b, (((0,), (0,)), ((), ())), preferred_element_type=F32, **kw)


def _params(*sem):
    return pltpu.CompilerParams(dimension_semantics=sem, vmem_limit_bytes=VMEM_LIMIT)


def _mod_kernel(c_ref, w_ref, b_ref, o_ref):
    a = _silu(c_ref[...])
    o_ref[...] = _dot(a, w_ref[...], precision=HIGHEST) + b_ref[...]


def _mod(c, mod_w, mod_b):
    b, d = c.shape
    n = mod_w.shape[1]
    tn = d
    return pl.pallas_call(
        _mod_kernel,
        grid=(n // tn,),
        in_specs=[
            pl.BlockSpec((b, d), lambda j: (0, 0)),
            pl.BlockSpec((d, tn), lambda j: (0, j)),
            pl.BlockSpec((1, tn), lambda j: (0, j)),
        ],
        out_specs=pl.BlockSpec((b, tn), lambda j: (0, j)),
        out_shape=jax.ShapeDtypeStruct((b, n), F32),
        compiler_params=_params("arbitrary"),
        name="mod",
    )(c, mod_w, mod_b.reshape(1, n))


def _inproj_kernel(x0_ref, shift0_ref, scale0_ref, xn_ref, shiftn_ref, scalen_ref, nw_ref, w_ref, ws_ref,
                   o_ref, og_ref, xb_even_ref, xb_odd_ref):
    i = pl.program_id(0)
    j = pl.program_id(1)
    tm = x0_ref.shape[0]
    slab = tm // pl.num_programs(1)

    def prepare(x, shift, scale):
        y = x * lax.rsqrt(jnp.mean(x * x, axis=-1, keepdims=True) + NORM_EPS) * nw_ref[...]
        return (y * (1.0 + scale) + shift).astype(BF16)

    @pl.when(jnp.logical_and(i == 0, j == 0))
    def _():
        xb_even_ref[...] = prepare(x0_ref[...], shift0_ref[...], scale0_ref[...])

    def step(cur_ref, nxt_ref):
        rows = pl.ds(pl.multiple_of(j * slab, slab), slab)
        nxt_ref[rows, :] = prepare(xn_ref[rows, :], shiftn_ref[...], scalen_ref[...])
        xb = cur_ref[...]
        o_ref[...] = _dot(xb, w_ref[...]).astype(BF16)

        @pl.when(j == 0)
        def _():
            og_ref[...] = _dot(xb, ws_ref[...])

    @pl.when(i % 2 == 0)
    def _():
        step(xb_even_ref, xb_odd_ref)

    @pl.when(i % 2 == 1)
    def _():
        step(xb_odd_ref, xb_even_ref)


def _inproj(x2, mod4, norm_w, w_main, w_small, seq, tm=1024, tn=4096):
    t, d = x2.shape
    n = w_main.shape[1]
    per_b = seq // tm
    last = t // tm - 1

    def nxt(i):
        return jnp.minimum(i + 1, last)

    return pl.pallas_call(
        _inproj_kernel,
        grid=(t // tm, n // tn),
        in_specs=[
            pl.BlockSpec((tm, d), lambda i, j: (0, 0), pipeline_mode=pl.Buffered(1)),
            pl.BlockSpec((None, None, 1, d), lambda i, j: (0, 0, 0, 0)),
            pl.BlockSpec((None, None, 1, d), lambda i, j: (0, 1, 0, 0)),
            pl.BlockSpec((tm, d), lambda i, j: (nxt(i), 0)),
            pl.BlockSpec((None, None, 1, d), lambda i, j: (nxt(i) // per_b, 0, 0, 0)),
            pl.BlockSpec((None, None, 1, d), lambda i, j: (nxt(i) // per_b, 1, 0, 0)),
            pl.BlockSpec((1, d), lambda i, j: (0, 0)),
            pl.BlockSpec((d, tn), lambda i, j: (0, j)),
            pl.BlockSpec((d, LANES), lambda i, j: (0, 0)),
        ],
        out_specs=[
            pl.BlockSpec((tm, tn), lambda i, j: (i, j)),
            pl.BlockSpec((tm, LANES), lambda i, j: (i, 0)),
        ],
        out_shape=[
            jax.ShapeDtypeStruct((t, n), BF16),
            jax.ShapeDtypeStruct((t, LANES), F32),
        ],
        scratch_shapes=[pltpu.VMEM((tm, d), BF16), pltpu.VMEM((tm, d), BF16)],
        compiler_params=pltpu.CompilerParams(dimension_semantics=("arbitrary", "arbitrary"),
                                             vmem_limit_bytes=INPROJ_VMEM_LIMIT),
        name="inproj",
    )(x2, mod4, mod4, x2, mod4, mod4, norm_w.reshape(1, d), w_main, w_small.astype(BF16))


def _ret_kernel(lg_ref, q_ref, k_ref, v_ref, rg_ref, ga_ref, cos_ref, sin_ref, o_ref,
                state_ref, intra_ref, qd_ref, kd_ref):
    c = SEQ_TILE
    first = jnp.logical_and(pl.program_id(0) == 0, pl.program_id(1) == 0)

    @pl.when(first)
    def _():
        row = lax.broadcasted_iota(jnp.int32, (c, c), 0)
        col = lax.broadcasted_iota(jnp.int32, (c, c), 1)
        rel = (row - col).astype(F32)
        causal = row >= col
        pos = lax.broadcasted_iota(jnp.int32, (c, RET_DK), 0).astype(F32)
        for h in range(RET_HEADS):
            lg = lg_ref[h:h + 1, :]
            intra_ref[h] = jnp.where(causal, jnp.exp(jnp.where(causal, rel, 0.0) * lg), 0.0)
            qd_ref[h] = jnp.exp((pos + 1.0) * lg[:, :RET_DK])
            kd_ref[h] = jnp.exp((c - 1.0 - pos) * lg[:, :RET_DK])

    @pl.when(pl.program_id(1) == 0)
    def _():
        state_ref[...] = jnp.zeros_like(state_ref)

    cos = cos_ref[...]
    sin = sin_ref[...]
    for s in range(RET_BATCH):
        for h in range(RET_HEADS):
            qs = slice(h * RET_DK, (h + 1) * RET_DK)
            vs = slice(h * RET_DV, (h + 1) * RET_DV)
            qr = q_ref[s, :, qs].astype(F32)
            kr = k_ref[s, :, qs].astype(F32)
            q = qr * cos + pltpu.roll(qr, RET_DK // 2, 1) * sin
            k = (kr * cos + pltpu.roll(kr, RET_DK // 2, 1) * sin) * (RET_DK ** -0.5)
            v = v_ref[s, :, vs]
            state = state_ref[s, h]
            chunk_decay = jnp.exp(float(c) * lg_ref[h:h + 1, :])
            scores = _dot_nt(q.astype(BF16), k.astype(BF16)) * intra_ref[h]
            o = _dot(scores.astype(BF16), v) + _dot((q * qd_ref[h]).astype(BF16), state.astype(BF16))
            state_ref[s, h] = state * chunk_decay + _dot_tn((k * kd_ref[h]).astype(BF16), v)
            o = o * lax.rsqrt(jnp.mean(o * o, axis=-1, keepdims=True) + NORM_EPS)
            y = _silu(rg_ref[s, :, vs].astype(F32)) * o
            o_ref[s, :, vs] = (_sigmoid(ga_ref[s, :, vs].astype(F32)) * y).astype(BF16)


def _retention(proj3, cos_t, sin_t, log_gamma):
    b, s, _ = proj3.shape
    ts = SEQ_TILE
    nb = RET_BATCH
    assert b % nb == 0
    qk_w = RET_HEADS * RET_DK
    v_w = RET_HEADS * RET_DV
    return pl.pallas_call(
        _ret_kernel,
        grid=(b // nb, s // ts),
        in_specs=[
            pl.BlockSpec((RET_HEADS, RET_DV), lambda i, j: (0, 0)),
            pl.BlockSpec((nb, ts, qk_w), lambda i, j: (i, j, COL_RQ // qk_w)),
            pl.BlockSpec((nb, ts, qk_w), lambda i, j: (i, j, COL_RK // qk_w)),
            pl.BlockSpec((nb, ts, v_w), lambda i, j: (i, j, COL_RV // v_w)),
            pl.BlockSpec((nb, ts, v_w), lambda i, j: (i, j, COL_RG // v_w)),
            pl.BlockSpec((nb, ts, v_w), lambda i, j: (i, j, COL_GATE_A // v_w)),
            pl.BlockSpec((ts, RET_DK), lambda i, j: (j, 0)),
            pl.BlockSpec((ts, RET_DK), lambda i, j: (j, 0)),
        ],
        out_specs=pl.BlockSpec((nb, ts, v_w), lambda i, j: (i, j, 0)),
        out_shape=jax.ShapeDtypeStruct((b, s, v_w), BF16),
        scratch_shapes=[
            pltpu.VMEM((nb, RET_HEADS, RET_DK, RET_DV), F32),
            pltpu.VMEM((RET_HEADS, ts, ts), F32),
            pltpu.VMEM((RET_HEADS, ts, RET_DK), F32),
            pltpu.VMEM((RET_HEADS, ts, RET_DK), F32),
        ],
        compiler_params=_params("arbitrary", "arbitrary"),
        name="retention",
    )(log_gamma, proj3, proj3, proj3, proj3, proj3, cos_t, sin_t)


def _gdn_kernel(q_ref, k_ref, v_ref, z_ref, gb_ref, gates_ref, alog_ref, dtb_ref, cwq_ref, cwk_ref, cwv_ref,
                nw_ref, o_ref, state_ref, tail_ref):
    ts = SEQ_TILE
    cc = GDN_CHUNK
    qk_w = GDN_HEADS * GDN_DK

    @pl.when(pl.program_id(1) == 0)
    def _():
        state_ref[...] = jnp.zeros_like(state_ref)
        tail_ref[...] = jnp.zeros_like(tail_ref)

    def conv_silu(cur, tail, cw_ref):
        assert CONV_K == 4
        ext = jnp.concatenate([tail, cur], axis=0)
        ext1 = pltpu.roll(ext, 1, 0)
        near = ext * cw_ref[3:4, :] + ext1 * cw_ref[2:3, :]
        far = ext * cw_ref[1:2, :] + ext1 * cw_ref[0:1, :]
        return _silu((near + pltpu.roll(far, 2, 0))[SUBLANES:])

    row = lax.broadcasted_iota(jnp.int32, (ts, ts), 0)
    col = lax.broadcasted_iota(jnp.int32, (ts, ts), 1)
    same_chunk = (row // cc) == (col // cc)
    causal = jnp.logical_and(same_chunk, row >= col)
    strict = jnp.logical_and(same_chunk, row > col)
    eye = jnp.where(row == col, 1.0, 0.0)
    tri_lower = jnp.where(causal, 1.0, 0.0).astype(BF16)
    tri_upper = jnp.where(jnp.logical_and(same_chunk, row <= col), 1.0, 0.0).astype(BF16)

    units = [(s, h) for s in range(GDN_BATCH) for h in range(GDN_HEADS)]
    q_all, k_all, v_all, beta_all, gc_col_all, gc_row_all = [], [], [], [], [], []
    for s in range(GDN_BATCH):
        q_raw = q_ref[s].astype(F32)
        k_raw = k_ref[s].astype(F32)
        v_raw = v_ref[s].astype(F32)
        q_all.append(conv_silu(q_raw, tail_ref[s, :, 0:qk_w], cwq_ref))
        k_all.append(conv_silu(k_raw, tail_ref[s, :, qk_w:2 * qk_w], cwk_ref))
        v_all.append(conv_silu(v_raw, tail_ref[s, :, 2 * qk_w:], cwv_ref))
        tail_ref[s, :, 0:qk_w] = q_raw[ts - SUBLANES:]
        tail_ref[s, :, qk_w:2 * qk_w] = k_raw[ts - SUBLANES:]
        tail_ref[s, :, 2 * qk_w:] = v_raw[ts - SUBLANES:]

        gates = gates_ref[s]
        x = gates + dtb_ref[...]
        softplus = jnp.maximum(x, 0.0) + jnp.log1p(jnp.exp(-jnp.abs(x)))
        g_all = -jnp.exp(alog_ref[...]) * softplus
        beta_all.append(_sigmoid(gates))
        g_hi = g_all.astype(BF16)
        g_rest = g_all - g_hi.astype(F32)
        g_mid = g_rest.astype(BF16)
        g_lo = (g_rest - g_mid.astype(F32)).astype(BF16)
        gc_col_all.append(sum(_dot(tri_lower, part) for part in (g_hi, g_mid, g_lo)))
        gc_row_all.append(sum(_dot_tn(part, tri_upper) for part in (g_hi, g_mid, g_lo)))

    qn, kn, k16, k_beta, gc, decay, beta = {}, {}, {}, {}, {}, {}, {}
    for u in units:
        s, h = u
        qs = slice(h * GDN_DK, (h + 1) * GDN_DK)
        qh = q_all[s][:, qs]
        kh = k_all[s][:, qs]
        qn[u] = qh * lax.rsqrt(jnp.sum(qh * qh, axis=-1, keepdims=True) + L2_EPS) * (GDN_DK ** -0.5)
        kn[u] = kh * lax.rsqrt(jnp.sum(kh * kh, axis=-1, keepdims=True) + L2_EPS)
        gc[u] = gc_col_all[s][:, h:h + 1]
        gcr = gc_row_all[s][h:h + 1, :]
        decay[u] = jnp.exp(jnp.where(causal, gc[u] - gcr, -jnp.inf))
        beta[u] = beta_all[s][:, GDN_HEADS + h:GDN_HEADS + h + 1]
        k_beta[u] = kn[u] * beta[u]
        k16[u] = kn[u].astype(BF16)
    a_mat = {u: jnp.where(strict, _dot_nt(k_beta[u].astype(BF16), k16[u]) * decay[u], 0.0) for u in units}
    p = {u: eye - a_mat[u] for u in units}
    m16 = {u: a_mat[u].astype(BF16) for u in units}
    for _ in range((cc - 1).bit_length() - 1):
        m16 = {u: _dot(m16[u], m16[u]).astype(BF16) for u in units}
        p = {u: p[u] + _dot(p[u].astype(BF16), m16[u]) for u in units}
    u_all, lhs_state, lhs_vnew, state_decay = {}, {}, {}, {}
    for u in units:
        s, h = u
        vs = slice(h * GDN_DV, (h + 1) * GDN_DV)
        t16 = p[u].astype(BF16)
        egc = jnp.exp(gc[u])
        u_all[u] = _dot(t16, (v_all[s][:, vs] * beta[u]).astype(BF16))
        w_all = _dot(t16, (k_beta[u] * egc).astype(BF16))
        qk = _dot_nt(qn[u].astype(BF16), k16[u]) * decay[u]
        qg = qn[u] * egc
        kn_t = kn[u].T
        gcr = gc_row_all[s][h:h + 1, :]
        for n in range(ts // cc):
            rs = slice(n * cc, (n + 1) * cc)
            g_last = gc[u][(n + 1) * cc - 1:(n + 1) * cc, :]
            kg_t = kn_t[:, rs] * jnp.exp(g_last - gcr[:, rs])
            lhs_state[u, n] = jnp.concatenate([w_all[rs], qg[rs]], axis=0).astype(BF16)
            lhs_vnew[u, n] = jnp.concatenate([qk[rs, rs], kg_t], axis=0).astype(BF16)
            state_decay[u, n] = jnp.exp(g_last)
    state = {u: state_ref[u[0], u[1]] for u in units}
    outs = {u: [] for u in units}
    for n in range(ts // cc):
        rs = slice(n * cc, (n + 1) * cc)
        for u in units:
            from_state = _dot(lhs_state[u, n], state[u].astype(BF16))
            v_new = (u_all[u][rs] - from_state[:cc]).astype(BF16)
            from_vnew = _dot(lhs_vnew[u, n], v_new)
            outs[u].append(from_state[cc:] + from_vnew[:cc])
            state[u] = state[u] * state_decay[u, n] + from_vnew[cc:]
    for u in units:
        s, h = u
        vs = slice(h * GDN_DV, (h + 1) * GDN_DV)
        state_ref[s, h] = state[u]
        o = jnp.concatenate(outs[u], axis=0)
        o = o * lax.rsqrt(jnp.mean(o * o, axis=-1, keepdims=True) + NORM_EPS) * nw_ref[...]
        o = o * _silu(z_ref[s, :, vs].astype(F32))
        o_ref[s, :, vs] = (_sigmoid(gb_ref[s, :, vs].astype(F32)) * o).astype(BF16)


def _gdn(proj3, gates3, alog_row, dtb_row, conv_w, norm_w):
    b, s, _ = proj3.shape
    ts = SEQ_TILE
    nb = GDN_BATCH
    assert b % nb == 0
    qk_w = GDN_HEADS * GDN_DK
    v_w = GDN_HEADS * GDN_DV
    return pl.pallas_call(
        _gdn_kernel,
        grid=(b // nb, s // ts),
        in_specs=[
            pl.BlockSpec((nb, ts, qk_w), lambda i, j: (i, j, COL_GQ // qk_w)),
            pl.BlockSpec((nb, ts, qk_w), lambda i, j: (i, j, COL_GK // qk_w)),
            pl.BlockSpec((nb, ts, v_w), lambda i, j: (i, j, COL_GV // v_w)),
            pl.BlockSpec((nb, ts, v_w), lambda i, j: (i, j, COL_GZ // v_w)),
            pl.BlockSpec((nb, ts, v_w), lambda i, j: (i, j, COL_GATE_B // v_w)),
            pl.BlockSpec((nb, ts, LANES), lambda i, j: (i, j, 0)),
            pl.BlockSpec((1, LANES), lambda i, j: (0, 0)),
            pl.BlockSpec((1, LANES), lambda i, j: (0, 0)),
            pl.BlockSpec((CONV_K, qk_w), lambda i, j: (0, 0)),
            pl.BlockSpec((CONV_K, qk_w), lambda i, j: (0, 1)),
            pl.BlockSpec((CONV_K, v_w), lambda i, j: (0, 1)),
            pl.BlockSpec((1, GDN_DV), lambda i, j: (0, 0)),
        ],
        out_specs=pl.BlockSpec((nb, ts, v_w), lambda i, j: (i, j, 0)),
        out_shape=jax.ShapeDtypeStruct((b, s, v_w), BF16),
        scratch_shapes=[
            pltpu.VMEM((nb, GDN_HEADS, GDN_DK, GDN_DV), F32),
            pltpu.VMEM((nb, SUBLANES, 2 * qk_w + v_w), F32),
        ],
        compiler_params=_params("arbitrary", "arbitrary"),
        name="gdn",
    )(proj3, proj3, proj3, proj3, proj3, gates3, alog_row, dtb_row, conv_w, conv_w, conv_w,
      norm_w.reshape(1, GDN_DV))


def _route(logits):
    lane = lax.broadcasted_iota(jnp.int32, logits.shape, 1)
    neg = jnp.float32(-jnp.inf)

    def first_max(v):
        top = jnp.max(v, axis=-1, keepdims=True)
        return top, jnp.min(jnp.where(v == top, lane, LANES), axis=-1, keepdims=True)

    gl = jnp.where(lane < N_GROUPS, logits, neg)
    g_max, g_idx = first_max(gl)
    g_top = 1.0 / jnp.sum(jnp.exp(gl - g_max), axis=-1, keepdims=True)
    lo = N_GROUPS + EXPERTS_PER_GROUP * g_idx
    el = jnp.where(jnp.logical_and(lane >= lo, lane < lo + EXPERTS_PER_GROUP), logits, neg)
    top1, i1 = first_max(el)
    top2, i2 = first_max(jnp.where(lane == i1, neg, el))
    gap = jnp.exp(top2 - top1)
    w1 = 1.0 / (1.0 + gap)
    comb = jnp.where(lane == i1, g_top * w1, jnp.where(lane == i2, g_top * (gap * w1), 0.0))
    return jnp.where(lane == 0, g_idx.astype(F32), comb)


def _outproj_kernel(ya_ref, yb_ref, x_ref, w_ref, gate_ref, nw_ref, shift_ref, scale_ref, wr_ref, wrh_ref, br_ref,
                    h_ref, xf_ref, comb_ref):
    tm = x_ref.shape[0]
    halves = [slice(k * (tm // 2), (k + 1) * (tm // 2)) for k in range(2)]
    hs, xfs, logits = [], [], []
    for rs in halves:
        merged = ya_ref[rs, :] + yb_ref[rs, :]
        hs.append(x_ref[rs, :] + gate_ref[...] * _dot(merged, w_ref[...]))
    for rs, h in zip(halves, hs):
        h_ref[rs, :] = h
        y = h * lax.rsqrt(jnp.mean(h * h, axis=-1, keepdims=True) + NORM_EPS) * nw_ref[...]
        xfs.append(y * (1.0 + scale_ref[...]) + shift_ref[...])
    for rs, xf in zip(halves, xfs):
        xf_hi = xf.astype(BF16)
        xf_ref[rs, :] = xf_hi
        xf_lo = (xf - xf_hi.astype(F32)).astype(BF16)
        both = _dot(xf_hi, wr_ref[...])
        logits.append(both[:, :LANES] + both[:, LANES:] + _dot(xf_lo, wrh_ref[...]) + br_ref[...])
    for rs, lg in zip(halves, logits):
        comb_ref[rs, :] = _route(lg)


def _outproj(ya2, yb2, x2, w_out_b, mod4, norm_w, w_route, b_route, seq, tm=1024):
    t, d = x2.shape
    per_b = seq // tm
    row_spec = pl.BlockSpec((tm, d), lambda i: (i, 0))
    wr_hi = w_route.astype(BF16)
    wr_lo = (w_route - wr_hi.astype(F32)).astype(BF16)

    def mod_spec(k):
        return pl.BlockSpec((None, None, 1, d), lambda i: (i // per_b, k, 0, 0))

    return pl.pallas_call(
        _outproj_kernel,
        grid=(t // tm,),
        in_specs=[
            row_spec, row_spec, row_spec,
            pl.BlockSpec((d, d), lambda i: (0, 0)),
            mod_spec(2),
            pl.BlockSpec((1, d), lambda i: (0, 0)),
            mod_spec(3),
            mod_spec(4),
            pl.BlockSpec((d, 2 * LANES), lambda i: (0, 0)),
            pl.BlockSpec((d, LANES), lambda i: (0, 0)),
            pl.BlockSpec((1, LANES), lambda i: (0, 0)),
        ],
        out_specs=[row_spec, row_spec, pl.BlockSpec((tm, LANES), lambda i: (i, 0))],
        out_shape=[
            jax.ShapeDtypeStruct((t, d), F32),
            jax.ShapeDtypeStruct((t, d), BF16),
            jax.ShapeDtypeStruct((t, LANES), F32),
        ],
        compiler_params=_params("arbitrary"),
        name="outproj",
    )(ya2, yb2, x2, w_out_b, mod4, norm_w.reshape(1, d), mod4, mod4,
      jnp.concatenate([wr_hi, wr_lo], axis=1), wr_hi, b_route)


def _moe_kernel(xf_ref, comb_ref, wg_ref, wu_ref, wd_ref, h_ref, gate_ref, nw_ref, o_ref,
                xs_ref, cs_ref, acc_ref, pos_ref, ltri_ref, meta_ref):
    i = pl.program_id(0)
    p = pl.program_id(1)
    tm = MOE_TILE
    steps_per_group = EXPERTS_PER_GROUP // MOE_EXPERTS_PER_STEP
    g = p // steps_per_group

    @pl.when(jnp.logical_and(i == 0, p == 0))
    def _():
        row = lax.broadcasted_iota(jnp.int32, (tm, tm), 0)
        col = lax.broadcasted_iota(jnp.int32, (tm, tm), 1)
        ltri_ref[...] = jnp.where(row >= col, 1.0, 0.0).astype(BF16)
        xs_ref[MOE_SORTED:, :] = jnp.zeros((MOE_ROWS - MOE_SORTED, xs_ref.shape[1]), BF16)
        cs_ref[MOE_SORTED:, :] = jnp.zeros((MOE_ROWS - MOE_SORTED, LANES), F32)
        acc_ref[MOE_SORTED:, :] = jnp.zeros((MOE_ROWS - MOE_SORTED, acc_ref.shape[1]), F32)

    @pl.when(p == 0)
    def _():
        comb = comb_ref[...]
        lane = lax.broadcasted_iota(jnp.int32, (tm, LANES), 1)
        gidx = comb[:, 0:1]
        mine = jnp.logical_and(lane.astype(F32) == gidx, lane < N_GROUPS)
        csum = _dot(ltri_ref[...], jnp.where(mine, 1.0, 0.0).astype(BF16))
        counts = jnp.broadcast_to(csum[tm - 1:tm, :], (SUBLANES, LANES))
        aligned = jnp.floor((counts + (MOE_ROW_ALIGN - 0.5)) * (1.0 / MOE_ROW_ALIGN)) * MOE_ROW_ALIGN
        lr = lax.broadcasted_iota(jnp.int32, (LANES, LANES), 0)
        lc = lax.broadcasted_iota(jnp.int32, (LANES, LANES), 1)
        seg_start = _dot(aligned, jnp.where(lr < lc, 1.0, 0.0), precision=HIGHEST)
        pos = jnp.sum(jnp.where(mine, seg_start[0:1, :] + csum - 1.0, 0.0), axis=-1, keepdims=True)
        pos_b = jnp.broadcast_to(pos, (tm, LANES))
        pos_ref[...] = pos_b.astype(jnp.int32)
        pos_row = pos_b.T[0:1, :].astype(jnp.int32)
        lane1 = lax.broadcasted_iota(jnp.int32, (SUBLANES, LANES), 1)
        for k in range(N_GROUPS):
            meta_ref[k] = jnp.sum(jnp.where(lane1 == k, counts, 0.0)[0:1, :]).astype(jnp.int32)
            meta_ref[N_GROUPS + k] = jnp.sum(jnp.where(lane1 == k, seg_start, 0.0)[0:1, :]).astype(jnp.int32)

        c_hi = comb.astype(BF16)
        c_lo = (comb - c_hi.astype(F32)).astype(BF16)
        sub = lax.broadcasted_iota(jnp.int32, (MOE_SORTED, tm), 0)
        perm = jnp.where(pos_row == sub, 1.0, 0.0).astype(BF16)
        xs_ref[0:MOE_SORTED, :] = _dot(perm, xf_ref[...]).astype(BF16)
        cs_ref[0:MOE_SORTED, :] = _dot(perm, c_hi) + _dot(perm, c_lo)
        acc_ref[0:MOE_SORTED, :] = jnp.zeros((MOE_SORTED, acc_ref.shape[1]), F32)

    def expert_piece(start, m):
        rows = pl.ds(pl.multiple_of(start, MOE_ROW_ALIGN), m)
        xs = xs_ref[rows, :]
        cs = cs_ref[rows, :]
        lane_m = lax.broadcasted_iota(jnp.int32, (m, LANES), 1)
        contrib = None
        for j in range(MOE_EXPERTS_PER_STEP):
            e = p * MOE_EXPERTS_PER_STEP + j
            wcol = jnp.sum(jnp.where(lane_m == N_GROUPS + e, cs, 0.0), axis=-1, keepdims=True)
            act = _silu(_dot(xs, wg_ref[j])) * _dot(xs, wu_ref[j]) * wcol
            part = _dot(act.astype(BF16), wd_ref[j])
            contrib = part if contrib is None else contrib + part
        acc_ref[rows, :] += contrib

    count = meta_ref[g]
    seg = meta_ref[N_GROUPS + g]
    n_full = jnp.maximum(count - 1, 0) // MOE_PIECE

    def full_piece(k, carry):
        expert_piece(seg + k * MOE_PIECE, MOE_PIECE)
        return carry

    lax.fori_loop(0, n_full, full_piece, 0)
    last_start = seg + n_full * MOE_PIECE
    last_rows = count - n_full * MOE_PIECE
    for m in range(MOE_MIN_PIECE, MOE_PIECE + 1, MOE_BUCKET):
        lo = 0 if m == MOE_MIN_PIECE else m - MOE_BUCKET

        @pl.when(jnp.logical_and(last_rows > lo, last_rows <= m))
        def _(m=m):
            expert_piece(last_start, m)

    @pl.when(p == pl.num_programs(1) - 1)
    def _():
        pos = pos_ref[:, 0:1]
        lane_s = lax.broadcasted_iota(jnp.int32, (tm, MOE_SORTED), 1)
        inv = jnp.where(pos == lane_s, 1.0, 0.0).astype(BF16)
        moe = _dot(inv, acc_ref[0:MOE_SORTED, :].astype(BF16))
        h = h_ref[...] + gate_ref[...] * moe
        o_ref[...] = h * lax.rsqrt(jnp.mean(h * h, axis=-1, keepdims=True) + NORM_EPS) * nw_ref[...]


def _moe(xf2, comb, wg, wu, wd, h2, mod4, norm_out_w, seq):
    t, d = xf2.shape
    f = wg.shape[-1]
    tm = MOE_TILE
    per_b = seq // tm
    eps = MOE_EXPERTS_PER_STEP
    row_spec = pl.BlockSpec((tm, d), lambda i, p: (i, 0))
    return pl.pallas_call(
        _moe_kernel,
        grid=(t // tm, N_EXPERTS // eps),
        in_specs=[
            row_spec,
            pl.BlockSpec((tm, LANES), lambda i, p: (i, 0)),
            pl.BlockSpec((eps, d, f), lambda i, p: (p, 0, 0)),
            pl.BlockSpec((eps, d, f), lambda i, p: (p, 0, 0)),
            pl.BlockSpec((eps, f, d), lambda i, p: (p, 0, 0)),
            row_spec,
            pl.BlockSpec((None, None, 1, d), lambda i, p: (i // per_b, 5, 0, 0)),
            pl.BlockSpec((1, d), lambda i, p: (0, 0)),
        ],
        out_specs=row_spec,
        out_shape=jax.ShapeDtypeStruct((t, d), F32),
        scratch_shapes=[
            pltpu.VMEM((MOE_ROWS, d), BF16),
            pltpu.VMEM((MOE_ROWS, LANES), F32),
            pltpu.VMEM((MOE_ROWS, d), F32),
            pltpu.VMEM((tm, LANES), jnp.int32),
            pltpu.VMEM((tm, tm), BF16),
            pltpu.SMEM((2 * N_GROUPS,), jnp.int32),
        ],
        compiler_params=pltpu.CompilerParams(dimension_semantics=("arbitrary", "arbitrary"),
                                             vmem_limit_bytes=MOE_VMEM_LIMIT),
        name="moe",
    )(xf2, comb, wg, wu, wd, h2, mod4, norm_out_w.reshape(1, d))


def _pad_lanes(a):
    return jnp.pad(a, ((0, 0), (0, LANES - a.shape[1])))


def _layer(h3, c, mod_w, mod_b, norm_mix_w, w_in, conv_w, a_log, dt_bias, gdn_norm_w, w_out, norm_ffn_w,
           w_group, b_group, w_router, b_router, w_gate, w_up, w_down, norm_out_w):
    b, s, d = h3.shape
    t = b * s
    x2 = h3.reshape(t, d)

    n_gate_cols = 2 * GDN_HEADS
    small_lo = COL_GATE_A
    w_main = jnp.concatenate([w_in[:, :small_lo], w_in[:, small_lo + n_gate_cols:]], axis=1).astype(BF16)
    w_small = _pad_lanes(w_in[:, small_lo:small_lo + n_gate_cols])
    w_route = _pad_lanes(jnp.concatenate([w_group, w_router], axis=1))
    b_route = _pad_lanes(jnp.concatenate([b_group, b_router.reshape(-1)])[None, :])
    alog_row = _pad_lanes(a_log[None, :])
    dtb_row = _pad_lanes(dt_bias[None, :])
    f = w_gate.shape[-1]
    wg = w_gate.reshape(N_EXPERTS, d, f).astype(BF16)
    wu = w_up.reshape(N_EXPERTS, d, f).astype(BF16)
    wd = w_down.reshape(N_EXPERTS, f, d).astype(BF16)

    half = RET_DK // 2
    inv_freq = 1.0 / (ROPE_BASE ** (jnp.arange(half, dtype=F32) / half))
    ang = jnp.arange(s, dtype=F32)[:, None] * inv_freq[None, :]
    cos_t = jnp.concatenate([jnp.cos(ang), jnp.cos(ang)], axis=1)
    sin_t = jnp.concatenate([-jnp.sin(ang), jnp.sin(ang)], axis=1)
    log_gamma = jnp.log(1.0 - 2.0 ** (-5.0 - jnp.arange(RET_HEADS, dtype=F32)))
    log_gamma = jnp.broadcast_to(log_gamma[:, None], (RET_HEADS, RET_DV))

    mod4 = _mod(c, mod_w, mod_b).reshape(b, N_MOD, 1, d)
    proj, gates = _inproj(x2, mod4, norm_mix_w, w_main, w_small, s)
    proj3 = proj.reshape(b, s, N_MAIN)
    ya = _retention(proj3, cos_t, sin_t, log_gamma)
    yb = _gdn(proj3, gates.reshape(b, s, LANES), alog_row, dtb_row, conv_w, gdn_norm_w)
    h2, xf2, comb = _outproj(ya.reshape(t, d), yb.reshape(t, d), x2, w_out.astype(BF16), mod4, norm_ffn_w,
                             w_route, b_route, s)
    return _moe(xf2, comb, wg, wu, wd, h2, mod4, norm_out_w, s).reshape(b, s, d)


def kernel(x, c, mod_w, mod_b, norm_mix_w, w_in, gdn_conv_w, gdn_a_log, gdn_dt_bias, gdn_norm_w, w_out, norm_ffn_w,
           w_group, b_group, w_router, b_router, w_gate, w_up, w_down, norm_out_w):
    assert mod_w.shape[0] == 1, "one residual layer"
    return _layer(x, c, mod_w[0], mod_b[0], norm_mix_w[0], w_in[0], gdn_conv_w[0], gdn_a_log[0], gdn_dt_bias[0],
                  gdn_norm_w[0], w_out[0], norm_ffn_w[0], w_group[0], b_group[0], w_router[0], b_router[0],
                  w_gate[0], w_up[0], w_down[0], norm_out_w)
```

```python
import jax
import jax.numpy as jnp
from jax import lax
from jax.experimental import pallas as pl
from jax.experimental.pallas import tpu as pltpu

F32 = jnp.float32
BF16 = jnp.bfloat16
HIGHEST = lax.Precision.HIGHEST

RET_HEADS = 4
RET_DK = 128
RET_DV = 256
RET_BATCH = 2
GDN_HEADS = 4
GDN_DK = 128
GDN_DV = 256
GDN_CHUNK = 64
GDN_BATCH = 2
CONV_K = 4
N_GROUPS = 4
EXPERTS_PER_GROUP = 4
N_EXPERTS = N_GROUPS * EXPERTS_PER_GROUP
ROPE_BASE = 10000.0
NORM_EPS = 1e-6
L2_EPS = 1e-6
N_MOD = 6
LANES = 128
SUBLANES = 8
SEQ_TILE = 256
VMEM_LIMIT = 48 * 1024 * 1024
MOE_TILE = 1024
MOE_ROW_ALIGN = 16
MOE_SORTED = MOE_TILE + N_GROUPS * MOE_ROW_ALIGN
MOE_PIECE = 512
MOE_BUCKET = 64
MOE_MIN_PIECE = 128
MOE_ROWS = MOE_SORTED + MOE_PIECE
MOE_EXPERTS_PER_STEP = 2
MOE_VMEM_LIMIT = 56 * 1024 * 1024
INPROJ_VMEM_LIMIT = 56 * 1024 * 1024

COL_RQ, COL_RK, COL_RV, COL_RG = 0, 512, 1024, 2048
COL_GQ, COL_GK, COL_GV, COL_GZ = 3072, 3584, 4096, 5120
COL_GATE_A, COL_GATE_B = 6144, 7168
N_MAIN = 8192


def _sigmoid(x):
    return 0.5 * jnp.tanh(0.5 * x) + 0.5


def _silu(x):
    h = 0.5 * x
    return h + h * jnp.tanh(h)


def _dot(a, b, **kw):
    return jnp.dot(a, b, preferred_element_type=F32, **kw)


def _dot_nt(a, b, **kw):
    return lax.dot_general(a, b, (((1,), (1,)), ((), ())), preferred_element_type=F32, **kw)


def _dot_tn(a, b, **kw):
    return lax.dot_general(a, b, (((0,), (0,)), ((), ())), preferred_element_type=F32, **kw)


def _params(*sem):
    return pltpu.CompilerParams(dimension_semantics=sem, vmem_limit_bytes=VMEM_LIMIT)


def _mod_kernel(c_ref, w_ref, b_ref, o_ref):
    a = _silu(c_ref[...])
    o_ref[...] = _dot(a, w_ref[...], precision=HIGHEST) + b_ref[...]


def _mod(c, mod_w, mod_b):
    b, d = c.shape
    n = mod_w.shape[1]
    tn = d
    return pl.pallas_call(
        _mod_kernel,
        grid=(n // tn,),
        in_specs=[
            pl.BlockSpec((b, d), lambda j: (0, 0)),
            pl.BlockSpec((d, tn), lambda j: (0, j)),
            pl.BlockSpec((1, tn), lambda j: (0, j)),
        ],
        out_specs=pl.BlockSpec((b, tn), lambda j: (0, j)),
        out_shape=jax.ShapeDtypeStruct((b, n), F32),
        compiler_params=_params("arbitrary"),
        name="mod",
    )(c, mod_w, mod_b.reshape(1, n))


def _inproj_kernel(x0_ref, shift0_ref, scale0_ref, xn_ref, shiftn_ref, scalen_ref, nw_ref, w_ref, ws_ref,
                   o_ref, og_ref, xb_even_ref, xb_odd_ref):
    i = pl.program_id(0)
    j = pl.program_id(1)
    tm = x0_ref.shape[0]
    slab = tm // pl.num_programs(1)

    def prepare(x, shift, scale):
        y = x * lax.rsqrt(jnp.mean(x * x, axis=-1, keepdims=True) + NORM_EPS) * nw_ref[...]
        return (y * (1.0 + scale) + shift).astype(BF16)

    @pl.when(jnp.logical_and(i == 0, j == 0))
    def _():
        xb_even_ref[...] = prepare(x0_ref[...], shift0_ref[...], scale0_ref[...])

    def step(cur_ref, nxt_ref):
        rows = pl.ds(pl.multiple_of(j * slab, slab), slab)
        nxt_ref[rows, :] = prepare(xn_ref[rows, :], shiftn_ref[...], scalen_ref[...])
        xb = cur_ref[...]
        o_ref[...] = _dot(xb, w_ref[...]).astype(BF16)

        @pl.when(j == 0)
        def _():
            og_ref[...] = _dot(xb, ws_ref[...])

    @pl.when(i % 2 == 0)
    def _():
        step(xb_even_ref, xb_odd_ref)

    @pl.when(i % 2 == 1)
    def _():
        step(xb_odd_ref, xb_even_ref)


def _inproj(x2, mod4, norm_w, w_main, w_small, seq, tm=1024, tn=4096):
    t, d = x2.shape
    n = w_main.shape[1]
    per_b = seq // tm
    last = t // tm - 1

    def nxt(i):
        return jnp.minimum(i + 1, last)

    return pl.pallas_call(
        _inproj_kernel,
        grid=(t // tm, n // tn),
        in_specs=[
            pl.BlockSpec((tm, d), lambda i, j: (0, 0), pipeline_mode=pl.Buffered(1)),
            pl.BlockSpec((None, None, 1, d), lambda i, j: (0, 0, 0, 0)),
            pl.BlockSpec((None, None, 1, d), lambda i, j: (0, 1, 0, 0)),
            pl.BlockSpec((tm, d), lambda i, j: (nxt(i), 0)),
            pl.BlockSpec((None, None, 1, d), lambda i, j: (nxt(i) // per_b, 0, 0, 0)),
            pl.BlockSpec((None, None, 1, d), lambda i, j: (nxt(i) // per_b, 1, 0, 0)),
            pl.BlockSpec((1, d), lambda i, j: (0, 0)),
            pl.BlockSpec((d, tn), lambda i, j: (0, j)),
            pl.BlockSpec((d, LANES), lambda i, j: (0, 0)),
        ],
        out_specs=[
            pl.BlockSpec((tm, tn), lambda i, j: (i, j)),
            pl.BlockSpec((tm, LANES), lambda i, j: (i, 0)),
        ],
        out_shape=[
            jax.ShapeDtypeStruct((t, n), BF16),
            jax.ShapeDtypeStruct((t, LANES), F32),
        ],
        scratch_shapes=[pltpu.VMEM((tm, d), BF16), pltpu.VMEM((tm, d), BF16)],
        compiler_params=pltpu.CompilerParams(dimension_semantics=("arbitrary", "arbitrary"),
                                             vmem_limit_bytes=INPROJ_VMEM_LIMIT),
        name="inproj",
    )(x2, mod4, mod4, x2, mod4, mod4, norm_w.reshape(1, d), w_main, w_small.astype(BF16))


def _ret_kernel(lg_ref, q_ref, k_ref, v_ref, rg_ref, ga_ref, cos_ref, sin_ref, o_ref,
                state_ref, intra_ref, qd_ref, kd_ref):
    c = SEQ_TILE
    first = jnp.logical_and(pl.program_id(0) == 0, pl.program_id(1) == 0)

    @pl.when(first)
    def _():
        row = lax.broadcasted_iota(jnp.int32, (c, c), 0)
        col = lax.broadcasted_iota(jnp.int32, (c, c), 1)
        rel = (row - col).astype(F32)
        causal = row >= col
        pos = lax.broadcasted_iota(jnp.int32, (c, RET_DK), 0).astype(F32)
        for h in range(RET_HEADS):
            lg = lg_ref[h:h + 1, :]
            intra_ref[h] = jnp.where(causal, jnp.exp(jnp.where(causal, rel, 0.0) * lg), 0.0)
            qd_ref[h] = jnp.exp((pos + 1.0) * lg[:, :RET_DK])
            kd_ref[h] = jnp.exp((c - 1.0 - pos) * lg[:, :RET_DK])

    @pl.when(pl.program_id(1) == 0)
    def _():
        state_ref[...] = jnp.zeros_like(state_ref)

    cos = cos_ref[...]
    sin = sin_ref[...]
    for s in range(RET_BATCH):
        for h in range(RET_HEADS):
            qs = slice(h * RET_DK, (h + 1) * RET_DK)
            vs = slice(h * RET_DV, (h + 1) * RET_DV)
            qr = q_ref[s, :, qs].astype(F32)
            kr = k_ref[s, :, qs].astype(F32)
            q = qr * cos + pltpu.roll(qr, RET_DK // 2, 1) * sin
            k = (kr * cos + pltpu.roll(kr, RET_DK // 2, 1) * sin) * (RET_DK ** -0.5)
            v = v_ref[s, :, vs]
            state = state_ref[s, h]
            chunk_decay = jnp.exp(float(c) * lg_ref[h:h + 1, :])
            scores = _dot_nt(q.astype(BF16), k.astype(BF16)) * intra_ref[h]
            o = _dot(scores.astype(BF16), v) + _dot((q * qd_ref[h]).astype(BF16), state.astype(BF16))
            state_ref[s, h] = state * chunk_decay + _dot_tn((k * kd_ref[h]).astype(BF16), v)
            o = o * lax.rsqrt(jnp.mean(o * o, axis=-1, keepdims=True) + NORM_EPS)
            y = _silu(rg_ref[s, :, vs].astype(F32)) * o
            o_ref[s, :, vs] = (_sigmoid(ga_ref[s, :, vs].astype(F32)) * y).astype(BF16)


def _retention(proj3, cos_t, sin_t, log_gamma):
    b, s, _ = proj3.shape
    ts = SEQ_TILE
    nb = RET_BATCH
    assert b % nb == 0
    qk_w = RET_HEADS * RET_DK
    v_w = RET_HEADS * RET_DV
    return pl.pallas_call(
        _ret_kernel,
        grid=(b // nb, s // ts),
        in_specs=[
            pl.BlockSpec((RET_HEADS, RET_DV), lambda i, j: (0, 0)),
            pl.BlockSpec((nb, ts, qk_w), lambda i, j: (i, j, COL_RQ // qk_w)),
            pl.BlockSpec((nb, ts, qk_w), lambda i, j: (i, j, COL_RK // qk_w)),
            pl.BlockSpec((nb, ts, v_w), lambda i, j: (i, j, COL_RV // v_w)),
            pl.BlockSpec((nb, ts, v_w), lambda i, j: (i, j, COL_RG // v_w)),
            pl.BlockSpec((nb, ts, v_w), lambda i, j: (i, j, COL_GATE_A // v_w)),
            pl.BlockSpec((ts, RET_DK), lambda i, j: (j, 0)),
            pl.BlockSpec((ts, RET_DK), lambda i, j: (j, 0)),
        ],
        out_specs=pl.BlockSpec((nb, ts, v_w), lambda i, j: (i, j, 0)),
        out_shape=jax.ShapeDtypeStruct((b, s, v_w), BF16),
        scratch_shapes=[
            pltpu.VMEM((nb, RET_HEADS, RET_DK, RET_DV), F32),
            pltpu.VMEM((RET_HEADS, ts, ts), F32),
            pltpu.VMEM((RET_HEADS, ts, RET_DK), F32),
            pltpu.VMEM((RET_HEADS, ts, RET_DK), F32),
        ],
        compiler_params=_params("arbitrary", "arbitrary"),
        name="retention",
    )(log_gamma, proj3, proj3, proj3, proj3, proj3, cos_t, sin_t)


def _gdn_kernel(q_ref, k_ref, v_ref, z_ref, gb_ref, gates_ref, alog_ref, dtb_ref, cwq_ref, cwk_ref, cwv_ref,
                nw_ref, o_ref, state_ref, tail_ref):
    ts = SEQ_TILE
    cc = GDN_CHUNK
    qk_w = GDN_HEADS * GDN_DK

    @pl.when(pl.program_id(1) == 0)
    def _():
        state_ref[...] = jnp.zeros_like(state_ref)
        tail_ref[...] = jnp.zeros_like(tail_ref)

    def conv_silu(cur, tail, cw_ref):
        assert CONV_K == 4
        ext = jnp.concatenate([tail, cur], axis=0)
        ext1 = pltpu.roll(ext, 1, 0)
        near = ext * cw_ref[3:4, :] + ext1 * cw_ref[2:3, :]
        far = ext * cw_ref[1:2, :] + ext1 * cw_ref[0:1, :]
        return _silu((near + pltpu.roll(far, 2, 0))[SUBLANES:])

    row = lax.broadcasted_iota(jnp.int32, (ts, ts), 0)
    col = lax.broadcasted_iota(jnp.int32, (ts, ts), 1)
    same_chunk = (row // cc) == (col // cc)
    causal = jnp.logical_and(same_chunk, row >= col)
    strict = jnp.logical_and(same_chunk, row > col)
    eye = jnp.where(row == col, 1.0, 0.0)
    tri_lower = jnp.where(causal, 1.0, 0.0).astype(BF16)
    tri_upper = jnp.where(jnp.logical_and(same_chunk, row <= col), 1.0, 0.0).astype(BF16)

    units = [(s, h) for s in range(GDN_BATCH) for h in range(GDN_HEADS)]
    q_all, k_all, v_all, beta_all, gc_col_all, gc_row_all = [], [], [], [], [], []
    for s in range(GDN_BATCH):
        q_raw = q_ref[s].astype(F32)
        k_raw = k_ref[s].astype(F32)
        v_raw = v_ref[s].astype(F32)
        q_all.append(conv_silu(q_raw, tail_ref[s, :, 0:qk_w], cwq_ref))
        k_all.append(conv_silu(k_raw, tail_ref[s, :, qk_w:2 * qk_w], cwk_ref))
        v_all.append(conv_silu(v_raw, tail_ref[s, :, 2 * qk_w:], cwv_ref))
        tail_ref[s, :, 0:qk_w] = q_raw[ts - SUBLANES:]
        tail_ref[s, :, qk_w:2 * qk_w] = k_raw[ts - SUBLANES:]
        tail_ref[s, :, 2 * qk_w:] = v_raw[ts - SUBLANES:]

        gates = gates_ref[s]
        x = gates + dtb_ref[...]
        softplus = jnp.maximum(x, 0.0) + jnp.log1p(jnp.exp(-jnp.abs(x)))
        g_all = -jnp.exp(alog_ref[...]) * softplus
        beta_all.append(_sigmoid(gates))
        g_hi = g_all.astype(BF16)
        g_rest = g_all - g_hi.astype(F32)
        g_mid = g_rest.astype(BF16)
        g_lo = (g_rest - g_mid.astype(F32)).astype(BF16)
        gc_col_all.append(sum(_dot(tri_lower, part) for part in (g_hi, g_mid, g_lo)))
        gc_row_all.append(sum(_dot_tn(part, tri_upper) for part in (g_hi, g_mid, g_lo)))

    qn, kn, k16, k_beta, gc, decay, beta = {}, {}, {}, {}, {}, {}, {}
    for u in units:
        s, h = u
        qs = slice(h * GDN_DK, (h + 1) * GDN_DK)
        qh = q_all[s][:, qs]
        kh = k_all[s][:, qs]
        qn[u] = qh * lax.rsqrt(jnp.sum(qh * qh, axis=-1, keepdims=True) + L2_EPS) * (GDN_DK ** -0.5)
        kn[u] = kh * lax.rsqrt(jnp.sum(kh * kh, axis=-1, keepdims=True) + L2_EPS)
        gc[u] = gc_col_all[s][:, h:h + 1]
        gcr = gc_row_all[s][h:h + 1, :]
        decay[u] = jnp.exp(jnp.where(causal, gc[u] - gcr, -jnp.inf))
        beta[u] = beta_all[s][:, GDN_HEADS + h:GDN_HEADS + h + 1]
        k_beta[u] = kn[u] * beta[u]
        k16[u] = kn[u].astype(BF16)
    a_mat = {u: jnp.where(strict, _dot_nt(k_beta[u].astype(BF16), k16[u]) * decay[u], 0.0) for u in units}
    p = {u: eye - a_mat[u] for u in units}
    m16 = {u: a_mat[u].astype(BF16) for u in units}
    for _ in range((cc - 1).bit_length() - 1):
        m16 = {u: _dot(m16[u], m16[u]).astype(BF16) for u in units}
        p = {u: p[u] + _dot(p[u].astype(BF16), m16[u]) for u in units}
    u_all, lhs_state, lhs_vnew, state_decay = {}, {}, {}, {}
    for u in units:
        s, h = u
        vs = slice(h * GDN_DV, (h + 1) * GDN_DV)
        t16 = p[u].astype(BF16)
        egc = jnp.exp(gc[u])
        u_all[u] = _dot(t16, (v_all[s][:, vs] * beta[u]).astype(BF16))
        w_all = _dot(t16, (k_beta[u] * egc).astype(BF16))
        qk = _dot_nt(qn[u].astype(BF16), k16[u]) * decay[u]
        qg = qn[u] * egc
        kn_t = kn[u].T
        gcr = gc_row_all[s][h:h + 1, :]
        for n in range(ts // cc):
            rs = slice(n * cc, (n + 1) * cc)
            g_last = gc[u][(n + 1) * cc - 1:(n + 1) * cc, :]
            kg_t = kn_t[:, rs] * jnp.exp(g_last - gcr[:, rs])
            lhs_state[u, n] = jnp.concatenate([w_all[rs], qg[rs]], axis=0).astype(BF16)
            lhs_vnew[u, n] = jnp.concatenate([qk[rs, rs], kg_t], axis=0).astype(BF16)
            state_decay[u, n] = jnp.exp(g_last)
    state = {u: state_ref[u[0], u[1]] for u in units}
    outs = {u: [] for u in units}
    for n in range(ts // cc):
        rs = slice(n * cc, (n + 1) * cc)
        for u in units:
            from_state = _dot(lhs_state[u, n], state[u].astype(BF16))
            v_new = (u_all[u][rs] - from_state[:cc]).astype(BF16)
            from_vnew = _dot(lhs_vnew[u, n], v_new)
            outs[u].append(from_state[cc:] + from_vnew[:cc])
            state[u] = state[u] * state_decay[u, n] + from_vnew[cc:]
    for u in units:
        s, h = u
        vs = slice(h * GDN_DV, (h + 1) * GDN_DV)
        state_ref[s, h] = state[u]
        o = jnp.concatenate(outs[u], axis=0)
        o = o * lax.rsqrt(jnp.mean(o * o, axis=-1, keepdims=True) + NORM_EPS) * nw_ref[...]
        o = o * _silu(z_ref[s, :, vs].astype(F32))
        o_ref[s, :, vs] = (_sigmoid(gb_ref[s, :, vs].astype(F32)) * o).astype(BF16)


def _gdn(proj3, gates3, alog_row, dtb_row, conv_w, norm_w):
    b, s, _ = proj3.shape
    ts = SEQ_TILE
    nb = GDN_BATCH
    assert b % nb == 0
    qk_w = GDN_HEADS * GDN_DK
    v_w = GDN_HEADS * GDN_DV
    return pl.pallas_call(
        _gdn_kernel,
        grid=(b // nb, s // ts),
        in_specs=[
            pl.BlockSpec((nb, ts, qk_w), lambda i, j: (i, j, COL_GQ // qk_w)),
            pl.BlockSpec((nb, ts, qk_w), lambda i, j: (i, j, COL_GK // qk_w)),
            pl.BlockSpec((nb, ts, v_w), lambda i, j: (i, j, COL_GV // v_w)),
            pl.BlockSpec((nb, ts, v_w), lambda i, j: (i, j, COL_GZ // v_w)),
            pl.BlockSpec((nb, ts, v_w), lambda i, j: (i, j, COL_GATE_B // v_w)),
            pl.BlockSpec((nb, ts, LANES), lambda i, j: (i, j, 0)),
            pl.BlockSpec((1, LANES), lambda i, j: (0, 0)),
            pl.BlockSpec((1, LANES), lambda i, j: (0, 0)),
            pl.BlockSpec((CONV_K, qk_w), lambda i, j: (0, 0)),
            pl.BlockSpec((CONV_K, qk_w), lambda i, j: (0, 1)),
            pl.BlockSpec((CONV_K, v_w), lambda i, j: (0, 1)),
            pl.BlockSpec((1, GDN_DV), lambda i, j: (0, 0)),
        ],
        out_specs=pl.BlockSpec((nb, ts, v_w), lambda i, j: (i, j, 0)),
        out_shape=jax.ShapeDtypeStruct((b, s, v_w), BF16),
        scratch_shapes=[
            pltpu.VMEM((nb, GDN_HEADS, GDN_DK, GDN_DV), F32),
            pltpu.VMEM((nb, SUBLANES, 2 * qk_w + v_w), F32),
        ],
        compiler_params=_params("arbitrary", "arbitrary"),
        name="gdn",
    )(proj3, proj3, proj3, proj3, proj3, gates3, alog_row, dtb_row, conv_w, conv_w, conv_w,
      norm_w.reshape(1, GDN_DV))


def _route(logits):
    lane = lax.broadcasted_iota(jnp.int32, logits.shape, 1)
    neg = jnp.float32(-jnp.inf)

    def first_max(v):
        top = jnp.max(v, axis=-1, keepdims=True)
        return top, jnp.min(jnp.where(v == top, lane, LANES), axis=-1, keepdims=True)

    gl = jnp.where(lane < N_GROUPS, logits, neg)
    g_max, g_idx = first_max(gl)
    g_top = 1.0 / jnp.sum(jnp.exp(gl - g_max), axis=-1, keepdims=True)
    lo = N_GROUPS + EXPERTS_PER_GROUP * g_idx
    el = jnp.where(jnp.logical_and(lane >= lo, lane < lo + EXPERTS_PER_GROUP), logits, neg)
    top1, i1 = first_max(el)
    top2, i2 = first_max(jnp.where(lane == i1, neg, el))
    gap = jnp.exp(top2 - top1)
    w1 = 1.0 / (1.0 + gap)
    comb = jnp.where(lane == i1, g_top * w1, jnp.where(lane == i2, g_top * (gap * w1), 0.0))
    return jnp.where(lane == 0, g_idx.astype(F32), comb)


def _outproj_kernel(ya_ref, yb_ref, x_ref, w_ref, gate_ref, nw_ref, shift_ref, scale_ref, wr_ref, wrh_ref, br_ref,
                    h_ref, xf_ref, comb_ref):
    tm = x_ref.shape[0]
    halves = [slice(k * (tm // 2), (k + 1) * (tm // 2)) for k in range(2)]
    hs, xfs, logits = [], [], []
    for rs in halves:
        merged = ya_ref[rs, :] + yb_ref[rs, :]
        hs.append(x_ref[rs, :] + gate_ref[...] * _dot(merged, w_ref[...]))
    for rs, h in zip(halves, hs):
        h_ref[rs, :] = h
        y = h * lax.rsqrt(jnp.mean(h * h, axis=-1, keepdims=True) + NORM_EPS) * nw_ref[...]
        xfs.append(y * (1.0 + scale_ref[...]) + shift_ref[...])
    for rs, xf in zip(halves, xfs):
        xf_hi = xf.astype(BF16)
        xf_ref[rs, :] = xf_hi
        xf_lo = (xf - xf_hi.astype(F32)).astype(BF16)
        both = _dot(xf_hi, wr_ref[...])
        logits.append(both[:, :LANES] + both[:, LANES:] + _dot(xf_lo, wrh_ref[...]) + br_ref[...])
    for rs, lg in zip(halves, logits):
        comb_ref[rs, :] = _route(lg)


def _outproj(ya2, yb2, x2, w_out_b, mod4, norm_w, w_route, b_route, seq, tm=1024):
    t, d = x2.shape
    per_b = seq // tm
    row_spec = pl.BlockSpec((tm, d), lambda i: (i, 0))
    wr_hi = w_route.astype(BF16)
    wr_lo = (w_route - wr_hi.astype(F32)).astype(BF16)

    def mod_spec(k):
        return pl.BlockSpec((None, None, 1, d), lambda i: (i // per_b, k, 0, 0))

    return pl.pallas_call(
        _outproj_kernel,
        grid=(t // tm,),
        in_specs=[
            row_spec, row_spec, row_spec,
            pl.BlockSpec((d, d), lambda i: (0, 0)),
            mod_spec(2),
            pl.BlockSpec((1, d), lambda i: (0, 0)),
            mod_spec(3),
            mod_spec(4),
            pl.BlockSpec((d, 2 * LANES), lambda i: (0, 0)),
            pl.BlockSpec((d, LANES), lambda i: (0, 0)),
            pl.BlockSpec((1, LANES), lambda i: (0, 0)),
        ],
        out_specs=[row_spec, row_spec, pl.BlockSpec((tm, LANES), lambda i: (i, 0))],
        out_shape=[
            jax.ShapeDtypeStruct((t, d), F32),
            jax.ShapeDtypeStruct((t, d), BF16),
            jax.ShapeDtypeStruct((t, LANES), F32),
        ],
        compiler_params=_params("arbitrary"),
        name="outproj",
    )(ya2, yb2, x2, w_out_b, mod4, norm_w.reshape(1, d), mod4, mod4,
      jnp.concatenate([wr_hi, wr_lo], axis=1), wr_hi, b_route)


def _moe_kernel(xf_ref, comb_ref, wg_ref, wu_ref, wd_ref, h_ref, gate_ref, nw_ref, o_ref,
                xs_ref, cs_ref, acc_ref, pos_ref, ltri_ref, meta_ref):
    i = pl.program_id(0)
    p = pl.program_id(1)
    tm = MOE_TILE
    steps_per_group = EXPERTS_PER_GROUP // MOE_EXPERTS_PER_STEP
    g = p // steps_per_group

    @pl.when(jnp.logical_and(i == 0, p == 0))
    def _():
        row = lax.broadcasted_iota(jnp.int32, (tm, tm), 0)
        col = lax.broadcasted_iota(jnp.int32, (tm, tm), 1)
        ltri_ref[...] = jnp.where(row >= col, 1.0, 0.0).astype(BF16)
        xs_ref[MOE_SORTED:, :] = jnp.zeros((MOE_ROWS - MOE_SORTED, xs_ref.shape[1]), BF16)
        cs_ref[MOE_SORTED:, :] = jnp.zeros((MOE_ROWS - MOE_SORTED, LANES), F32)
        acc_ref[MOE_SORTED:, :] = jnp.zeros((MOE_ROWS - MOE_SORTED, acc_ref.shape[1]), F32)

    @pl.when(p == 0)
    def _():
        comb = comb_ref[...]
        lane = lax.broadcasted_iota(jnp.int32, (tm, LANES), 1)
        gidx = comb[:, 0:1]
        mine = jnp.logical_and(lane.astype(F32) == gidx, lane < N_GROUPS)
        csum = _dot(ltri_ref[...], jnp.where(mine, 1.0, 0.0).astype(BF16))
        counts = jnp.broadcast_to(csum[tm - 1:tm, :], (SUBLANES, LANES))
        aligned = jnp.floor((counts + (MOE_ROW_ALIGN - 0.5)) * (1.0 / MOE_ROW_ALIGN)) * MOE_ROW_ALIGN
        lr = lax.broadcasted_iota(jnp.int32, (LANES, LANES), 0)
        lc = lax.broadcasted_iota(jnp.int32, (LANES, LANES), 1)
        seg_start = _dot(aligned, jnp.where(lr < lc, 1.0, 0.0), precision=HIGHEST)
        pos = jnp.sum(jnp.where(mine, seg_start[0:1, :] + csum - 1.0, 0.0), axis=-1, keepdims=True)
        pos_b = jnp.broadcast_to(pos, (tm, LANES))
        pos_ref[...] = pos_b.astype(jnp.int32)
        pos_row = pos_b.T[0:1, :].astype(jnp.int32)
        lane1 = lax.broadcasted_iota(jnp.int32, (SUBLANES, LANES), 1)
        for k in range(N_GROUPS):
            meta_ref[k] = jnp.sum(jnp.where(lane1 == k, counts, 0.0)[0:1, :]).astype(jnp.int32)
            meta_ref[N_GROUPS + k] = jnp.sum(jnp.where(lane1 == k, seg_start, 0.0)[0:1, :]).astype(jnp.int32)

        c_hi = comb.astype(BF16)
        c_lo = (comb - c_hi.astype(F32)).astype(BF16)
        sub = lax.broadcasted_iota(jnp.int32, (MOE_SORTED, tm), 0)
        perm = jnp.where(pos_row == sub, 1.0, 0.0).astype(BF16)
        xs_ref[0:MOE_SORTED, :] = _dot(perm, xf_ref[...]).astype(BF16)
        both = _dot(perm, jnp.concatenate([c_hi, c_lo], axis=1))
        cs_ref[0:MOE_SORTED, :] = both[:, :LANES] + both[:, LANES:]
        acc_ref[0:MOE_SORTED, :] = jnp.zeros((MOE_SORTED, acc_ref.shape[1]), F32)

    def expert_piece(start, m):
        rows = pl.ds(pl.multiple_of(start, MOE_ROW_ALIGN), m)
        xs = xs_ref[rows, :]
        cs = cs_ref[rows, :]
        lane_m = lax.broadcasted_iota(jnp.int32, (m, LANES), 1)
        contrib = None
        for j in range(MOE_EXPERTS_PER_STEP):
            e = p * MOE_EXPERTS_PER_STEP + j
            wcol = jnp.sum(jnp.where(lane_m == N_GROUPS + e, cs, 0.0), axis=-1, keepdims=True)
            act = _silu(_dot(xs, wg_ref[j])) * _dot(xs, wu_ref[j]) * wcol
            part = _dot(act.astype(BF16), wd_ref[j])
            contrib = part if contrib is None else contrib + part
        acc_ref[rows, :] += contrib

    count = meta_ref[g]
    seg = meta_ref[N_GROUPS + g]
    n_full = jnp.maximum(count - 1, 0) // MOE_PIECE

    def full_piece(k, carry):
        expert_piece(seg + k * MOE_PIECE, MOE_PIECE)
        return carry

    lax.fori_loop(0, n_full, full_piece, 0)
    last_start = seg + n_full * MOE_PIECE
    last_rows = count - n_full * MOE_PIECE
    for m in range(MOE_MIN_PIECE, MOE_PIECE + 1, MOE_BUCKET):
        lo = 0 if m == MOE_MIN_PIECE else m - MOE_BUCKET

        @pl.when(jnp.logical_and(last_rows > lo, last_rows <= m))
        def _(m=m):
            expert_piece(last_start, m)

    @pl.when(p == pl.num_programs(1) - 1)
    def _():
        pos = pos_ref[:, 0:1]
        lane_s = lax.broadcasted_iota(jnp.int32, (tm, MOE_SORTED), 1)
        inv = jnp.where(pos == lane_s, 1.0, 0.0).astype(BF16)
        moe = _dot(inv, acc_ref[0:MOE_SORTED, :].astype(BF16))
        h = h_ref[...] + gate_ref[...] * moe
        o_ref[...] = h * lax.rsqrt(jnp.mean(h * h, axis=-1, keepdims=True) + NORM_EPS) * nw_ref[...]


def _moe(xf2, comb, wg, wu, wd, h2, mod4, norm_out_w, seq):
    t, d = xf2.shape
    f = wg.shape[-1]
    tm = MOE_TILE
    per_b = seq // tm
    eps = MOE_EXPERTS_PER_STEP
    row_spec = pl.BlockSpec((tm, d), lambda i, p: (i, 0))
    return pl.pallas_call(
        _moe_kernel,
        grid=(t // tm, N_EXPERTS // eps),
        in_specs=[
            row_spec,
            pl.BlockSpec((tm, LANES), lambda i, p: (i, 0)),
            pl.BlockSpec((eps, d, f), lambda i, p: (p, 0, 0)),
            pl.BlockSpec((eps, d, f), lambda i, p: (p, 0, 0)),
            pl.BlockSpec((eps, f, d), lambda i, p: (p, 0, 0)),
            row_spec,
            pl.BlockSpec((None, None, 1, d), lambda i, p: (i // per_b, 5, 0, 0)),
            pl.BlockSpec((1, d), lambda i, p: (0, 0)),
        ],
        out_specs=row_spec,
        out_shape=jax.ShapeDtypeStruct((t, d), F32),
        scratch_shapes=[
            pltpu.VMEM((MOE_ROWS, d), BF16),
            pltpu.VMEM((MOE_ROWS, LANES), F32),
            pltpu.VMEM((MOE_ROWS, d), F32),
            pltpu.VMEM((tm, LANES), jnp.int32),
            pltpu.VMEM((tm, tm), BF16),
            pltpu.SMEM((2 * N_GROUPS,), jnp.int32),
        ],
        compiler_params=pltpu.CompilerParams(dimension_semantics=("arbitrary", "arbitrary"),
                                             vmem_limit_bytes=MOE_VMEM_LIMIT),
        name="moe",
    )(xf2, comb, wg, wu, wd, h2, mod4, norm_out_w.reshape(1, d))


def _pad_lanes(a):
    return jnp.pad(a, ((0, 0), (0, LANES - a.shape[1])))


def _layer(h3, c, mod_w, mod_b, norm_mix_w, w_in, conv_w, a_log, dt_bias, gdn_norm_w, w_out, norm_ffn_w,
           w_group, b_group, w_router, b_router, w_gate, w_up, w_down, norm_out_w):
    b, s, d = h3.shape
    t = b * s
    x2 = h3.reshape(t, d)

    n_gate_cols = 2 * GDN_HEADS
    small_lo = COL_GATE_A
    w_main = jnp.concatenate([w_in[:, :small_lo], w_in[:, small_lo + n_gate_cols:]], axis=1).astype(BF16)
    w_small = _pad_lanes(w_in[:, small_lo:small_lo + n_gate_cols])
    w_route = _pad_lanes(jnp.concatenate([w_group, w_router], axis=1))
    b_route = _pad_lanes(jnp.concatenate([b_group, b_router.reshape(-1)])[None, :])
    alog_row = _pad_lanes(a_log[None, :])
    dtb_row = _pad_lanes(dt_bias[None, :])
    f = w_gate.shape[-1]
    wg = w_gate.reshape(N_EXPERTS, d, f).astype(BF16)
    wu = w_up.reshape(N_EXPERTS, d, f).astype(BF16)
    wd = w_down.reshape(N_EXPERTS, f, d).astype(BF16)

    half = RET_DK // 2
    inv_freq = 1.0 / (ROPE_BASE ** (jnp.arange(half, dtype=F32) / half))
    ang = jnp.arange(s, dtype=F32)[:, None] * inv_freq[None, :]
    cos_t = jnp.concatenate([jnp.cos(ang), jnp.cos(ang)], axis=1)
    sin_t = jnp.concatenate([-jnp.sin(ang), jnp.sin(ang)], axis=1)
    log_gamma = jnp.log(1.0 - 2.0 ** (-5.0 - jnp.arange(RET_HEADS, dtype=F32)))
    log_gamma = jnp.broadcast_to(log_gamma[:, None], (RET_HEADS, RET_DV))

    mod4 = _mod(c, mod_w, mod_b).reshape(b, N_MOD, 1, d)
    proj, gates = _inproj(x2, mod4, norm_mix_w, w_main, w_small, s)
    proj3 = proj.reshape(b, s, N_MAIN)
    ya = _retention(proj3, cos_t, sin_t, log_gamma)
    yb = _gdn(proj3, gates.reshape(b, s, LANES), alog_row, dtb_row, conv_w, gdn_norm_w)
    h2, xf2, comb = _outproj(ya.reshape(t, d), yb.reshape(t, d), x2, w_out.astype(BF16), mod4, norm_ffn_w,
                             w_route, b_route, s)
    return _moe(xf2, comb, wg, wu, wd, h2, mod4, norm_out_w, s).reshape(b, s, d)


def kernel(x, c, mod_w, mod_b, norm_mix_w, w_in, gdn_conv_w, gdn_a_log, gdn_dt_bias, gdn_norm_w, w_out, norm_ffn_w,
           w_group, b_group, w_router, b_router, w_gate, w_up, w_down, norm_out_w):
    assert mod_w.shape[0] == 1, "one residual layer"
    return _layer(x, c, mod_w[0], mod_b[0], norm_mix_w[0], w_in[0], gdn_conv_w[0], gdn_a_log[0], gdn_dt_bias[0],
                  gdn_norm_w[0], w_out[0], norm_ffn_w[0], w_group[0], b_group[0], w_router[0], b_router[0],
                  w_gate[0], w_up[0], w_down[0], norm_out_w)
```

```python
import jax
import jax.numpy as jnp
from jax import lax
from jax.experimental import pallas as pl
from jax.experimental.pallas import tpu as pltpu

F32 = jnp.float32
BF16 = jnp.bfloat16
HIGHEST = lax.Precision.HIGHEST

RET_HEADS = 4
RET_DK = 128
RET_DV = 256
RET_BATCH = 4
GDN_HEADS = 4
GDN_DK = 128
GDN_DV = 256
GDN_CHUNK = 64
GDN_BATCH = 2
CONV_K = 4
N_GROUPS = 4
EXPERTS_PER_GROUP = 4
N_EXPERTS = N_GROUPS * EXPERTS_PER_GROUP
ROPE_BASE = 10000.0
NORM_EPS = 1e-6
L2_EPS = 1e-6
N_MOD = 6
LANES = 128
SUBLANES = 8
SEQ_TILE = 256
VMEM_LIMIT = 48 * 1024 * 1024
MOE_TILE = 1024
MOE_ROW_ALIGN = 16
MOE_SORTED = MOE_TILE + N_GROUPS * MOE_ROW_ALIGN
MOE_PIECE = 512
MOE_BUCKET = 64
MOE_MIN_PIECE = 128
MOE_ROWS = MOE_SORTED + MOE_PIECE
MOE_EXPERTS_PER_STEP = 2
MOE_VMEM_LIMIT = 56 * 1024 * 1024
INPROJ_VMEM_LIMIT = 56 * 1024 * 1024

COL_RQ, COL_RK, COL_RV, COL_RG = 0, 512, 1024, 2048
COL_GQ, COL_GK, COL_GV, COL_GZ = 3072, 3584, 4096, 5120
COL_GATE_A, COL_GATE_B = 6144, 7168
N_MAIN = 8192


def _sigmoid(x):
    return 0.5 * jnp.tanh(0.5 * x) + 0.5


def _silu(x):
    h = 0.5 * x
    return h + h * jnp.tanh(h)


def _dot(a, b, **kw):
    return jnp.dot(a, b, preferred_element_type=F32, **kw)


def _dot_nt(a, b, **kw):
    return lax.dot_general(a, b, (((1,), (1,)), ((), ())), preferred_element_type=F32, **kw)


def _dot_tn(a, b, **kw):
    return lax.dot_general(a, b, (((0,), (0,)), ((), ())), preferred_element_type=F32, **kw)


def _params(*sem):
    return pltpu.CompilerParams(dimension_semantics=sem, vmem_limit_bytes=VMEM_LIMIT)


def _mod_kernel(c_ref, w_ref, b_ref, o_ref):
    a = _silu(c_ref[...])
    o_ref[...] = _dot(a, w_ref[...], precision=HIGHEST) + b_ref[...]


def _mod(c, mod_w, mod_b):
    b, d = c.shape
    n = mod_w.shape[1]
    tn = d
    return pl.pallas_call(
        _mod_kernel,
        grid=(n // tn,),
        in_specs=[
            pl.BlockSpec((b, d), lambda j: (0, 0)),
            pl.BlockSpec((d, tn), lambda j: (0, j)),
            pl.BlockSpec((1, tn), lambda j: (0, j)),
        ],
        out_specs=pl.BlockSpec((b, tn), lambda j: (0, j)),
        out_shape=jax.ShapeDtypeStruct((b, n), F32),
        compiler_params=_params("arbitrary"),
        name="mod",
    )(c, mod_w, mod_b.reshape(1, n))


def _inproj_kernel(x0_ref, shift0_ref, scale0_ref, xn_ref, shiftn_ref, scalen_ref, nw_ref, w_ref, ws_ref,
                   o_ref, og_ref, xb_even_ref, xb_odd_ref):
    i = pl.program_id(0)
    j = pl.program_id(1)
    tm = x0_ref.shape[0]
    slab = tm // pl.num_programs(1)

    def prepare(x, shift, scale):
        y = x * lax.rsqrt(jnp.mean(x * x, axis=-1, keepdims=True) + NORM_EPS) * nw_ref[...]
        return (y * (1.0 + scale) + shift).astype(BF16)

    @pl.when(jnp.logical_and(i == 0, j == 0))
    def _():
        xb_even_ref[...] = prepare(x0_ref[...], shift0_ref[...], scale0_ref[...])

    def step(cur_ref, nxt_ref):
        rows = pl.ds(pl.multiple_of(j * slab, slab), slab)
        nxt_ref[rows, :] = prepare(xn_ref[rows, :], shiftn_ref[...], scalen_ref[...])
        xb = cur_ref[...]
        o_ref[...] = _dot(xb, w_ref[...]).astype(BF16)

        @pl.when(j == 0)
        def _():
            og_ref[...] = _dot(xb, ws_ref[...])

    @pl.when(i % 2 == 0)
    def _():
        step(xb_even_ref, xb_odd_ref)

    @pl.when(i % 2 == 1)
    def _():
        step(xb_odd_ref, xb_even_ref)


def _inproj(x2, mod4, norm_w, w_main, w_small, seq, tm=1024, tn=4096):
    t, d = x2.shape
    n = w_main.shape[1]
    per_b = seq // tm
    last = t // tm - 1

    def nxt(i):
        return jnp.minimum(i + 1, last)

    return pl.pallas_call(
        _inproj_kernel,
        grid=(t // tm, n // tn),
        in_specs=[
            pl.BlockSpec((tm, d), lambda i, j: (0, 0), pipeline_mode=pl.Buffered(1)),
            pl.BlockSpec((None, None, 1, d), lambda i, j: (0, 0, 0, 0)),
            pl.BlockSpec((None, None, 1, d), lambda i, j: (0, 1, 0, 0)),
            pl.BlockSpec((tm, d), lambda i, j: (nxt(i), 0)),
            pl.BlockSpec((None, None, 1, d), lambda i, j: (nxt(i) // per_b, 0, 0, 0)),
            pl.BlockSpec((None, None, 1, d), lambda i, j: (nxt(i) // per_b, 1, 0, 0)),
            pl.BlockSpec((1, d), lambda i, j: (0, 0)),
            pl.BlockSpec((d, tn), lambda i, j: (0, j)),
            pl.BlockSpec((d, LANES), lambda i, j: (0, 0)),
        ],
        out_specs=[
            pl.BlockSpec((tm, tn), lambda i, j: (i, j)),
            pl.BlockSpec((tm, LANES), lambda i, j: (i, 0)),
        ],
        out_shape=[
            jax.ShapeDtypeStruct((t, n), BF16),
            jax.ShapeDtypeStruct((t, LANES), F32),
        ],
        scratch_shapes=[pltpu.VMEM((tm, d), BF16), pltpu.VMEM((tm, d), BF16)],
        compiler_params=pltpu.CompilerParams(dimension_semantics=("arbitrary", "arbitrary"),
                                             vmem_limit_bytes=INPROJ_VMEM_LIMIT),
        name="inproj",
    )(x2, mod4, mod4, x2, mod4, mod4, norm_w.reshape(1, d), w_main, w_small.astype(BF16))


def _ret_kernel(lg_ref, q_ref, k_ref, v_ref, rg_ref, ga_ref, cos_ref, sin_ref, o_ref,
                state_ref, intra_ref, qd_ref, kd_ref):
    c = SEQ_TILE
    first = jnp.logical_and(pl.program_id(0) == 0, pl.program_id(1) == 0)

    @pl.when(first)
    def _():
        row = lax.broadcasted_iota(jnp.int32, (c, c), 0)
        col = lax.broadcasted_iota(jnp.int32, (c, c), 1)
        rel = (row - col).astype(F32)
        causal = row >= col
        pos = lax.broadcasted_iota(jnp.int32, (c, RET_DK), 0).astype(F32)
        for h in range(RET_HEADS):
            lg = lg_ref[h:h + 1, :]
            intra_ref[h] = jnp.where(causal, jnp.exp(jnp.where(causal, rel, 0.0) * lg), 0.0)
            qd_ref[h] = jnp.exp((pos + 1.0) * lg[:, :RET_DK])
            kd_ref[h] = jnp.exp((c - 1.0 - pos) * lg[:, :RET_DK])

    @pl.when(pl.program_id(1) == 0)
    def _():
        state_ref[...] = jnp.zeros_like(state_ref)

    cos = cos_ref[...]
    sin = sin_ref[...]
    for s in range(RET_BATCH):
        for h in range(RET_HEADS):
            qs = slice(h * RET_DK, (h + 1) * RET_DK)
            vs = slice(h * RET_DV, (h + 1) * RET_DV)
            qr = q_ref[s, :, qs].astype(F32)
            kr = k_ref[s, :, qs].astype(F32)
            q = qr * cos + pltpu.roll(qr, RET_DK // 2, 1) * sin
            k = (kr * cos + pltpu.roll(kr, RET_DK // 2, 1) * sin) * (RET_DK ** -0.5)
            v = v_ref[s, :, vs]
            state = state_ref[s, h]
            chunk_decay = jnp.exp(float(c) * lg_ref[h:h + 1, :])
            scores = _dot_nt(q.astype(BF16), k.astype(BF16)) * intra_ref[h]
            o = _dot(scores.astype(BF16), v) + _dot((q * qd_ref[h]).astype(BF16), state.astype(BF16))
            state_ref[s, h] = state * chunk_decay + _dot_tn((k * kd_ref[h]).astype(BF16), v)
            o = o * lax.rsqrt(jnp.mean(o * o, axis=-1, keepdims=True) + NORM_EPS)
            y = _silu(rg_ref[s, :, vs].astype(F32)) * o
            o_ref[s, :, vs] = (_sigmoid(ga_ref[s, :, vs].astype(F32)) * y).astype(BF16)


def _retention(proj3, cos_t, sin_t, log_gamma):
    b, s, _ = proj3.shape
    ts = SEQ_TILE
    nb = RET_BATCH
    assert b % nb == 0
    qk_w = RET_HEADS * RET_DK
    v_w = RET_HEADS * RET_DV
    return pl.pallas_call(
        _ret_kernel,
        grid=(b // nb, s // ts),
        in_specs=[
            pl.BlockSpec((RET_HEADS, RET_DV), lambda i, j: (0, 0)),
            pl.BlockSpec((nb, ts, qk_w), lambda i, j: (i, j, COL_RQ // qk_w)),
            pl.BlockSpec((nb, ts, qk_w), lambda i, j: (i, j, COL_RK // qk_w)),
            pl.BlockSpec((nb, ts, v_w), lambda i, j: (i, j, COL_RV // v_w)),
            pl.BlockSpec((nb, ts, v_w), lambda i, j: (i, j, COL_RG // v_w)),
            pl.BlockSpec((nb, ts, v_w), lambda i, j: (i, j, COL_GATE_A // v_w)),
            pl.BlockSpec((ts, RET_DK), lambda i, j: (j, 0)),
            pl.BlockSpec((ts, RET_DK), lambda i, j: (j, 0)),
        ],
        out_specs=pl.BlockSpec((nb, ts, v_w), lambda i, j: (i, j, 0)),
        out_shape=jax.ShapeDtypeStruct((b, s, v_w), BF16),
        scratch_shapes=[
            pltpu.VMEM((nb, RET_HEADS, RET_DK, RET_DV), F32),
            pltpu.VMEM((RET_HEADS, ts, ts), F32),
            pltpu.VMEM((RET_HEADS, ts, RET_DK), F32),
            pltpu.VMEM((RET_HEADS, ts, RET_DK), F32),
        ],
        compiler_params=_params("arbitrary", "arbitrary"),
        name="retention",
    )(log_gamma, proj3, proj3, proj3, proj3, proj3, cos_t, sin_t)


def _gdn_kernel(q_ref, k_ref, v_ref, z_ref, gb_ref, gates_ref, alog_ref, dtb_ref, cwq_ref, cwk_ref, cwv_ref,
                nw_ref, o_ref, state_ref, tail_ref):
    ts = SEQ_TILE
    cc = GDN_CHUNK
    qk_w = GDN_HEADS * GDN_DK

    @pl.when(pl.program_id(1) == 0)
    def _():
        state_ref[...] = jnp.zeros_like(state_ref)
        tail_ref[...] = jnp.zeros_like(tail_ref)

    def conv_silu(cur, tail, cw_ref):
        assert CONV_K == 4
        ext = jnp.concatenate([tail, cur], axis=0)
        ext1 = pltpu.roll(ext, 1, 0)
        near = ext * cw_ref[3:4, :] + ext1 * cw_ref[2:3, :]
        far = ext * cw_ref[1:2, :] + ext1 * cw_ref[0:1, :]
        return _silu((near + pltpu.roll(far, 2, 0))[SUBLANES:])

    row = lax.broadcasted_iota(jnp.int32, (ts, ts), 0)
    col = lax.broadcasted_iota(jnp.int32, (ts, ts), 1)
    same_chunk = (row // cc) == (col // cc)
    causal = jnp.logical_and(same_chunk, row >= col)
    strict = jnp.logical_and(same_chunk, row > col)
    eye = jnp.where(row == col, 1.0, 0.0)
    tri_lower = jnp.where(causal, 1.0, 0.0).astype(BF16)
    tri_upper = jnp.where(jnp.logical_and(same_chunk, row <= col), 1.0, 0.0).astype(BF16)

    units = [(s, h) for s in range(GDN_BATCH) for h in range(GDN_HEADS)]
    q_all, k_all, v_all, beta_all, gc_col_all, gc_row_all = [], [], [], [], [], []
    for s in range(GDN_BATCH):
        q_raw = q_ref[s].astype(F32)
        k_raw = k_ref[s].astype(F32)
        v_raw = v_ref[s].astype(F32)
        q_all.append(conv_silu(q_raw, tail_ref[s, :, 0:qk_w], cwq_ref))
        k_all.append(conv_silu(k_raw, tail_ref[s, :, qk_w:2 * qk_w], cwk_ref))
        v_all.append(conv_silu(v_raw, tail_ref[s, :, 2 * qk_w:], cwv_ref))
        tail_ref[s, :, 0:qk_w] = q_raw[ts - SUBLANES:]
        tail_ref[s, :, qk_w:2 * qk_w] = k_raw[ts - SUBLANES:]
        tail_ref[s, :, 2 * qk_w:] = v_raw[ts - SUBLANES:]

        gates = gates_ref[s]
        x = gates + dtb_ref[...]
        softplus = jnp.maximum(x, 0.0) + jnp.log1p(jnp.exp(-jnp.abs(x)))
        g_all = -jnp.exp(alog_ref[...]) * softplus
        beta_all.append(_sigmoid(gates))
        g_hi = g_all.astype(BF16)
        g_rest = g_all - g_hi.astype(F32)
        g_mid = g_rest.astype(BF16)
        g_lo = (g_rest - g_mid.astype(F32)).astype(BF16)
        gc_col_all.append(sum(_dot(tri_lower, part) for part in (g_hi, g_mid, g_lo)))
        gc_row_all.append(sum(_dot_tn(part, tri_upper) for part in (g_hi, g_mid, g_lo)))

    qn, kn, k16, k_beta, gc, decay, beta = {}, {}, {}, {}, {}, {}, {}
    for u in units:
        s, h = u
        qs = slice(h * GDN_DK, (h + 1) * GDN_DK)
        qh = q_all[s][:, qs]
        kh = k_all[s][:, qs]
        qn[u] = qh * lax.rsqrt(jnp.sum(qh * qh, axis=-1, keepdims=True) + L2_EPS) * (GDN_DK ** -0.5)
        kn[u] = kh * lax.rsqrt(jnp.sum(kh * kh, axis=-1, keepdims=True) + L2_EPS)
        gc[u] = gc_col_all[s][:, h:h + 1]
        gcr = gc_row_all[s][h:h + 1, :]
        decay[u] = jnp.exp(jnp.where(causal, gc[u] - gcr, -jnp.inf))
        beta[u] = beta_all[s][:, GDN_HEADS + h:GDN_HEADS + h + 1]
        k_beta[u] = kn[u] * beta[u]
        k16[u] = kn[u].astype(BF16)
    a_mat = {u: jnp.where(strict, _dot_nt(k_beta[u].astype(BF16), k16[u]) * decay[u], 0.0) for u in units}
    p = {u: eye - a_mat[u] for u in units}
    m16 = {u: a_mat[u].astype(BF16) for u in units}
    for _ in range((cc - 1).bit_length() - 1):
        m16 = {u: _dot(m16[u], m16[u]).astype(BF16) for u in units}
        p = {u: p[u] + _dot(p[u].astype(BF16), m16[u]) for u in units}
    u_all, lhs_state, lhs_vnew, state_decay = {}, {}, {}, {}
    for u in units:
        s, h = u
        vs = slice(h * GDN_DV, (h + 1) * GDN_DV)
        t16 = p[u].astype(BF16)
        egc = jnp.exp(gc[u])
        u_all[u] = _dot(t16, (v_all[s][:, vs] * beta[u]).astype(BF16))
        w_all = _dot(t16, (k_beta[u] * egc).astype(BF16))
        qk = _dot_nt(qn[u].astype(BF16), k16[u]) * decay[u]
        qg = qn[u] * egc
        kn_t = kn[u].T
        gcr = gc_row_all[s][h:h + 1, :]
        for n in range(ts // cc):
            rs = slice(n * cc, (n + 1) * cc)
            g_last = gc[u][(n + 1) * cc - 1:(n + 1) * cc, :]
            kg_t = kn_t[:, rs] * jnp.exp(g_last - gcr[:, rs])
            lhs_state[u, n] = jnp.concatenate([w_all[rs], qg[rs]], axis=0).astype(BF16)
            lhs_vnew[u, n] = jnp.concatenate([qk[rs, rs], kg_t], axis=0).astype(BF16)
            state_decay[u, n] = jnp.exp(g_last)
    state = {u: state_ref[u[0], u[1]] for u in units}
    outs = {u: [] for u in units}
    for n in range(ts // cc):
        rs = slice(n * cc, (n + 1) * cc)
        for u in units:
            from_state = _dot(lhs_state[u, n], state[u].astype(BF16))
            v_new = (u_all[u][rs] - from_state[:cc]).astype(BF16)
            from_vnew = _dot(lhs_vnew[u, n], v_new)
            outs[u].append(from_state[cc:] + from_vnew[:cc])
            state[u] = state[u] * state_decay[u, n] + from_vnew[cc:]
    for u in units:
        s, h = u
        vs = slice(h * GDN_DV, (h + 1) * GDN_DV)
        state_ref[s, h] = state[u]
        o = jnp.concatenate(outs[u], axis=0)
        o = o * lax.rsqrt(jnp.mean(o * o, axis=-1, keepdims=True) + NORM_EPS) * nw_ref[...]
        o = o * _silu(z_ref[s, :, vs].astype(F32))
        o_ref[s, :, vs] = (_sigmoid(gb_ref[s, :, vs].astype(F32)) * o).astype(BF16)


def _gdn(proj3, gates3, alog_row, dtb_row, conv_w, norm_w):
    b, s, _ = proj3.shape
    ts = SEQ_TILE
    nb = GDN_BATCH
    assert b % nb == 0
    qk_w = GDN_HEADS * GDN_DK
    v_w = GDN_HEADS * GDN_DV
    return pl.pallas_call(
        _gdn_kernel,
        grid=(b // nb, s // ts),
        in_specs=[
            pl.BlockSpec((nb, ts, qk_w), lambda i, j: (i, j, COL_GQ // qk_w)),
            pl.BlockSpec((nb, ts, qk_w), lambda i, j: (i, j, COL_GK // qk_w)),
            pl.BlockSpec((nb, ts, v_w), lambda i, j: (i, j, COL_GV // v_w)),
            pl.BlockSpec((nb, ts, v_w), lambda i, j: (i, j, COL_GZ // v_w)),
            pl.BlockSpec((nb, ts, v_w), lambda i, j: (i, j, COL_GATE_B // v_w)),
            pl.BlockSpec((nb, ts, LANES), lambda i, j: (i, j, 0)),
            pl.BlockSpec((1, LANES), lambda i, j: (0, 0)),
            pl.BlockSpec((1, LANES), lambda i, j: (0, 0)),
            pl.BlockSpec((CONV_K, qk_w), lambda i, j: (0, 0)),
            pl.BlockSpec((CONV_K, qk_w), lambda i, j: (0, 1)),
            pl.BlockSpec((CONV_K, v_w), lambda i, j: (0, 1)),
            pl.BlockSpec((1, GDN_DV), lambda i, j: (0, 0)),
        ],
        out_specs=pl.BlockSpec((nb, ts, v_w), lambda i, j: (i, j, 0)),
        out_shape=jax.ShapeDtypeStruct((b, s, v_w), BF16),
        scratch_shapes=[
            pltpu.VMEM((nb, GDN_HEADS, GDN_DK, GDN_DV), F32),
            pltpu.VMEM((nb, SUBLANES, 2 * qk_w + v_w), F32),
        ],
        compiler_params=_params("arbitrary", "arbitrary"),
        name="gdn",
    )(proj3, proj3, proj3, proj3, proj3, gates3, alog_row, dtb_row, conv_w, conv_w, conv_w,
      norm_w.reshape(1, GDN_DV))


def _route(logits):
    lane = lax.broadcasted_iota(jnp.int32, logits.shape, 1)
    neg = jnp.float32(-jnp.inf)

    def first_max(v):
        top = jnp.max(v, axis=-1, keepdims=True)
        return top, jnp.min(jnp.where(v == top, lane, LANES), axis=-1, keepdims=True)

    gl = jnp.where(lane < N_GROUPS, logits, neg)
    g_max, g_idx = first_max(gl)
    g_top = 1.0 / jnp.sum(jnp.exp(gl - g_max), axis=-1, keepdims=True)
    lo = N_GROUPS + EXPERTS_PER_GROUP * g_idx
    el = jnp.where(jnp.logical_and(lane >= lo, lane < lo + EXPERTS_PER_GROUP), logits, neg)
    top1, i1 = first_max(el)
    top2, i2 = first_max(jnp.where(lane == i1, neg, el))
    gap = jnp.exp(top2 - top1)
    w1 = 1.0 / (1.0 + gap)
    comb = jnp.where(lane == i1, g_top * w1, jnp.where(lane == i2, g_top * (gap * w1), 0.0))
    return jnp.where(lane == 0, g_idx.astype(F32), comb)


def _outproj_kernel(ya_ref, yb_ref, x_ref, w_ref, gate_ref, nw_ref, shift_ref, scale_ref, wr_ref, wrh_ref, br_ref,
                    h_ref, xf_ref, comb_ref):
    tm = x_ref.shape[0]
    halves = [slice(k * (tm // 4), (k + 1) * (tm // 4)) for k in range(4)]
    hs, xfs, logits = [], [], []
    for rs in halves:
        merged = ya_ref[rs, :] + yb_ref[rs, :]
        hs.append(x_ref[rs, :] + gate_ref[...] * _dot(merged, w_ref[...]))
    for rs, h in zip(halves, hs):
        h_ref[rs, :] = h
        y = h * lax.rsqrt(jnp.mean(h * h, axis=-1, keepdims=True) + NORM_EPS) * nw_ref[...]
        xfs.append(y * (1.0 + scale_ref[...]) + shift_ref[...])
    for rs, xf in zip(halves, xfs):
        xf_hi = xf.astype(BF16)
        xf_ref[rs, :] = xf_hi
        xf_lo = (xf - xf_hi.astype(F32)).astype(BF16)
        both = _dot(xf_hi, wr_ref[...])
        logits.append(both[:, :LANES] + both[:, LANES:] + _dot(xf_lo, wrh_ref[...]) + br_ref[...])
    for rs, lg in zip(halves, logits):
        comb_ref[rs, :] = _route(lg)


def _outproj(ya2, yb2, x2, w_out_b, mod4, norm_w, w_route, b_route, seq, tm=1024):
    t, d = x2.shape
    per_b = seq // tm
    row_spec = pl.BlockSpec((tm, d), lambda i: (i, 0))
    wr_hi = w_route.astype(BF16)
    wr_lo = (w_route - wr_hi.astype(F32)).astype(BF16)

    def mod_spec(k):
        return pl.BlockSpec((None, None, 1, d), lambda i: (i // per_b, k, 0, 0))

    return pl.pallas_call(
        _outproj_kernel,
        grid=(t // tm,),
        in_specs=[
            row_spec, row_spec, row_spec,
            pl.BlockSpec((d, d), lambda i: (0, 0)),
            mod_spec(2),
            pl.BlockSpec((1, d), lambda i: (0, 0)),
            mod_spec(3),
            mod_spec(4),
            pl.BlockSpec((d, 2 * LANES), lambda i: (0, 0)),
            pl.BlockSpec((d, LANES), lambda i: (0, 0)),
            pl.BlockSpec((1, LANES), lambda i: (0, 0)),
        ],
        out_specs=[row_spec, row_spec, pl.BlockSpec((tm, LANES), lambda i: (i, 0))],
        out_shape=[
            jax.ShapeDtypeStruct((t, d), F32),
            jax.ShapeDtypeStruct((t, d), BF16),
            jax.ShapeDtypeStruct((t, LANES), F32),
        ],
        compiler_params=_params("arbitrary"),
        name="outproj",
    )(ya2, yb2, x2, w_out_b, mod4, norm_w.reshape(1, d), mod4, mod4,
      jnp.concatenate([wr_hi, wr_lo], axis=1), wr_hi, b_route)


def _moe_kernel(xf_ref, comb_ref, wg_ref, wu_ref, wd_ref, h_ref, gate_ref, nw_ref, o_ref,
                xs_ref, cs_ref, acc_ref, pos_ref, ltri_ref, meta_ref):
    i = pl.program_id(0)
    p = pl.program_id(1)
    tm = MOE_TILE
    steps_per_group = EXPERTS_PER_GROUP // MOE_EXPERTS_PER_STEP
    g = p // steps_per_group

    @pl.when(jnp.logical_and(i == 0, p == 0))
    def _():
        row = lax.broadcasted_iota(jnp.int32, (tm, tm), 0)
        col = lax.broadcasted_iota(jnp.int32, (tm, tm), 1)
        ltri_ref[...] = jnp.where(row >= col, 1.0, 0.0).astype(BF16)
        xs_ref[MOE_SORTED:, :] = jnp.zeros((MOE_ROWS - MOE_SORTED, xs_ref.shape[1]), BF16)
        cs_ref[MOE_SORTED:, :] = jnp.zeros((MOE_ROWS - MOE_SORTED, LANES), F32)
        acc_ref[MOE_SORTED:, :] = jnp.zeros((MOE_ROWS - MOE_SORTED, acc_ref.shape[1]), F32)

    @pl.when(p == 0)
    def _():
        comb = comb_ref[...]
        lane = lax.broadcasted_iota(jnp.int32, (tm, LANES), 1)
        gidx = comb[:, 0:1]
        mine = jnp.logical_and(lane.astype(F32) == gidx, lane < N_GROUPS)
        csum = _dot(ltri_ref[...], jnp.where(mine, 1.0, 0.0).astype(BF16))
        counts = jnp.broadcast_to(csum[tm - 1:tm, :], (SUBLANES, LANES))
        aligned = jnp.floor((counts + (MOE_ROW_ALIGN - 0.5)) * (1.0 / MOE_ROW_ALIGN)) * MOE_ROW_ALIGN
        lr = lax.broadcasted_iota(jnp.int32, (LANES, LANES), 0)
        lc = lax.broadcasted_iota(jnp.int32, (LANES, LANES), 1)
        seg_start = _dot(aligned, jnp.where(lr < lc, 1.0, 0.0), precision=HIGHEST)
        pos = jnp.sum(jnp.where(mine, seg_start[0:1, :] + csum - 1.0, 0.0), axis=-1, keepdims=True)
        pos_b = jnp.broadcast_to(pos, (tm, LANES))
        pos_ref[...] = pos_b.astype(jnp.int32)
        pos_row = pos_b.T[0:1, :].astype(jnp.int32)
        lane1 = lax.broadcasted_iota(jnp.int32, (SUBLANES, LANES), 1)
        for k in range(N_GROUPS):
            meta_ref[k] = jnp.sum(jnp.where(lane1 == k, counts, 0.0)[0:1, :]).astype(jnp.int32)
            meta_ref[N_GROUPS + k] = jnp.sum(jnp.where(lane1 == k, seg_start, 0.0)[0:1, :]).astype(jnp.int32)

        c_hi = comb.astype(BF16)
        c_lo = (comb - c_hi.astype(F32)).astype(BF16)
        sub = lax.broadcasted_iota(jnp.int32, (MOE_SORTED, tm), 0)
        perm = jnp.where(pos_row == sub, 1.0, 0.0).astype(BF16)
        xs_ref[0:MOE_SORTED, :] = _dot(perm, xf_ref[...]).astype(BF16)
        both = _dot(perm, jnp.concatenate([c_hi, c_lo], axis=1))
        cs_ref[0:MOE_SORTED, :] = both[:, :LANES] + both[:, LANES:]
        acc_ref[0:MOE_SORTED, :] = jnp.zeros((MOE_SORTED, acc_ref.shape[1]), F32)

    def expert_piece(start, m):
        rows = pl.ds(pl.multiple_of(start, MOE_ROW_ALIGN), m)
        xs = xs_ref[rows, :]
        cs = cs_ref[rows, :]
        lane_m = lax.broadcasted_iota(jnp.int32, (m, LANES), 1)
        contrib = None
        for j in range(MOE_EXPERTS_PER_STEP):
            e = p * MOE_EXPERTS_PER_STEP + j
            wcol = jnp.sum(jnp.where(lane_m == N_GROUPS + e, cs, 0.0), axis=-1, keepdims=True)
            act = _silu(_dot(xs, wg_ref[j])) * _dot(xs, wu_ref[j]) * wcol
            part = _dot(act.astype(BF16), wd_ref[j])
            contrib = part if contrib is None else contrib + part
        acc_ref[rows, :] += contrib

    count = meta_ref[g]
    seg = meta_ref[N_GROUPS + g]
    n_full = jnp.maximum(count - 1, 0) // MOE_PIECE

    def full_piece(k, carry):
        expert_piece(seg + k * MOE_PIECE, MOE_PIECE)
        return carry

    lax.fori_loop(0, n_full, full_piece, 0)
    last_start = seg + n_full * MOE_PIECE
    last_rows = count - n_full * MOE_PIECE
    for m in range(MOE_MIN_PIECE, MOE_PIECE + 1, MOE_BUCKET):
        lo = 0 if m == MOE_MIN_PIECE else m - MOE_BUCKET

        @pl.when(jnp.logical_and(last_rows > lo, last_rows <= m))
        def _(m=m):
            expert_piece(last_start, m)

    @pl.when(p == pl.num_programs(1) - 1)
    def _():
        pos = pos_ref[:, 0:1]
        lane_s = lax.broadcasted_iota(jnp.int32, (tm, MOE_SORTED), 1)
        inv = jnp.where(pos == lane_s, 1.0, 0.0).astype(BF16)
        moe = _dot(inv, acc_ref[0:MOE_SORTED, :].astype(BF16))
        h = h_ref[...] + gate_ref[...] * moe
        o_ref[...] = h * lax.rsqrt(jnp.mean(h * h, axis=-1, keepdims=True) + NORM_EPS) * nw_ref[...]


def _moe(xf2, comb, wg, wu, wd, h2, mod4, norm_out_w, seq):
    t, d = xf2.shape
    f = wg.shape[-1]
    tm = MOE_TILE
    per_b = seq // tm
    eps = MOE_EXPERTS_PER_STEP
    row_spec = pl.BlockSpec((tm, d), lambda i, p: (i, 0))
    return pl.pallas_call(
        _moe_kernel,
        grid=(t // tm, N_EXPERTS // eps),
        in_specs=[
            row_spec,
            pl.BlockSpec((tm, LANES), lambda i, p: (i, 0)),
            pl.BlockSpec((eps, d, f), lambda i, p: (p, 0, 0)),
            pl.BlockSpec((eps, d, f), lambda i, p: (p, 0, 0)),
            pl.BlockSpec((eps, f, d), lambda i, p: (p, 0, 0)),
            row_spec,
            pl.BlockSpec((None, None, 1, d), lambda i, p: (i // per_b, 5, 0, 0)),
            pl.BlockSpec((1, d), lambda i, p: (0, 0)),
        ],
        out_specs=row_spec,
        out_shape=jax.ShapeDtypeStruct((t, d), F32),
        scratch_shapes=[
            pltpu.VMEM((MOE_ROWS, d), BF16),
            pltpu.VMEM((MOE_ROWS, LANES), F32),
            pltpu.VMEM((MOE_ROWS, d), F32),
            pltpu.VMEM((tm, LANES), jnp.int32),
            pltpu.VMEM((tm, tm), BF16),
            pltpu.SMEM((2 * N_GROUPS,), jnp.int32),
        ],
        compiler_params=pltpu.CompilerParams(dimension_semantics=("arbitrary", "arbitrary"),
                                             vmem_limit_bytes=MOE_VMEM_LIMIT),
        name="moe",
    )(xf2, comb, wg, wu, wd, h2, mod4, norm_out_w.reshape(1, d))


def _pad_lanes(a):
    return jnp.pad(a, ((0, 0), (0, LANES - a.shape[1])))


def _layer(h3, c, mod_w, mod_b, norm_mix_w, w_in, conv_w, a_log, dt_bias, gdn_norm_w, w_out, norm_ffn_w,
           w_group, b_group, w_router, b_router, w_gate, w_up, w_down, norm_out_w):
    b, s, d = h3.shape
    t = b * s
    x2 = h3.reshape(t, d)

    n_gate_cols = 2 * GDN_HEADS
    small_lo = COL_GATE_A
    w_main = jnp.concatenate([w_in[:, :small_lo], w_in[:, small_lo + n_gate_cols:]], axis=1).astype(BF16)
    w_small = _pad_lanes(w_in[:, small_lo:small_lo + n_gate_cols])
    w_route = _pad_lanes(jnp.concatenate([w_group, w_router], axis=1))
    b_route = _pad_lanes(jnp.concatenate([b_group, b_router.reshape(-1)])[None, :])
    alog_row = _pad_lanes(a_log[None, :])
    dtb_row = _pad_lanes(dt_bias[None, :])
    f = w_gate.shape[-1]
    wg = w_gate.reshape(N_EXPERTS, d, f).astype(BF16)
    wu = w_up.reshape(N_EXPERTS, d, f).astype(BF16)
    wd = w_down.reshape(N_EXPERTS, f, d).astype(BF16)

    half = RET_DK // 2
    inv_freq = 1.0 / (ROPE_BASE ** (jnp.arange(half, dtype=F32) / half))
    ang = jnp.arange(s, dtype=F32)[:, None] * inv_freq[None, :]
    cos_t = jnp.concatenate([jnp.cos(ang), jnp.cos(ang)], axis=1)
    sin_t = jnp.concatenate([-jnp.sin(ang), jnp.sin(ang)], axis=1)
    log_gamma = jnp.log(1.0 - 2.0 ** (-5.0 - jnp.arange(RET_HEADS, dtype=F32)))
    log_gamma = jnp.broadcast_to(log_gamma[:, None], (RET_HEADS, RET_DV))

    mod4 = _mod(c, mod_w, mod_b).reshape(b, N_MOD, 1, d)
    proj, gates = _inproj(x2, mod4, norm_mix_w, w_main, w_small, s)
    proj3 = proj.reshape(b, s, N_MAIN)
    ya = _retention(proj3, cos_t, sin_t, log_gamma)
    yb = _gdn(proj3, gates.reshape(b, s, LANES), alog_row, dtb_row, conv_w, gdn_norm_w)
    h2, xf2, comb = _outproj(ya.reshape(t, d), yb.reshape(t, d), x2, w_out.astype(BF16), mod4, norm_ffn_w,
                             w_route, b_route, s)
    return _moe(xf2, comb, wg, wu, wd, h2, mod4, norm_out_w, s).reshape(b, s, d)


def kernel(x, c, mod_w, mod_b, norm_mix_w, w_in, gdn_conv_w, gdn_a_log, gdn_dt_bias, gdn_norm_w, w_out, norm_ffn_w,
           w_group, b_group, w_router, b_router, w_gate, w_up, w_down, norm_out_w):
    assert mod_w.shape[0] == 1, "one residual layer"
    return _layer(x, c, mod_w[0], mod_b[0], norm_mix_w[0], w_in[0], gdn_conv_w[0], gdn_a_log[0], gdn_dt_bias[0],
                  gdn_norm_w[0], w_out[0], norm_ffn_w[0], w_group[0], b_group[0], w_router[0], b_router[0],
                  w_gate[0], w_up[0], w_down[0], norm_out_w)
```
